```python
import jax, jax.numpy as jnp
from jax import lax
import numpy as np

D_MODEL = 1024
BATCH = 2
SEQ = 8192
DEPTH = 1

HEAD_DIM = 64
N_HEADS_DSA = 8
N_HEADS_FOX = 8
WIDTH_DSA = N_HEADS_DSA * HEAD_DIM
WIDTH_FOX = N_HEADS_FOX * HEAD_DIM
N_IDX_HEADS = 4
IDX_DIM = 64
TOPK_MAX = 256
ROPE_THETA = 500000.0
ROPE_DIM = HEAD_DIM // 4
BLOCK_Q = 128
RMS_EPS = 1e-6
NEG = -1e30

SPLIT_SIZES = (
    WIDTH_DSA, WIDTH_DSA, WIDTH_DSA, WIDTH_DSA,
    N_IDX_HEADS * IDX_DIM, IDX_DIM, N_IDX_HEADS,
    WIDTH_FOX, WIDTH_FOX, WIDTH_FOX, WIDTH_FOX,
    N_HEADS_FOX,
    2 * D_MODEL,
)
N_IN = int(sum(SPLIT_SIZES))
SPLIT_POINTS = tuple(int(v) for v in np.cumsum(SPLIT_SIZES)[:-1])

kernel_name = "hybrid_dsa_fox_gated_parallel"


def rmsnorm(x, gain):
    x32 = x.astype(jnp.float32)
    y = x32 * lax.rsqrt(jnp.mean(x32 * x32, axis=-1, keepdims=True) + RMS_EPS)
    return (y * gain.astype(jnp.float32)).astype(x.dtype)


def rope_partial(t, positions):
    half = ROPE_DIM // 2
    inv_freq = ROPE_THETA ** (-jnp.arange(half, dtype=jnp.float32) * 2.0 / ROPE_DIM)
    ang = positions.astype(jnp.float32)[..., None] * inv_freq
    if t.ndim == 4:
        ang = ang[:, :, None, :]
    cos, sin = jnp.cos(ang), jnp.sin(ang)
    t32 = t.astype(jnp.float32)
    x1, x2, rest = t32[..., :half], t32[..., half:ROPE_DIM], t32[..., ROPE_DIM:]
    out = jnp.concatenate([x1 * cos - x2 * sin, x2 * cos + x1 * sin, rest], axis=-1)
    return out.astype(t.dtype)


def dsa_attention(q, k, v, q_idx, k_idx, w_idx):
    B, S, H, Dh = q.shape
    top_k = min(TOPK_MAX, S // 4)
    n_blocks = S // BLOCK_Q
    key_pos = jnp.arange(S)
    scale = Dh ** -0.5

    def block(i):
        start = i * BLOCK_Q
        qb = lax.dynamic_slice_in_dim(q, start, BLOCK_Q, axis=1)
        qib = lax.dynamic_slice_in_dim(q_idx, start, BLOCK_Q, axis=1)
        wib = lax.dynamic_slice_in_dim(w_idx, start, BLOCK_Q, axis=1)
        q_pos = start + jnp.arange(BLOCK_Q)
        causal = key_pos[None, :] <= q_pos[:, None]
        rel = jax.nn.relu(jnp.einsum('bqhd,bsd->bqhs', qib, k_idx).astype(jnp.float32))
        score = jnp.einsum('bqh,bqhs->bqs', wib.astype(jnp.float32), rel)
        score = jnp.where(causal[None], score, -jnp.inf)
        _, idx = lax.top_k(score, top_k)
        gather = jax.vmap(lambda arr, ii: arr[ii])
        k_sel = gather(k, idx)
        v_sel = gather(v, idx)
        logits = jnp.einsum('bqhd,bqkhd->bhqk', qb, k_sel).astype(jnp.float32) * scale
        valid = idx <= q_pos[None, :, None]
        logits = jnp.where(valid[:, None], logits, NEG)
        p = jax.nn.softmax(logits, axis=-1).astype(v.dtype)
        return jnp.einsum('bhqk,bqkhd->bqhd', p, v_sel)

    out = lax.map(block, jnp.arange(n_blocks))
    return out.transpose(1, 0, 2, 3, 4).reshape(B, S, H, Dh)


def fox_attention(q, k, v, log_f):
    B, S, H, Dh = q.shape
    n_blocks = S // BLOCK_Q
    key_pos = jnp.arange(S)
    scale = Dh ** -0.5
    cum = jnp.cumsum(log_f, axis=1).transpose(0, 2, 1)

    def block(i):
        start = i * BLOCK_Q
        qb = lax.dynamic_slice_in_dim(q, start, BLOCK_Q, axis=1)
        cq = lax.dynamic_slice_in_dim(cum, start, BLOCK_Q, axis=2)
        q_pos = start + jnp.arange(BLOCK_Q)
        causal = key_pos[None, :] <= q_pos[:, None]
        logits = jnp.einsum('bqhd,bshd->bhqs', qb, k).astype(jnp.float32) * scale
        logits = logits + (cq[..., :, None] - cum[..., None, :])
        logits = jnp.where(causal[None, None], logits, NEG)
        p = jax.nn.softmax(logits, axis=-1).astype(v.dtype)
        return jnp.einsum('bhqs,bshd->bqhd', p, v)

    out = lax.map(block, jnp.arange(n_blocks))
    return out.transpose(1, 0, 2, 3, 4).reshape(B, S, H, Dh)


def setup_inputs(seed: int = 0) -> dict:
    key = jax.random.key(seed)
    ks = jax.random.split(key, 10)
    x = jax.random.normal(ks[0], (BATCH, SEQ, D_MODEL), jnp.float32)
    positions = jnp.broadcast_to(jnp.arange(SEQ, dtype=jnp.int32), (BATCH, SEQ))
    norm_gain = 1.0 + 0.02 * jax.random.normal(ks[1], (DEPTH, D_MODEL), jnp.float32)
    w_in = jax.random.normal(ks[2], (DEPTH, D_MODEL, N_IN), jnp.float32) * D_MODEL ** -0.5
    b_forget = 2.0 + 0.1 * jax.random.normal(ks[3], (DEPTH, N_HEADS_FOX), jnp.float32)
    b_merge = 0.01 * jax.random.normal(ks[4], (DEPTH, 2 * D_MODEL), jnp.float32)
    w_branch_dsa = jax.random.normal(ks[5], (DEPTH, WIDTH_DSA, D_MODEL), jnp.float32) * WIDTH_DSA ** -0.5
    w_branch_fox = jax.random.normal(ks[6], (DEPTH, WIDTH_FOX, D_MODEL), jnp.float32) * WIDTH_FOX ** -0.5
    w_out = jax.random.normal(ks[7], (DEPTH, D_MODEL, D_MODEL), jnp.float32) * D_MODEL ** -0.5
    final_gain = 1.0 + 0.02 * jax.random.normal(ks[8], (D_MODEL,), jnp.float32)
    return {"x": x, "positions": positions, "norm_gain": norm_gain, "w_in": w_in,
            "b_forget": b_forget, "b_merge": b_merge, "w_branch_dsa": w_branch_dsa,
            "w_branch_fox": w_branch_fox, "w_out": w_out, "final_gain": final_gain}


def reference(x, positions, norm_gain, w_in, b_forget, b_merge, w_branch_dsa,
              w_branch_fox, w_out, final_gain):
    B, S, _ = x.shape
    for l in range(DEPTH):
        h = rmsnorm(x, norm_gain[l])
        proj = jnp.einsum('bsd,dn->bsn', h, w_in[l])
        (a_q, a_k, a_v, a_gate, i_q, i_k, i_w,
         f_q, f_k, f_v, f_gate, f_logit, m_logit) = jnp.split(proj, SPLIT_POINTS, axis=-1)

        a_q = rope_partial(a_q.reshape(B, S, N_HEADS_DSA, HEAD_DIM), positions)
        a_k = rope_partial(a_k.reshape(B, S, N_HEADS_DSA, HEAD_DIM), positions)
        a_v = a_v.reshape(B, S, N_HEADS_DSA, HEAD_DIM)
        i_q = rope_partial(i_q.reshape(B, S, N_IDX_HEADS, IDX_DIM), positions) * (IDX_DIM ** -0.5)
        i_k = rope_partial(i_k, positions)
        i_w = i_w * (N_IDX_HEADS ** -0.5)
        y_a = dsa_attention(a_q, a_k, a_v, i_q, i_k, i_w).reshape(B, S, WIDTH_DSA)
        u_a = jnp.einsum('bsw,wd->bsd', y_a * jax.nn.silu(a_gate), w_branch_dsa[l])

        f_q = f_q.reshape(B, S, N_HEADS_FOX, HEAD_DIM)
        f_k = f_k.reshape(B, S, N_HEADS_FOX, HEAD_DIM)
        f_v = f_v.reshape(B, S, N_HEADS_FOX, HEAD_DIM)
        log_f = jax.nn.log_sigmoid((f_logit + b_forget[l]).astype(jnp.float32))
        y_b = fox_attention(f_q, f_k, f_v, log_f).reshape(B, S, WIDTH_FOX)
        u_b = jnp.einsum('bsw,wd->bsd', y_b * jax.nn.silu(f_gate), w_branch_fox[l])

        gates = jax.nn.sigmoid(m_logit + b_merge[l])
        g_a, g_b = gates[..., :D_MODEL], gates[..., D_MODEL:]
        merged = g_a * u_a + g_b * u_b
        x = x + jnp.einsum('bsd,de->bse', merged, w_out[l])
    return rmsnorm(x, final_gain)
```

```python
import functools

import jax
import jax.numpy as jnp
import numpy as np
from jax import lax
from jax.experimental import pallas as pl
from jax.experimental.pallas import tpu as pltpu

D_MODEL = 1024
HEAD_DIM = 64
N_HEADS = 8
WIDTH = N_HEADS * HEAD_DIM
N_IDX_HEADS = 4
IDX_DIM = 64
TOPK_MAX = 256
ROPE_THETA = 500000.0
ROPE_DIM = HEAD_DIM // 4
ROPE_HALF = ROPE_DIM // 2
RMS_EPS = 1e-6
NEG = -1e30

LANES = 128
HEADS_PER_BLOCK = LANES // HEAD_DIM
N_HEAD_BLOCKS = N_HEADS // HEADS_PER_BLOCK
VMEM_LIMIT = 48 * 1024 * 1024

PROJ_ROWS = 512
ATT_TILE = 256

TINY = float(np.finfo(np.float32).tiny)
ALL_THR = -3.0e38
BIG = 1.0e9
MAX_BISECT = 400

MISC_IW = 0
MISC_CUM = N_IDX_HEADS

_NT = (((1,), (1,)), ((), ()))


def _rms(x, gain):
    ms = jnp.mean(x * x, axis=-1, keepdims=True)
    return x * lax.rsqrt(ms + RMS_EPS) * gain


def _proj_kernel(x_ref, pos_ref, gain_ref, invf_ref, w_ref, bf_ref,
                 aq_ref, ak_ref, av_ref, fq_ref, fk_ref, fv_ref, iq_ref, ik_ref,
                 misc_ref, carry_ref, *, tiles_per_batch):
    tm = x_ref.shape[0]
    h = _rms(x_ref[...], gain_ref[...]).astype(jnp.bfloat16)

    ang = pos_ref[...].astype(jnp.float32) * invf_ref[...]
    cos = jnp.cos(ang)
    sin = jnp.sin(ang)
    d = lax.broadcasted_iota(jnp.int32, (tm, LANES), 1) % HEAD_DIM
    sin_lo = jnp.where(d < ROPE_HALF, -sin, 0.0)
    sin_hi = jnp.where(d >= ROPE_HALF, sin, 0.0)

    def rope(t):
        up = pltpu.roll(t, LANES - ROPE_HALF, 1)
        dn = pltpu.roll(t, ROPE_HALF, 1)
        return t * cos + up * sin_lo + dn * sin_hi

    def emit(out_ref, col0, width, use_rope, scale):
        r = jnp.dot(h, w_ref[:, col0:col0 + width],
                    preferred_element_type=jnp.float32)
        for c in range(width // LANES):
            t = r[:, c * LANES:(c + 1) * LANES]
            if use_rope:
                t = rope(t)
            if scale != 1.0:
                t = t * scale
            out_ref[:, c * LANES:(c + 1) * LANES] = t.astype(out_ref.dtype)

    qk_scale = HEAD_DIM ** -0.5
    emit(aq_ref, 0 * WIDTH, WIDTH, True, qk_scale)
    emit(ak_ref, 1 * WIDTH, WIDTH, True, 1.0)
    emit(av_ref, 2 * WIDTH, WIDTH, False, 1.0)
    emit(fq_ref, 3 * WIDTH, WIDTH, False, qk_scale)
    emit(fk_ref, 4 * WIDTH, WIDTH, False, 1.0)
    emit(fv_ref, 5 * WIDTH, WIDTH, False, 1.0)
    col = 6 * WIDTH
    emit(iq_ref, col, N_IDX_HEADS * IDX_DIM, True, IDX_DIM ** -0.5)
    col += N_IDX_HEADS * IDX_DIM
    emit(ik_ref, col, LANES, True, 1.0)
    col += LANES

    r = jnp.dot(h, w_ref[:, col:col + LANES], preferred_element_type=jnp.float32)
    z = r + bf_ref[...]
    logf = jnp.minimum(z, 0.0) - jnp.log1p(jnp.exp(-jnp.abs(z)))

    @pl.when(pl.program_id(0) % tiles_per_batch == 0)
    def _():
        carry_ref[...] = jnp.zeros_like(carry_ref)

    rows = lax.broadcasted_iota(jnp.int32, (tm, tm), 0)
    cols = lax.broadcasted_iota(jnp.int32, (tm, tm), 1)
    tri = jnp.where(cols <= rows, 1.0, 0.0).astype(jnp.bfloat16)
    hi = logf.astype(jnp.bfloat16)
    rem = logf - hi.astype(jnp.float32)
    mid = rem.astype(jnp.bfloat16)
    lo = (rem - mid.astype(jnp.float32)).astype(jnp.bfloat16)
    cum = (jnp.dot(tri, hi, preferred_element_type=jnp.float32)
           + jnp.dot(tri, mid, preferred_element_type=jnp.float32)
           + jnp.dot(tri, lo, preferred_element_type=jnp.float32)
           + carry_ref[...])
    carry_ref[...] = cum[tm - 1:tm, :]

    lane = lax.broadcasted_iota(jnp.int32, (tm, LANES), 1)
    misc = jnp.where(lane < MISC_CUM, r * (N_IDX_HEADS ** -0.5),
                     jnp.where(lane < MISC_CUM + N_HEADS, cum, 0.0))
    misc_ref[...] = misc


def _project(x2, pos2, gain, invf, w_all, bf_row, seq):
    m = x2.shape[0]
    tm = PROJ_ROWS
    n_w = w_all.shape[1]
    row = lambda i: (i, 0)
    fixed = lambda i: (0, 0)
    bf16 = jnp.bfloat16
    out_shape = (
        [jax.ShapeDtypeStruct((m, WIDTH), bf16)] * 6
        + [jax.ShapeDtypeStruct((m, N_IDX_HEADS * IDX_DIM), bf16),
           jax.ShapeDtypeStruct((m, LANES), bf16),
           jax.ShapeDtypeStruct((m, LANES), jnp.float32)])
    out_specs = (
        [pl.BlockSpec((tm, WIDTH), row)] * 6
        + [pl.BlockSpec((tm, N_IDX_HEADS * IDX_DIM), row),
           pl.BlockSpec((tm, LANES), row),
           pl.BlockSpec((tm, LANES), row)])
    return pl.pallas_call(
        functools.partial(_proj_kernel, tiles_per_batch=seq // tm),
        grid=(m // tm,),
        in_specs=[
            pl.BlockSpec((tm, D_MODEL), row),
            pl.BlockSpec((tm, 1), row),
            pl.BlockSpec((1, D_MODEL), fixed),
            pl.BlockSpec((1, LANES), fixed),
            pl.BlockSpec((D_MODEL, n_w), fixed),
            pl.BlockSpec((1, LANES), fixed),
        ],
        out_specs=out_specs,
        out_shape=out_shape,
        scratch_shapes=[pltpu.VMEM((1, LANES), jnp.float32)],
        compiler_params=pltpu.CompilerParams(
            dimension_semantics=("arbitrary",), vmem_limit_bytes=VMEM_LIMIT),
        name="in_proj",
    )(x2, pos2, gain, invf, w_all, bf_row)


def _half_select(x, first):
    lane = lax.broadcasted_iota(jnp.int32, x.shape, x.ndim - 1)
    keep = (lane < HEAD_DIM) if first else (lane >= HEAD_DIM)
    return jnp.where(keep, x, jnp.zeros_like(x))


def _softmax_step(s, vj, m, l, acc):
    m_new = jnp.maximum(m, jnp.max(s, axis=1, keepdims=True))
    alpha = jnp.exp(m - m_new)
    p = jnp.exp(s - m_new)
    l = alpha * l + jnp.sum(p, axis=1, keepdims=True)
    acc = alpha * acc + jnp.dot(p.astype(jnp.bfloat16), vj,
                                preferred_element_type=jnp.float32)
    return m_new, l, acc


def _dsa_kernel(iq_ref, ik_ref, misc_ref, q_ref, k_ref, v_ref, o_ref, sc_ref):
    t = ATT_TILE
    i = pl.program_id(1)
    n_tiles = i + 1
    top_k = float(TOPK_MAX)

    def tile(j):
        return pl.ds(pl.multiple_of(j * t, t), t)

    @pl.when(pl.program_id(2) == 0)
    def _index():
        iq = iq_ref[0]
        qh = [_half_select(iq[:, (h // 2) * LANES:(h // 2 + 1) * LANES], h % 2 == 0)
              for h in range(N_IDX_HEADS)]
        wh = [misc_ref[0, :, MISC_IW + h:MISC_IW + h + 1] for h in range(N_IDX_HEADS)]
        row = i * t + lax.broadcasted_iota(jnp.int32, (t, t), 0)

        def score_tile(j, carry):
            rmax, rmin = carry
            kj = ik_ref[0, tile(j), :]
            sc = jnp.zeros((t, t), jnp.float32)
            for h in range(N_IDX_HEADS):
                rel = lax.dot_general(qh[h], kj, _NT,
                                      preferred_element_type=jnp.float32)
                sc = sc + wh[h] * jnp.maximum(rel, 0.0)
            causal = (j * t + lax.broadcasted_iota(jnp.int32, (t, t), 1)) <= row
            sc_ref[:, tile(j)] = jnp.where(causal, sc, -jnp.inf)
            rmax = jnp.maximum(
                rmax, jnp.max(jnp.where(causal, sc, -jnp.inf), axis=1, keepdims=True))
            rmin = jnp.minimum(
                rmin, jnp.min(jnp.where(causal, sc, jnp.inf), axis=1, keepdims=True))
            return rmax, rmin

        rmax, rmin = lax.fori_loop(
            0, n_tiles, score_tile,
            (jnp.full((t, 1), -jnp.inf, jnp.float32),
             jnp.full((t, 1), jnp.inf, jnp.float32)))

        def count_ge(thr):
            def body(j, acc):
                hit = jnp.where(sc_ref[:, tile(j)] >= thr, 1.0, 0.0)
                for c in range(t // LANES):
                    acc = acc + hit[:, c * LANES:(c + 1) * LANES]
                return acc
            acc = lax.fori_loop(0, n_tiles, body, jnp.zeros((t, LANES), jnp.float32))
            return jnp.sum(acc, axis=1, keepdims=True)

        n_causal = (i * t + 1 + lax.broadcasted_iota(jnp.int32, (t, 1), 0)
                    ).astype(jnp.float32)
        c_pos = count_ge(jnp.full((t, 1), TINY, jnp.float32))
        c_zero = count_ge(jnp.zeros((t, 1), jnp.float32))
        take_all = n_causal <= top_k
        tie_zero = (c_pos < top_k) & (c_zero >= top_k)
        positive = c_pos >= top_k
        done = jnp.where(take_all | tie_zero | (c_pos == top_k), 1.0, 0.0)
        thr = jnp.where(take_all, ALL_THR, jnp.where(tie_zero, 0.0, TINY))
        take = jnp.where(tie_zero & jnp.logical_not(take_all), top_k - c_pos, BIG)
        lo = jnp.where(positive, TINY, rmin)
        hi = jnp.where(positive, 2.0 * rmax + 1.0, 0.0)
        c_hi = jnp.where(positive, 0.0, c_zero)

        def cond(state):
            it, _, _, _, _, _, done = state
            return (it < MAX_BISECT) & (jnp.min(done) < 0.5)

        def step(state):
            it, lo, hi, c_hi, thr, take, done = state
            mid = 0.5 * lo + 0.5 * hi
            spent = (mid <= lo) | (mid >= hi)
            c = count_ge(mid)
            live = done < 0.5
            finish = live & (spent | (c == top_k))
            thr = jnp.where(finish, jnp.where(spent, lo, mid), thr)
            take = jnp.where(finish, jnp.where(spent, top_k - c_hi, BIG), take)
            move = live & jnp.logical_not(finish)
            up = move & (c > top_k)
            down = move & (c < top_k)
            lo = jnp.where(up, mid, lo)
            hi = jnp.where(down, mid, hi)
            c_hi = jnp.where(down, c, c_hi)
            done = jnp.where(finish, 1.0, done)
            return it + 1, lo, hi, c_hi, thr, take, done

        _, lo, _, c_hi, thr, take, done = lax.while_loop(
            cond, step, (jnp.int32(0), lo, hi, c_hi, thr, take, done))
        thr = jnp.where(done < 0.5, lo, thr)
        take = jnp.where(done < 0.5, top_k - c_hi, take)

        before = (lax.broadcasted_iota(jnp.int32, (t, t), 0)
                  < lax.broadcasted_iota(jnp.int32, (t, t), 1))
        before = jnp.where(before, 1.0, 0.0).astype(jnp.bfloat16)

        def mask_tile(j, seen):
            s = sc_ref[:, tile(j)]
            eq = jnp.where(s == thr, 1.0, 0.0)
            rank = seen + jnp.dot(eq.astype(jnp.bfloat16), before,
                                  preferred_element_type=jnp.float32)
            sel = (s > thr) | ((s == thr) & (rank < take))
            sc_ref[:, tile(j)] = jnp.where(sel, 0.0, NEG)
            return seen + jnp.sum(eq, axis=1, keepdims=True)

        lax.fori_loop(0, n_tiles, mask_tile, jnp.zeros((t, 1), jnp.float32))

    q = q_ref[0]
    outs = []
    for first in (True, False):
        qa = _half_select(q, first)

        def body(j, carry, qa=qa):
            kj = k_ref[0, tile(j), :]
            vj = v_ref[0, tile(j), :]
            s = lax.dot_general(qa, kj, _NT, preferred_element_type=jnp.float32)
            s = s + sc_ref[:, tile(j)]
            return _softmax_step(s, vj, *carry)

        m, l, acc = lax.fori_loop(
            0, n_tiles, body,
            (jnp.full((t, 1), NEG, jnp.float32), jnp.zeros((t, 1), jnp.float32),
             jnp.zeros((t, LANES), jnp.float32)))
        outs.append(acc / l)
    lane = lax.broadcasted_iota(jnp.int32, (t, LANES), 1)
    o_ref[0] = jnp.where(lane < HEAD_DIM, outs[0], outs[1])


def _dsa_attention(iq, ik2, misc, aq, ak, av):
    b, s, _ = aq.shape
    t = ATT_TILE
    return pl.pallas_call(
        _dsa_kernel,
        grid=(b, s // t, N_HEAD_BLOCKS),
        in_specs=[
            pl.BlockSpec((1, t, N_IDX_HEADS * IDX_DIM), lambda bb, i, hp: (bb, i, 0)),
            pl.BlockSpec((1, s, LANES), lambda bb, i, hp: (bb, 0, 0)),
            pl.BlockSpec((1, t, LANES), lambda bb, i, hp: (bb, i, 0)),
            pl.BlockSpec((1, t, LANES), lambda bb, i, hp: (bb, i, hp)),
            pl.BlockSpec((1, s, LANES), lambda bb, i, hp: (bb, 0, hp)),
            pl.BlockSpec((1, s, LANES), lambda bb, i, hp: (bb, 0, hp)),
        ],
        out_specs=pl.BlockSpec((1, t, LANES), lambda bb, i, hp: (bb, i, hp)),
        out_shape=jax.ShapeDtypeStruct((b, s, WIDTH), jnp.float32),
        scratch_shapes=[pltpu.VMEM((t, s), jnp.float32)],
        compiler_params=pltpu.CompilerParams(
            dimension_semantics=("arbitrary", "arbitrary", "arbitrary"),
            vmem_limit_bytes=VMEM_LIMIT),
        name="dsa_attention",
    )(iq, ik2, misc, aq, ak, av)


def _fox_kernel(q_ref, k_ref, v_ref, cq_ref, ck_ref, o_ref):
    t = ATT_TILE
    i = pl.program_id(2)

    def tile(j):
        return pl.ds(pl.multiple_of(j * t, t), t)

    q = q_ref[0]
    row = i * t + lax.broadcasted_iota(jnp.int32, (t, t), 0)
    outs = []
    for a, first in enumerate((True, False)):
        qa = _half_select(q, first)
        cq = cq_ref[0, 0, :, a:a + 1]

        def body(j, carry, qa=qa, cq=cq, a=a):
            kj = k_ref[0, tile(j), :]
            vj = v_ref[0, tile(j), :]
            s = lax.dot_general(qa, kj, _NT, preferred_element_type=jnp.float32)
            s = s + (cq - ck_ref[0, 0, a:a + 1, tile(j)])
            causal = (j * t + lax.broadcasted_iota(jnp.int32, (t, t), 1)) <= row
            s = jnp.where(causal, s, NEG)
            return _softmax_step(s, vj, *carry)

        m, l, acc = lax.fori_loop(
            0, i + 1, body,
            (jnp.full((t, 1), NEG, jnp.float32), jnp.zeros((t, 1), jnp.float32),
             jnp.zeros((t, LANES), jnp.float32)))
        outs.append(acc / l)
    lane = lax.broadcasted_iota(jnp.int32, (t, LANES), 1)
    o_ref[0] = jnp.where(lane < HEAD_DIM, outs[0], outs[1])


def _fox_attention(fq, fk, fv, cum_q, cum_k):
    b, s, _ = fq.shape
    t = ATT_TILE
    return pl.pallas_call(
        _fox_kernel,
        grid=(b, N_HEAD_BLOCKS, s // t),
        in_specs=[
            pl.BlockSpec((1, t, LANES), lambda bb, hp, i: (bb, i, hp)),
            pl.BlockSpec((1, s, LANES), lambda bb, hp, i: (bb, 0, hp)),
            pl.BlockSpec((1, s, LANES), lambda bb, hp, i: (bb, 0, hp)),
            pl.BlockSpec((1, 1, t, HEADS_PER_BLOCK), lambda bb, hp, i: (bb, hp, i, 0)),
            pl.BlockSpec((1, 1, HEADS_PER_BLOCK, s), lambda bb, hp, i: (bb, hp, 0, 0)),
        ],
        out_specs=pl.BlockSpec((1, t, LANES), lambda bb, hp, i: (bb, i, hp)),
        out_shape=jax.ShapeDtypeStruct((b, s, WIDTH), jnp.float32),
        compiler_params=pltpu.CompilerParams(
            dimension_semantics=("arbitrary", "arbitrary", "arbitrary"),
            vmem_limit_bytes=VMEM_LIMIT),
        name="fox_attention",
    )(fq, fk, fv, cum_q, cum_k)


def _out_kernel(x_ref, ya_ref, yb_ref, gain_ref, wg_ref, bm_ref, wa_ref, wb_ref,
                wo_ref, ngain_ref, o_ref):
    x = x_ref[...]
    h = _rms(x, gain_ref[...]).astype(jnp.bfloat16)
    g = jnp.dot(h, wg_ref[...], preferred_element_type=jnp.float32)
    a_gate = g[:, :WIDTH]
    f_gate = g[:, WIDTH:2 * WIDTH]
    m_logit = g[:, 2 * WIDTH:] + bm_ref[...]
    za = (ya_ref[...] * (a_gate * jax.nn.sigmoid(a_gate))).astype(jnp.bfloat16)
    zb = (yb_ref[...] * (f_gate * jax.nn.sigmoid(f_gate))).astype(jnp.bfloat16)
    ua = jnp.dot(za, wa_ref[...], preferred_element_type=jnp.float32)
    ub = jnp.dot(zb, wb_ref[...], preferred_element_type=jnp.float32)
    gates = jax.nn.sigmoid(m_logit)
    merged = gates[:, :D_MODEL] * ua + gates[:, D_MODEL:] * ub
    y = x + jnp.dot(merged.astype(jnp.bfloat16), wo_ref[...],
                    preferred_element_type=jnp.float32)
    o_ref[...] = _rms(y, ngain_ref[...])


def _output(x2, ya, yb, gain, w_gates, b_merge, w_a, w_b, w_o, next_gain):
    m = x2.shape[0]
    tm = PROJ_ROWS
    row = lambda i: (i, 0)
    fixed = lambda i: (0, 0)
    return pl.pallas_call(
        _out_kernel,
        grid=(m // tm,),
        in_specs=[
            pl.BlockSpec((tm, D_MODEL), row),
            pl.BlockSpec((tm, WIDTH), row),
            pl.BlockSpec((tm, WIDTH), row),
            pl.BlockSpec((1, D_MODEL), fixed),
            pl.BlockSpec(w_gates.shape, fixed),
            pl.BlockSpec((1, 2 * D_MODEL), fixed),
            pl.BlockSpec(w_a.shape, fixed),
            pl.BlockSpec(w_b.shape, fixed),
            pl.BlockSpec(w_o.shape, fixed),
            pl.BlockSpec((1, D_MODEL), fixed),
        ],
        out_specs=pl.BlockSpec((tm, D_MODEL), row),
        out_shape=jax.ShapeDtypeStruct((m, D_MODEL), jnp.float32),
        compiler_params=pltpu.CompilerParams(
            dimension_semantics=("arbitrary",), vmem_limit_bytes=VMEM_LIMIT),
        name="out_proj",
    )(x2, ya, yb, gain, w_gates, b_merge, w_a, w_b, w_o, next_gain)


def _split_w_in(w):
    sizes = (WIDTH, WIDTH, WIDTH, WIDTH, N_IDX_HEADS * IDX_DIM, IDX_DIM, N_IDX_HEADS,
             WIDTH, WIDTH, WIDTH, WIDTH, N_HEADS, 2 * D_MODEL)
    points = np.cumsum(sizes)[:-1]
    return jnp.split(w, points, axis=-1)


def kernel(x, positions, norm_gain, w_in, b_forget, b_merge, w_branch_dsa,
           w_branch_fox, w_out, final_gain):
    b, s, d = x.shape
    depth = w_in.shape[0]
    bf16 = jnp.bfloat16

    half = jnp.arange(ROPE_HALF, dtype=jnp.float32)
    inv_freq = ROPE_THETA ** (-half * 2.0 / ROPE_DIM)
    dim = np.arange(LANES) % HEAD_DIM
    invf = jnp.where(dim < ROPE_DIM, inv_freq[dim % ROPE_HALF], 0.0)[None, :]
    pos2 = positions.reshape(b * s, 1)

    x2 = x.reshape(b * s, d)
    normed = None
    for l in range(depth):
        (w_aq, w_ak, w_av, w_ag, w_iq, w_ik, w_iw,
         w_fq, w_fk, w_fv, w_fg, w_fl, w_m) = _split_w_in(w_in[l])
        pad = jnp.zeros((d, LANES - N_IDX_HEADS - N_HEADS), w_in.dtype)
        w_all = jnp.concatenate(
            [w_aq, w_ak, w_av, w_fq, w_fk, w_fv, w_iq, w_ik, w_ik, w_iw, w_fl, pad],
            axis=1).astype(bf16)
        w_gates = jnp.concatenate([w_ag, w_fg, w_m], axis=1).astype(bf16)
        bf_row = jnp.zeros((1, LANES), jnp.float32).at[
            0, MISC_CUM:MISC_CUM + N_HEADS].set(b_forget[l])
        gain = norm_gain[l][None, :]
        next_gain = (norm_gain[l + 1] if l + 1 < depth else final_gain)[None, :]

        aq, ak, av, fq, fk, fv, iq, ik2, misc = _project(
            x2, pos2, gain, invf, w_all, bf_row, s)
        shape3 = lambda a: a.reshape(b, s, a.shape[-1])
        ya = _dsa_attention(shape3(iq), shape3(ik2), shape3(misc),
                            shape3(aq), shape3(ak), shape3(av))
        cum = misc[:, MISC_CUM:MISC_CUM + N_HEADS].reshape(
            b, s, N_HEAD_BLOCKS, HEADS_PER_BLOCK)
        cum_q = cum.transpose(0, 2, 1, 3)
        cum_k = cum.transpose(0, 2, 3, 1)
        yb = _fox_attention(shape3(fq), shape3(fk), shape3(fv), cum_q, cum_k)
        normed = _output(x2, ya.reshape(b * s, WIDTH), yb.reshape(b * s, WIDTH),
                         gain, w_gates, b_merge[l][None, :],
                         w_branch_dsa[l].astype(bf16), w_branch_fox[l].astype(bf16),
                         w_out[l].astype(bf16), next_gain)
        assert depth == 1, "stacked layers need the un-normalised residual as well"
    return normed.reshape(b, s, d)
```

```python
import functools

import jax
import jax.numpy as jnp
import numpy as np
from jax import lax
from jax.experimental import pallas as pl
from jax.experimental.pallas import tpu as pltpu

D_MODEL = 1024
HEAD_DIM = 64
N_HEADS = 8
WIDTH = N_HEADS * HEAD_DIM
N_IDX_HEADS = 4
IDX_DIM = 64
TOPK_MAX = 256
ROPE_THETA = 500000.0
ROPE_DIM = HEAD_DIM // 4
ROPE_HALF = ROPE_DIM // 2
RMS_EPS = 1e-6
NEG = -1e30

LANES = 128
HEADS_PER_BLOCK = LANES // HEAD_DIM
N_HEAD_BLOCKS = N_HEADS // HEADS_PER_BLOCK
VMEM_LIMIT = 48 * 1024 * 1024

PROJ_ROWS = 512
ATT_TILE = 256
WIDE_KEYS = 1024

TINY = float(np.finfo(np.float32).tiny)
ALL_THR = -3.0e38
BIG = 1.0e9
MAX_BISECT = 400

MISC_IW = 0
MISC_CUM = N_IDX_HEADS

_NT = (((1,), (1,)), ((), ()))


def _rms(x, gain):
    ms = jnp.mean(x * x, axis=-1, keepdims=True)
    return x * lax.rsqrt(ms + RMS_EPS) * gain


def _proj_kernel(x_ref, pos_ref, gain_ref, invf_ref, w_ref, bf_ref,
                 aq_ref, ak_ref, av_ref, fq_ref, fk_ref, fv_ref, iq_ref, ik_ref,
                 misc_ref, carry_ref, *, tiles_per_batch):
    tm = x_ref.shape[0]
    h = _rms(x_ref[...], gain_ref[...]).astype(jnp.bfloat16)

    ang = pos_ref[...].astype(jnp.float32) * invf_ref[...]
    cos = jnp.cos(ang)
    sin = jnp.sin(ang)
    d = lax.broadcasted_iota(jnp.int32, (tm, LANES), 1) % HEAD_DIM
    sin_lo = jnp.where(d < ROPE_HALF, -sin, 0.0)
    sin_hi = jnp.where(d >= ROPE_HALF, sin, 0.0)

    def rope(t):
        up = pltpu.roll(t, LANES - ROPE_HALF, 1)
        dn = pltpu.roll(t, ROPE_HALF, 1)
        return t * cos + up * sin_lo + dn * sin_hi

    def emit(out_ref, col0, width, use_rope, scale):
        r = jnp.dot(h, w_ref[:, col0:col0 + width],
                    preferred_element_type=jnp.float32)
        for c in range(width // LANES):
            t = r[:, c * LANES:(c + 1) * LANES]
            if use_rope:
                t = rope(t)
            if scale != 1.0:
                t = t * scale
            out_ref[:, c * LANES:(c + 1) * LANES] = t.astype(out_ref.dtype)

    qk_scale = HEAD_DIM ** -0.5
    emit(aq_ref, 0 * WIDTH, WIDTH, True, qk_scale)
    emit(ak_ref, 1 * WIDTH, WIDTH, True, 1.0)
    emit(av_ref, 2 * WIDTH, WIDTH, False, 1.0)
    emit(fq_ref, 3 * WIDTH, WIDTH, False, qk_scale)
    emit(fk_ref, 4 * WIDTH, WIDTH, False, 1.0)
    emit(fv_ref, 5 * WIDTH, WIDTH, False, 1.0)
    col = 6 * WIDTH
    emit(iq_ref, col, N_IDX_HEADS * IDX_DIM, True, IDX_DIM ** -0.5)
    col += N_IDX_HEADS * IDX_DIM
    emit(ik_ref, col, LANES, True, 1.0)
    col += LANES

    r = jnp.dot(h, w_ref[:, col:col + LANES], preferred_element_type=jnp.float32)
    z = r + bf_ref[...]
    logf = jnp.minimum(z, 0.0) - jnp.log1p(jnp.exp(-jnp.abs(z)))

    @pl.when(pl.program_id(0) % tiles_per_batch == 0)
    def _():
        carry_ref[...] = jnp.zeros_like(carry_ref)

    rows = lax.broadcasted_iota(jnp.int32, (tm, tm), 0)
    cols = lax.broadcasted_iota(jnp.int32, (tm, tm), 1)
    tri = jnp.where(cols <= rows, 1.0, 0.0).astype(jnp.bfloat16)
    hi = logf.astype(jnp.bfloat16)
    rem = logf - hi.astype(jnp.float32)
    mid = rem.astype(jnp.bfloat16)
    lo = (rem - mid.astype(jnp.float32)).astype(jnp.bfloat16)
    cum = (jnp.dot(tri, hi, preferred_element_type=jnp.float32)
           + jnp.dot(tri, mid, preferred_element_type=jnp.float32)
           + jnp.dot(tri, lo, preferred_element_type=jnp.float32)
           + carry_ref[...])
    carry_ref[...] = cum[tm - 1:tm, :]

    lane = lax.broadcasted_iota(jnp.int32, (tm, LANES), 1)
    misc = jnp.where(lane < MISC_CUM, r * (N_IDX_HEADS ** -0.5),
                     jnp.where(lane < MISC_CUM + N_HEADS, cum, 0.0))
    misc_ref[...] = misc


def _project(x2, pos2, gain, invf, w_all, bf_row, seq):
    m = x2.shape[0]
    tm = PROJ_ROWS
    n_w = w_all.shape[1]
    row = lambda i: (i, 0)
    fixed = lambda i: (0, 0)
    bf16 = jnp.bfloat16
    out_shape = (
        [jax.ShapeDtypeStruct((m, WIDTH), bf16)] * 6
        + [jax.ShapeDtypeStruct((m, N_IDX_HEADS * IDX_DIM), bf16),
           jax.ShapeDtypeStruct((m, LANES), bf16),
           jax.ShapeDtypeStruct((m, LANES), jnp.float32)])
    out_specs = (
        [pl.BlockSpec((tm, WIDTH), row)] * 6
        + [pl.BlockSpec((tm, N_IDX_HEADS * IDX_DIM), row),
           pl.BlockSpec((tm, LANES), row),
           pl.BlockSpec((tm, LANES), row)])
    return pl.pallas_call(
        functools.partial(_proj_kernel, tiles_per_batch=seq // tm),
        grid=(m // tm,),
        in_specs=[
            pl.BlockSpec((tm, D_MODEL), row),
            pl.BlockSpec((tm, 1), row),
            pl.BlockSpec((1, D_MODEL), fixed),
            pl.BlockSpec((1, LANES), fixed),
            pl.BlockSpec((D_MODEL, n_w), fixed),
            pl.BlockSpec((1, LANES), fixed),
        ],
        out_specs=out_specs,
        out_shape=out_shape,
        scratch_shapes=[pltpu.VMEM((1, LANES), jnp.float32)],
        compiler_params=pltpu.CompilerParams(
            dimension_semantics=("arbitrary",), vmem_limit_bytes=VMEM_LIMIT),
        name="in_proj",
    )(x2, pos2, gain, invf, w_all, bf_row)


def _half_select(x, first):
    lane = lax.broadcasted_iota(jnp.int32, x.shape, x.ndim - 1)
    keep = (lane < HEAD_DIM) if first else (lane >= HEAD_DIM)
    return jnp.where(keep, x, jnp.zeros_like(x))


def _stack_heads(q):
    return jnp.concatenate([_half_select(q, True), _half_select(q, False)], axis=0)


def _attend(qs, kt_ref, v_ref, bias, i, s_ref, m_ref, l_ref, acc_ref):
    t = ATT_TILE
    n_wide = (i * t) // WIDE_KEYS
    first_narrow = n_wide * (WIDE_KEYS // t)

    def phase_a(start, width, diagonal):
        keys = pl.ds(pl.multiple_of(start, width), width)
        s = jnp.dot(qs, kt_ref[0, :, keys], preferred_element_type=jnp.float32)
        s = bias(s, keys, width, diagonal)
        s_ref[:, keys] = s
        m = m_ref[...]
        for c in range(width // LANES):
            m = jnp.maximum(m, s[:, c * LANES:(c + 1) * LANES])
        m_ref[...] = m

    def phase_b(start, width, m):
        keys = pl.ds(pl.multiple_of(start, width), width)
        p = jnp.exp(s_ref[:, keys] - m)
        l = l_ref[...]
        for c in range(width // LANES):
            l = l + p[:, c * LANES:(c + 1) * LANES]
        l_ref[...] = l
        acc_ref[...] += jnp.dot(p.astype(jnp.bfloat16), v_ref[0, keys, :],
                                preferred_element_type=jnp.float32)

    def loop(lo, hi, fn):
        def body(j, carry):
            fn(j)
            return carry
        lax.fori_loop(lo, hi, body, 0)

    m_ref[...] = jnp.full(m_ref.shape, NEG, jnp.float32)
    loop(0, n_wide, lambda j: phase_a(j * WIDE_KEYS, WIDE_KEYS, False))
    loop(first_narrow, i, lambda j: phase_a(j * t, t, False))
    phase_a(i * t, t, True)

    m = jnp.max(m_ref[...], axis=1, keepdims=True)
    l_ref[...] = jnp.zeros(l_ref.shape, jnp.float32)
    acc_ref[...] = jnp.zeros(acc_ref.shape, jnp.float32)
    loop(0, n_wide, lambda j: phase_b(j * WIDE_KEYS, WIDE_KEYS, m))
    loop(first_narrow, i + 1, lambda j: phase_b(j * t, t, m))

    o = acc_ref[...] / jnp.sum(l_ref[...], axis=1, keepdims=True)
    lane = lax.broadcasted_iota(jnp.int32, (t, LANES), 1)
    return jnp.where(lane < HEAD_DIM, o[:t], o[t:])


def _attend_scratch(t, s):
    rows = HEADS_PER_BLOCK * t
    return [pltpu.VMEM((rows, s), jnp.float32),
            pltpu.VMEM((rows, LANES), jnp.float32),
            pltpu.VMEM((rows, LANES), jnp.float32),
            pltpu.VMEM((rows, LANES), jnp.float32)]


def _dsa_kernel(iq_ref, ik_ref, misc_ref, q_ref, kt_ref, v_ref, o_ref, sc_ref,
                s_ref, m_ref, l_ref, acc_ref):
    t = ATT_TILE
    i = pl.program_id(1)
    n_tiles = i + 1
    top_k = float(TOPK_MAX)

    def tile(j):
        return pl.ds(pl.multiple_of(j * t, t), t)

    @pl.when(pl.program_id(2) == 0)
    def _index():
        iq = iq_ref[0]
        qh = [_half_select(iq[:, (h // 2) * LANES:(h // 2 + 1) * LANES], h % 2 == 0)
              for h in range(N_IDX_HEADS)]
        wh = [misc_ref[0, :, MISC_IW + h:MISC_IW + h + 1] for h in range(N_IDX_HEADS)]
        row = i * t + lax.broadcasted_iota(jnp.int32, (t, t), 0)

        def score_tile(j, carry):
            rmax, rmin = carry
            kj = ik_ref[0, tile(j), :]
            sc = jnp.zeros((t, t), jnp.float32)
            for h in range(N_IDX_HEADS):
                rel = lax.dot_general(qh[h], kj, _NT,
                                      preferred_element_type=jnp.float32)
                sc = sc + wh[h] * jnp.maximum(rel, 0.0)
            causal = (j * t + lax.broadcasted_iota(jnp.int32, (t, t), 1)) <= row
            sc_ref[:, tile(j)] = jnp.where(causal, sc, -jnp.inf)
            rmax = jnp.maximum(
                rmax, jnp.max(jnp.where(causal, sc, -jnp.inf), axis=1, keepdims=True))
            rmin = jnp.minimum(
                rmin, jnp.min(jnp.where(causal, sc, jnp.inf), axis=1, keepdims=True))
            return rmax, rmin

        rmax, rmin = lax.fori_loop(
            0, n_tiles, score_tile,
            (jnp.full((t, 1), -jnp.inf, jnp.float32),
             jnp.full((t, 1), jnp.inf, jnp.float32)))

        def count_ge(thr):
            def body(j, acc):
                hit = jnp.where(sc_ref[:, tile(j)] >= thr, 1.0, 0.0)
                for c in range(t // LANES):
                    acc = acc + hit[:, c * LANES:(c + 1) * LANES]
                return acc
            acc = lax.fori_loop(0, n_tiles, body, jnp.zeros((t, LANES), jnp.float32))
            return jnp.sum(acc, axis=1, keepdims=True)

        n_causal = (i * t + 1 + lax.broadcasted_iota(jnp.int32, (t, 1), 0)
                    ).astype(jnp.float32)
        c_pos = count_ge(jnp.full((t, 1), TINY, jnp.float32))
        c_zero = count_ge(jnp.zeros((t, 1), jnp.float32))
        take_all = n_causal <= top_k
        tie_zero = (c_pos < top_k) & (c_zero >= top_k)
        positive = c_pos >= top_k
        done = jnp.where(take_all | tie_zero | (c_pos == top_k), 1.0, 0.0)
        thr = jnp.where(take_all, ALL_THR, jnp.where(tie_zero, 0.0, TINY))
        take = jnp.where(tie_zero & jnp.logical_not(take_all), top_k - c_pos, BIG)
        lo = jnp.where(positive, TINY, rmin)
        hi = jnp.where(positive, 2.0 * rmax + 1.0, 0.0)
        c_hi = jnp.where(positive, 0.0, c_zero)

        def cond(state):
            it, _, _, _, _, _, done = state
            return (it < MAX_BISECT) & (jnp.min(done) < 0.5)

        def step(state):
            it, lo, hi, c_hi, thr, take, done = state
            mid = 0.5 * lo + 0.5 * hi
            spent = (mid <= lo) | (mid >= hi)
            c = count_ge(mid)
            live = done < 0.5
            finish = live & (spent | (c == top_k))
            thr = jnp.where(finish, jnp.where(spent, lo, mid), thr)
            take = jnp.where(finish, jnp.where(spent, top_k - c_hi, BIG), take)
            move = live & jnp.logical_not(finish)
            up = move & (c > top_k)
            down = move & (c < top_k)
            lo = jnp.where(up, mid, lo)
            hi = jnp.where(down, mid, hi)
            c_hi = jnp.where(down, c, c_hi)
            done = jnp.where(finish, 1.0, done)
            return it + 1, lo, hi, c_hi, thr, take, done

        _, lo, _, c_hi, thr, take, done = lax.while_loop(
            cond, step, (jnp.int32(0), lo, hi, c_hi, thr, take, done))
        thr = jnp.where(done < 0.5, lo, thr)
        take = jnp.where(done < 0.5, top_k - c_hi, take)

        before = (lax.broadcasted_iota(jnp.int32, (t, t), 0)
                  < lax.broadcasted_iota(jnp.int32, (t, t), 1))
        before = jnp.where(before, 1.0, 0.0).astype(jnp.bfloat16)

        def mask_tile(j, seen):
            s = sc_ref[:, tile(j)]
            eq = jnp.where(s == thr, 1.0, 0.0)
            rank = seen + jnp.dot(eq.astype(jnp.bfloat16), before,
                                  preferred_element_type=jnp.float32)
            sel = (s > thr) | ((s == thr) & (rank < take))
            sc_ref[:, tile(j)] = jnp.where(sel, 0.0, NEG)
            return seen + jnp.sum(eq, axis=1, keepdims=True)

        lax.fori_loop(0, n_tiles, mask_tile, jnp.zeros((t, 1), jnp.float32))

    def bias(s, keys, width, diagonal):
        mask = sc_ref[:, keys]
        return s + jnp.concatenate([mask, mask], axis=0)

    o_ref[0] = _attend(_stack_heads(q_ref[0]), kt_ref, v_ref, bias, i,
                       s_ref, m_ref, l_ref, acc_ref)


def _dsa_attention(iq, ik2, misc, aq, ak_t, av):
    b, s, _ = aq.shape
    t = ATT_TILE
    return pl.pallas_call(
        _dsa_kernel,
        grid=(b, s // t, N_HEAD_BLOCKS),
        in_specs=[
            pl.BlockSpec((1, t, N_IDX_HEADS * IDX_DIM), lambda bb, i, hp: (bb, i, 0)),
            pl.BlockSpec((1, s, LANES), lambda bb, i, hp: (bb, 0, 0)),
            pl.BlockSpec((1, t, LANES), lambda bb, i, hp: (bb, i, 0)),
            pl.BlockSpec((1, t, LANES), lambda bb, i, hp: (bb, i, hp)),
            pl.BlockSpec((1, LANES, s), lambda bb, i, hp: (bb, hp, 0)),
            pl.BlockSpec((1, s, LANES), lambda bb, i, hp: (bb, 0, hp)),
        ],
        out_specs=pl.BlockSpec((1, t, LANES), lambda bb, i, hp: (bb, i, hp)),
        out_shape=jax.ShapeDtypeStruct((b, s, WIDTH), jnp.float32),
        scratch_shapes=[pltpu.VMEM((t, s), jnp.float32)] + _attend_scratch(t, s),
        compiler_params=pltpu.CompilerParams(
            dimension_semantics=("arbitrary", "arbitrary", "arbitrary"),
            vmem_limit_bytes=VMEM_LIMIT),
        name="dsa_attention",
    )(iq, ik2, misc, aq, ak_t, av)


def _fox_kernel(q_ref, kt_ref, v_ref, cq_ref, ck_ref, o_ref, s_ref, m_ref, l_ref, acc_ref):
    t = ATT_TILE
    cq = jnp.concatenate([cq_ref[0, 0, :, 0:1], cq_ref[0, 0, :, 1:2]], axis=0)

    def bias(s, keys, width, diagonal):
        ck = ck_ref[0, 0, :, keys]
        ck = jnp.concatenate([jnp.broadcast_to(ck[0:1], (t, width)),
                              jnp.broadcast_to(ck[1:2], (t, width))], axis=0)
        s = s + (cq - ck)
        if diagonal:
            below = (lax.broadcasted_iota(jnp.int32, (t, t), 1)
                     <= lax.broadcasted_iota(jnp.int32, (t, t), 0))
            s = jnp.where(jnp.concatenate([below, below], axis=0), s, NEG)
        return s

    o_ref[0] = _attend(_stack_heads(q_ref[0]), kt_ref, v_ref, bias, pl.program_id(2),
                       s_ref, m_ref, l_ref, acc_ref)


def _fox_attention(fq, fk_t, fv, cum_q, cum_k):
    b, s, _ = fq.shape
    t = ATT_TILE
    return pl.pallas_call(
        _fox_kernel,
        grid=(b, N_HEAD_BLOCKS, s // t),
        in_specs=[
            pl.BlockSpec((1, t, LANES), lambda bb, hp, i: (bb, i, hp)),
            pl.BlockSpec((1, LANES, s), lambda bb, hp, i: (bb, hp, 0)),
            pl.BlockSpec((1, s, LANES), lambda bb, hp, i: (bb, 0, hp)),
            pl.BlockSpec((1, 1, t, HEADS_PER_BLOCK), lambda bb, hp, i: (bb, hp, i, 0)),
            pl.BlockSpec((1, 1, HEADS_PER_BLOCK, s), lambda bb, hp, i: (bb, hp, 0, 0)),
        ],
        out_specs=pl.BlockSpec((1, t, LANES), lambda bb, hp, i: (bb, i, hp)),
        out_shape=jax.ShapeDtypeStruct((b, s, WIDTH), jnp.float32),
        scratch_shapes=_attend_scratch(t, s),
        compiler_params=pltpu.CompilerParams(
            dimension_semantics=("arbitrary", "arbitrary", "arbitrary"),
            vmem_limit_bytes=VMEM_LIMIT),
        name="fox_attention",
    )(fq, fk_t, fv, cum_q, cum_k)


def _out_kernel(x_ref, ya_ref, yb_ref, gain_ref, wg_ref, bm_ref, wa_ref, wb_ref,
                wo_ref, ngain_ref, o_ref):
    x = x_ref[...]
    h = _rms(x, gain_ref[...]).astype(jnp.bfloat16)
    g = jnp.dot(h, wg_ref[...], preferred_element_type=jnp.float32)
    a_gate = g[:, :WIDTH]
    f_gate = g[:, WIDTH:2 * WIDTH]
    m_logit = g[:, 2 * WIDTH:] + bm_ref[...]
    za = (ya_ref[...] * (a_gate * jax.nn.sigmoid(a_gate))).astype(jnp.bfloat16)
    zb = (yb_ref[...] * (f_gate * jax.nn.sigmoid(f_gate))).astype(jnp.bfloat16)
    ua = jnp.dot(za, wa_ref[...], preferred_element_type=jnp.float32)
    ub = jnp.dot(zb, wb_ref[...], preferred_element_type=jnp.float32)
    gates = jax.nn.sigmoid(m_logit)
    merged = gates[:, :D_MODEL] * ua + gates[:, D_MODEL:] * ub
    y = x + jnp.dot(merged.astype(jnp.bfloat16), wo_ref[...],
                    preferred_element_type=jnp.float32)
    o_ref[...] = _rms(y, ngain_ref[...])


def _output(x2, ya, yb, gain, w_gates, b_merge, w_a, w_b, w_o, next_gain):
    m = x2.shape[0]
    tm = PROJ_ROWS
    row = lambda i: (i, 0)
    fixed = lambda i: (0, 0)
    return pl.pallas_call(
        _out_kernel,
        grid=(m // tm,),
        in_specs=[
            pl.BlockSpec((tm, D_MODEL), row),
            pl.BlockSpec((tm, WIDTH), row),
            pl.BlockSpec((tm, WIDTH), row),
            pl.BlockSpec((1, D_MODEL), fixed),
            pl.BlockSpec(w_gates.shape, fixed),
            pl.BlockSpec((1, 2 * D_MODEL), fixed),
            pl.BlockSpec(w_a.shape, fixed),
            pl.BlockSpec(w_b.shape, fixed),
            pl.BlockSpec(w_o.shape, fixed),
            pl.BlockSpec((1, D_MODEL), fixed),
        ],
        out_specs=pl.BlockSpec((tm, D_MODEL), row),
        out_shape=jax.ShapeDtypeStruct((m, D_MODEL), jnp.float32),
        compiler_params=pltpu.CompilerParams(
            dimension_semantics=("arbitrary",), vmem_limit_bytes=VMEM_LIMIT),
        name="out_proj",
    )(x2, ya, yb, gain, w_gates, b_merge, w_a, w_b, w_o, next_gain)


def _split_w_in(w):
    sizes = (WIDTH, WIDTH, WIDTH, WIDTH, N_IDX_HEADS * IDX_DIM, IDX_DIM, N_IDX_HEADS,
             WIDTH, WIDTH, WIDTH, WIDTH, N_HEADS, 2 * D_MODEL)
    points = np.cumsum(sizes)[:-1]
    return jnp.split(w, points, axis=-1)


def kernel(x, positions, norm_gain, w_in, b_forget, b_merge, w_branch_dsa,
           w_branch_fox, w_out, final_gain):
    b, s, d = x.shape
    depth = w_in.shape[0]
    bf16 = jnp.bfloat16

    half = jnp.arange(ROPE_HALF, dtype=jnp.float32)
    inv_freq = ROPE_THETA ** (-half * 2.0 / ROPE_DIM)
    dim = np.arange(LANES) % HEAD_DIM
    invf = jnp.where(dim < ROPE_DIM, inv_freq[dim % ROPE_HALF], 0.0)[None, :]
    pos2 = positions.reshape(b * s, 1)

    x2 = x.reshape(b * s, d)
    normed = None
    for l in range(depth):
        (w_aq, w_ak, w_av, w_ag, w_iq, w_ik, w_iw,
         w_fq, w_fk, w_fv, w_fg, w_fl, w_m) = _split_w_in(w_in[l])
        pad = jnp.zeros((d, LANES - N_IDX_HEADS - N_HEADS), w_in.dtype)
        w_all = jnp.concatenate(
            [w_aq, w_ak, w_av, w_fq, w_fk, w_fv, w_iq, w_ik, w_ik, w_iw, w_fl, pad],
            axis=1).astype(bf16)
        w_gates = jnp.concatenate([w_ag, w_fg, w_m], axis=1).astype(bf16)
        bf_row = jnp.zeros((1, LANES), jnp.float32).at[
            0, MISC_CUM:MISC_CUM + N_HEADS].set(b_forget[l])
        gain = norm_gain[l][None, :]
        next_gain = (norm_gain[l + 1] if l + 1 < depth else final_gain)[None, :]

        aq, ak, av, fq, fk, fv, iq, ik2, misc = _project(
            x2, pos2, gain, invf, w_all, bf_row, s)
        shape3 = lambda a: a.reshape(b, s, a.shape[-1])
        ya = _dsa_attention(shape3(iq), shape3(ik2), shape3(misc),
                            shape3(aq), shape3(ak).transpose(0, 2, 1), shape3(av))
        cum = misc[:, MISC_CUM:MISC_CUM + N_HEADS].reshape(
            b, s, N_HEAD_BLOCKS, HEADS_PER_BLOCK)
        cum_q = cum.transpose(0, 2, 1, 3)
        cum_k = cum.transpose(0, 2, 3, 1)
        yb = _fox_attention(shape3(fq), shape3(fk).transpose(0, 2, 1), shape3(fv),
                            cum_q, cum_k)
        normed = _output(x2, ya.reshape(b * s, WIDTH), yb.reshape(b * s, WIDTH),
                         gain, w_gates, b_merge[l][None, :],
                         w_branch_dsa[l].astype(bf16), w_branch_fox[l].astype(bf16),
                         w_out[l].astype(bf16), next_gain)
        assert depth == 1, "stacked layers need the un-normalised residual as well"
    return normed.reshape(b, s, d)
```

```python
import functools

import jax
import jax.numpy as jnp
import numpy as np
from jax import lax
from jax.experimental import pallas as pl
from jax.experimental.pallas import tpu as pltpu

D_MODEL = 1024
HEAD_DIM = 64
N_HEADS = 8
WIDTH = N_HEADS * HEAD_DIM
N_IDX_HEADS = 4
IDX_DIM = 64
TOPK_MAX = 256
ROPE_THETA = 500000.0
ROPE_DIM = HEAD_DIM // 4
ROPE_HALF = ROPE_DIM // 2
RMS_EPS = 1e-6
NEG = -1e30

LANES = 128
HEADS_PER_BLOCK = LANES // HEAD_DIM
N_HEAD_BLOCKS = N_HEADS // HEADS_PER_BLOCK
VMEM_LIMIT = 48 * 1024 * 1024

PROJ_ROWS = 512
ATT_TILE = 256
WIDE_KEYS = 1024
COUNT_ROWS = 64

TINY = float(np.finfo(np.float32).tiny)
ALL_THR = -3.0e38
BIG = 1.0e9
MAX_SEARCH = 400
INTERPOLATE_STEPS = 12
MIN_FRACTION = 1.0 / 32.0

MISC_IW = 0
MISC_CUM = N_IDX_HEADS


def _rms(x, gain):
    ms = jnp.mean(x * x, axis=-1, keepdims=True)
    return x * lax.rsqrt(ms + RMS_EPS) * gain


def _proj_kernel(x_ref, pos_ref, gain_ref, invf_ref, w_ref, bf_ref,
                 aq_ref, ak_ref, av_ref, fq_ref, fk_ref, fv_ref, iq_ref, ik_ref,
                 misc_ref, carry_ref, *, tiles_per_batch):
    tm = x_ref.shape[0]
    h = _rms(x_ref[...], gain_ref[...]).astype(jnp.bfloat16)

    ang = pos_ref[...].astype(jnp.float32) * invf_ref[...]
    cos = jnp.cos(ang)
    sin = jnp.sin(ang)
    d = lax.broadcasted_iota(jnp.int32, (tm, LANES), 1) % HEAD_DIM
    sin_lo = jnp.where(d < ROPE_HALF, -sin, 0.0)
    sin_hi = jnp.where(d >= ROPE_HALF, sin, 0.0)

    def rope(t):
        up = pltpu.roll(t, LANES - ROPE_HALF, 1)
        dn = pltpu.roll(t, ROPE_HALF, 1)
        return t * cos + up * sin_lo + dn * sin_hi

    def emit(out_ref, col0, width, use_rope, scale):
        r = jnp.dot(h, w_ref[:, col0:col0 + width],
                    preferred_element_type=jnp.float32)
        for c in range(width // LANES):
            t = r[:, c * LANES:(c + 1) * LANES]
            if use_rope:
                t = rope(t)
            if scale != 1.0:
                t = t * scale
            out_ref[:, c * LANES:(c + 1) * LANES] = t.astype(out_ref.dtype)

    qk_scale = HEAD_DIM ** -0.5
    emit(aq_ref, 0 * WIDTH, WIDTH, True, qk_scale)
    emit(ak_ref, 1 * WIDTH, WIDTH, True, 1.0)
    emit(av_ref, 2 * WIDTH, WIDTH, False, 1.0)
    emit(fq_ref, 3 * WIDTH, WIDTH, False, qk_scale)
    emit(fk_ref, 4 * WIDTH, WIDTH, False, 1.0)
    emit(fv_ref, 5 * WIDTH, WIDTH, False, 1.0)
    col = 6 * WIDTH
    emit(iq_ref, col, N_IDX_HEADS * IDX_DIM, True, IDX_DIM ** -0.5)
    col += N_IDX_HEADS * IDX_DIM
    emit(ik_ref, col, LANES, True, 1.0)
    col += LANES

    r = jnp.dot(h, w_ref[:, col:col + LANES], preferred_element_type=jnp.float32)
    z = r + bf_ref[...]
    logf = jnp.minimum(z, 0.0) - jnp.log1p(jnp.exp(-jnp.abs(z)))

    @pl.when(pl.program_id(0) % tiles_per_batch == 0)
    def _():
        carry_ref[...] = jnp.zeros_like(carry_ref)

    rows = lax.broadcasted_iota(jnp.int32, (tm, tm), 0)
    cols = lax.broadcasted_iota(jnp.int32, (tm, tm), 1)
    tri = jnp.where(cols <= rows, 1.0, 0.0).astype(jnp.bfloat16)
    hi = logf.astype(jnp.bfloat16)
    rem = logf - hi.astype(jnp.float32)
    mid = rem.astype(jnp.bfloat16)
    lo = (rem - mid.astype(jnp.float32)).astype(jnp.bfloat16)
    cum = (jnp.dot(tri, hi, preferred_element_type=jnp.float32)
           + jnp.dot(tri, mid, preferred_element_type=jnp.float32)
           + jnp.dot(tri, lo, preferred_element_type=jnp.float32)
           + carry_ref[...])
    carry_ref[...] = cum[tm - 1:tm, :]

    lane = lax.broadcasted_iota(jnp.int32, (tm, LANES), 1)
    misc = jnp.where(lane < MISC_CUM, r * (N_IDX_HEADS ** -0.5),
                     jnp.where(lane < MISC_CUM + N_HEADS, cum, 0.0))
    misc_ref[...] = misc


def _project(x2, pos2, gain, invf, w_all, bf_row, seq):
    m = x2.shape[0]
    tm = PROJ_ROWS
    n_w = w_all.shape[1]
    row = lambda i: (i, 0)
    fixed = lambda i: (0, 0)
    bf16 = jnp.bfloat16
    out_shape = (
        [jax.ShapeDtypeStruct((m, WIDTH), bf16)] * 6
        + [jax.ShapeDtypeStruct((m, N_IDX_HEADS * IDX_DIM), bf16),
           jax.ShapeDtypeStruct((m, LANES), bf16),
           jax.ShapeDtypeStruct((m, LANES), jnp.float32)])
    out_specs = (
        [pl.BlockSpec((tm, WIDTH), row)] * 6
        + [pl.BlockSpec((tm, N_IDX_HEADS * IDX_DIM), row),
           pl.BlockSpec((tm, LANES), row),
           pl.BlockSpec((tm, LANES), row)])
    return pl.pallas_call(
        functools.partial(_proj_kernel, tiles_per_batch=seq // tm),
        grid=(m // tm,),
        in_specs=[
            pl.BlockSpec((tm, D_MODEL), row),
            pl.BlockSpec((tm, 1), row),
            pl.BlockSpec((1, D_MODEL), fixed),
            pl.BlockSpec((1, LANES), fixed),
            pl.BlockSpec((D_MODEL, n_w), fixed),
            pl.BlockSpec((1, LANES), fixed),
        ],
        out_specs=out_specs,
        out_shape=out_shape,
        scratch_shapes=[pltpu.VMEM((1, LANES), jnp.float32)],
        compiler_params=pltpu.CompilerParams(
            dimension_semantics=("arbitrary",), vmem_limit_bytes=VMEM_LIMIT),
        name="in_proj",
    )(x2, pos2, gain, invf, w_all, bf_row)


def _half_select(x, first):
    lane = lax.broadcasted_iota(jnp.int32, x.shape, x.ndim - 1)
    keep = (lane < HEAD_DIM) if first else (lane >= HEAD_DIM)
    return jnp.where(keep, x, jnp.zeros_like(x))


def _stack_heads(q):
    return jnp.concatenate([_half_select(q, True), _half_select(q, False)], axis=0)


def _attend(qs, kt_ref, v_ref, bias, i, s_ref, m_ref, l_ref, acc_ref):
    t = ATT_TILE
    n_wide = (i * t) // WIDE_KEYS
    first_narrow = n_wide * (WIDE_KEYS // t)

    def phase_a(start, width, diagonal):
        keys = pl.ds(pl.multiple_of(start, width), width)
        s = jnp.dot(qs, kt_ref[0, :, keys], preferred_element_type=jnp.float32)
        s = bias(s, keys, width, diagonal)
        s_ref[:, keys] = s
        m = m_ref[...]
        for c in range(width // LANES):
            m = jnp.maximum(m, s[:, c * LANES:(c + 1) * LANES])
        m_ref[...] = m

    def phase_b(start, width, m):
        keys = pl.ds(pl.multiple_of(start, width), width)
        p = jnp.exp(s_ref[:, keys] - m)
        l = l_ref[...]
        for c in range(width // LANES):
            l = l + p[:, c * LANES:(c + 1) * LANES]
        l_ref[...] = l
        acc_ref[...] += jnp.dot(p.astype(jnp.bfloat16), v_ref[0, keys, :],
                                preferred_element_type=jnp.float32)

    def loop(lo, hi, fn):
        def body(j, carry):
            fn(j)
            return carry
        lax.fori_loop(lo, hi, body, 0)

    m_ref[...] = jnp.full(m_ref.shape, NEG, jnp.float32)
    loop(0, n_wide, lambda j: phase_a(j * WIDE_KEYS, WIDE_KEYS, False))
    loop(first_narrow, i, lambda j: phase_a(j * t, t, False))
    phase_a(i * t, t, True)

    m = jnp.max(m_ref[...], axis=1, keepdims=True)
    l_ref[...] = jnp.zeros(l_ref.shape, jnp.float32)
    acc_ref[...] = jnp.zeros(acc_ref.shape, jnp.float32)
    loop(0, n_wide, lambda j: phase_b(j * WIDE_KEYS, WIDE_KEYS, m))
    loop(first_narrow, i + 1, lambda j: phase_b(j * t, t, m))

    o = acc_ref[...] / jnp.sum(l_ref[...], axis=1, keepdims=True)
    lane = lax.broadcasted_iota(jnp.int32, (t, LANES), 1)
    return jnp.where(lane < HEAD_DIM, o[:t], o[t:])


def _attend_scratch(t, s):
    rows = HEADS_PER_BLOCK * t
    return [pltpu.VMEM((rows, s), jnp.float32),
            pltpu.VMEM((rows, LANES), jnp.float32),
            pltpu.VMEM((rows, LANES), jnp.float32),
            pltpu.VMEM((rows, LANES), jnp.float32)]


def _column_to_lanes(col):
    return jnp.broadcast_to(col, (col.shape[0], LANES))


def _index_mask(iq_ref, ikt_ref, misc_ref, sc_ref, hi_ref, lo_ref, cnt_ref, i):
    t = ATT_TILE
    top_k = float(TOPK_MAX)
    n_wide = (i * t) // WIDE_KEYS
    first_narrow = n_wide * (WIDE_KEYS // t)

    def blocks(fn, carry, last_narrow):
        def wide(j, c):
            return fn(pl.ds(pl.multiple_of(j * WIDE_KEYS, WIDE_KEYS), WIDE_KEYS),
                      WIDE_KEYS, c)

        def narrow(j, c):
            return fn(pl.ds(pl.multiple_of(j * t, t), t), t, c)

        carry = lax.fori_loop(0, n_wide, wide, carry)
        return lax.fori_loop(first_narrow, last_narrow, narrow, carry)

    iq = iq_ref[0]
    qh = [_half_select(iq[:, (h // 2) * LANES:(h // 2 + 1) * LANES], h % 2 == 0)
          for h in range(N_IDX_HEADS)]
    wh = [_column_to_lanes(misc_ref[0, :, MISC_IW + h:MISC_IW + h + 1])
          for h in range(N_IDX_HEADS)]

    def scores(keys, width):
        kt = ikt_ref[0, :, keys]
        rel = [jnp.dot(qh[h], kt, preferred_element_type=jnp.float32)
               for h in range(N_IDX_HEADS)]
        chunks = []
        for c in range(width // LANES):
            lanes = slice(c * LANES, (c + 1) * LANES)
            sc = wh[0] * jnp.maximum(rel[0][:, lanes], 0.0)
            for h in range(1, N_IDX_HEADS):
                sc = sc + wh[h] * jnp.maximum(rel[h][:, lanes], 0.0)
            chunks.append(sc)
        return chunks

    def track(chunks_hi, chunks_lo):
        hi = hi_ref[...]
        lo = lo_ref[...]
        for c_hi, c_lo in zip(chunks_hi, chunks_lo):
            hi = jnp.maximum(hi, c_hi)
            lo = jnp.minimum(lo, c_lo)
        hi_ref[...] = hi
        lo_ref[...] = lo

    def score_block(keys, width, carry):
        chunks = scores(keys, width)
        sc_ref[:, keys] = jnp.concatenate(chunks, axis=1)
        track(chunks, chunks)
        return carry

    hi_ref[...] = jnp.full(hi_ref.shape, -jnp.inf, jnp.float32)
    lo_ref[...] = jnp.full(lo_ref.shape, jnp.inf, jnp.float32)
    blocks(score_block, 0, i)
    diag = pl.ds(pl.multiple_of(i * t, t), t)
    chunks = scores(diag, t)
    row = lax.broadcasted_iota(jnp.int32, (t, LANES), 0)
    lane = lax.broadcasted_iota(jnp.int32, (t, LANES), 1)
    causal = [lane + c * LANES <= row for c in range(t // LANES)]
    masked = [jnp.where(m, c, -jnp.inf) for m, c in zip(causal, chunks)]
    sc_ref[:, diag] = jnp.concatenate(masked, axis=1)
    track(masked, [jnp.where(m, c, jnp.inf) for m, c in zip(causal, chunks)])

    def per_row(row):
        return jnp.transpose(jnp.broadcast_to(row, (LANES, t)))

    def count_ge(thr):
        thr = per_row(thr)
        for r in range(t // COUNT_ROWS):
            rows = slice(r * COUNT_ROWS, (r + 1) * COUNT_ROWS)
            thr_r = thr[rows]

            def block(keys, width, acc, rows=rows, thr_r=thr_r):
                x = sc_ref[rows, keys]
                for c in range(width // LANES):
                    acc = acc + jnp.where(x[:, c * LANES:(c + 1) * LANES] >= thr_r,
                                          1.0, 0.0)
                return acc

            cnt_ref[rows, :] = blocks(
                block, jnp.zeros((COUNT_ROWS, LANES), jnp.float32), i + 1)
        ones = jnp.ones((8, LANES), jnp.bfloat16)
        return lax.dot_general(ones, cnt_ref[...].astype(jnp.bfloat16),
                               (((1,), (1,)), ((), ())),
                               preferred_element_type=jnp.float32)[0:1]

    def flag(cond):
        return jnp.where(cond, 1.0, 0.0)

    rmax = jnp.max(jnp.transpose(hi_ref[...]), axis=0, keepdims=True)
    rmin = jnp.min(jnp.transpose(lo_ref[...]), axis=0, keepdims=True)
    n_causal = (i * t + 1 + lax.broadcasted_iota(jnp.int32, (1, t), 1)
                ).astype(jnp.float32)
    c_pos = count_ge(jnp.full((1, t), TINY, jnp.float32))
    c_zero = count_ge(jnp.zeros((1, t), jnp.float32))
    c_max = count_ge(rmax)
    take_all = n_causal <= top_k
    tie_max = c_max >= top_k
    tie_zero = jnp.where(c_pos < top_k, flag(c_zero >= top_k), 0.0) > 0.5
    done = jnp.where(take_all, 1.0, jnp.where(tie_max, 1.0, jnp.where(
        tie_zero, 1.0, flag(c_pos == top_k))))
    thr = jnp.where(take_all, ALL_THR, jnp.where(tie_max, rmax, jnp.where(
        tie_zero, 0.0, TINY)))
    take = jnp.where(take_all, BIG, jnp.where(tie_max, top_k, jnp.where(
        tie_zero, top_k - c_pos, BIG)))
    positive = c_pos > top_k
    lo = jnp.where(positive, TINY, rmin)
    c_lo = jnp.where(positive, c_pos, n_causal)
    hi = jnp.where(positive, rmax, 0.0)
    c_hi = jnp.where(positive, c_max, c_zero)
    log_target = float(np.log(top_k + 0.5))

    def cond(state):
        it, done = state[0], state[-1]
        return (it < MAX_SEARCH) & (jnp.min(done) < 0.5)

    def step(state):
        it, lo, hi, c_lo, c_hi, thr, take, done = state
        log_lo = jnp.log(c_lo)
        frac = (log_lo - log_target) / (log_lo - jnp.log(jnp.maximum(c_hi, 0.5)))
        frac = jnp.clip(frac, MIN_FRACTION, 1.0 - MIN_FRACTION)
        frac = jnp.where(it < INTERPOLATE_STEPS, frac, 0.5)
        guess = lo + (hi - lo) * frac
        halfway = 0.5 * lo + 0.5 * hi
        mid = jnp.where(guess <= lo, halfway, jnp.where(guess >= hi, halfway, guess))
        spent = jnp.where(mid <= lo, 1.0, flag(mid >= hi))
        c = count_ge(mid)
        finish = (1.0 - done) * jnp.maximum(spent, flag(c == top_k)) > 0.5
        thr = jnp.where(finish, jnp.where(spent > 0.5, lo, mid), thr)
        take = jnp.where(finish, jnp.where(spent > 0.5, top_k - c_hi, BIG), take)
        above = c > top_k
        below = c < top_k
        lo = jnp.where(above, mid, lo)
        c_lo = jnp.where(above, c, c_lo)
        hi = jnp.where(below, mid, hi)
        c_hi = jnp.where(below, c, c_hi)
        done = jnp.where(finish, 1.0, done)
        return it + 1, lo, hi, c_lo, c_hi, thr, take, done

    _, lo, _, _, c_hi, thr, take, done = lax.while_loop(
        cond, step, (jnp.int32(0), lo, hi, c_lo, c_hi, thr, take, done))
    thr = jnp.where(done < 0.5, lo, thr)
    take = jnp.where(done < 0.5, top_k - c_hi, take)

    ranked = jnp.max(flag(take < BIG)) > 0.5

    thr_l = per_row(thr)

    @pl.when(jnp.logical_not(ranked))
    def _():
        def block(keys, width, carry):
            x = sc_ref[:, keys]
            sc_ref[:, keys] = jnp.concatenate(
                [jnp.where(x[:, c * LANES:(c + 1) * LANES] >= thr_l, 0.0, NEG)
                 for c in range(width // LANES)], axis=1)
            return carry

        blocks(block, 0, i + 1)

    @pl.when(ranked)
    def _():
        before = (lax.broadcasted_iota(jnp.int32, (t, t), 0)
                  < lax.broadcasted_iota(jnp.int32, (t, t), 1))
        before = jnp.where(before, 1.0, 0.0).astype(jnp.bfloat16)
        thr_c = thr_l[:, 0:1]
        take_c = per_row(take)[:, 0:1]

        def mask_tile(j, seen):
            keys = pl.ds(pl.multiple_of(j * t, t), t)
            s = sc_ref[:, keys]
            eq = flag(s == thr_c)
            rank = seen + jnp.dot(eq.astype(jnp.bfloat16), before,
                                  preferred_element_type=jnp.float32)
            sc_ref[:, keys] = jnp.where(s > thr_c, 0.0, jnp.where(
                s == thr_c, jnp.where(rank < take_c, 0.0, NEG), NEG))
            return seen + jnp.sum(eq, axis=1, keepdims=True)

        lax.fori_loop(0, i + 1, mask_tile, jnp.zeros((t, 1), jnp.float32))


def _dsa_kernel(iq_ref, ikt_ref, misc_ref, q_ref, kt_ref, v_ref, o_ref, sc_ref,
                hi_ref, lo_ref, cnt_ref, s_ref, m_ref, l_ref, acc_ref):
    i = pl.program_id(1)

    @pl.when(pl.program_id(2) == 0)
    def _():
        _index_mask(iq_ref, ikt_ref, misc_ref, sc_ref, hi_ref, lo_ref, cnt_ref, i)

    def bias(s, keys, width, diagonal):
        mask = sc_ref[:, keys]
        return s + jnp.concatenate([mask, mask], axis=0)

    o_ref[0] = _attend(_stack_heads(q_ref[0]), kt_ref, v_ref, bias, i,
                       s_ref, m_ref, l_ref, acc_ref)


def _dsa_attention(iq, ik2_t, misc, aq, ak_t, av):
    b, s, _ = aq.shape
    t = ATT_TILE
    return pl.pallas_call(
        _dsa_kernel,
        grid=(b, s // t, N_HEAD_BLOCKS),
        in_specs=[
            pl.BlockSpec((1, t, N_IDX_HEADS * IDX_DIM), lambda bb, i, hp: (bb, i, 0)),
            pl.BlockSpec((1, LANES, s), lambda bb, i, hp: (bb, 0, 0)),
            pl.BlockSpec((1, t, LANES), lambda bb, i, hp: (bb, i, 0)),
            pl.BlockSpec((1, t, LANES), lambda bb, i, hp: (bb, i, hp)),
            pl.BlockSpec((1, LANES, s), lambda bb, i, hp: (bb, hp, 0)),
            pl.BlockSpec((1, s, LANES), lambda bb, i, hp: (bb, 0, hp)),
        ],
        out_specs=pl.BlockSpec((1, t, LANES), lambda bb, i, hp: (bb, i, hp)),
        out_shape=jax.ShapeDtypeStruct((b, s, WIDTH), jnp.float32),
        scratch_shapes=[pltpu.VMEM((t, s), jnp.float32),
                        pltpu.VMEM((t, LANES), jnp.float32),
                        pltpu.VMEM((t, LANES), jnp.float32),
                        pltpu.VMEM((t, LANES), jnp.float32)] + _attend_scratch(t, s),
        compiler_params=pltpu.CompilerParams(
            dimension_semantics=("arbitrary", "arbitrary", "arbitrary"),
            vmem_limit_bytes=VMEM_LIMIT),
        name="dsa_attention",
    )(iq, ik2_t, misc, aq, ak_t, av)


def _fox_kernel(q_ref, kt_ref, v_ref, cq_ref, ck_ref, o_ref, s_ref, m_ref, l_ref, acc_ref):
    t = ATT_TILE
    cq = jnp.concatenate([cq_ref[0, 0, :, 0:1], cq_ref[0, 0, :, 1:2]], axis=0)

    def bias(s, keys, width, diagonal):
        ck = ck_ref[0, 0, :, keys]
        ck = jnp.concatenate([jnp.broadcast_to(ck[0:1], (t, width)),
                              jnp.broadcast_to(ck[1:2], (t, width))], axis=0)
        s = s + (cq - ck)
        if diagonal:
            below = (lax.broadcasted_iota(jnp.int32, (t, t), 1)
                     <= lax.broadcasted_iota(jnp.int32, (t, t), 0))
            s = jnp.where(jnp.concatenate([below, below], axis=0), s, NEG)
        return s

    o_ref[0] = _attend(_stack_heads(q_ref[0]), kt_ref, v_ref, bias, pl.program_id(2),
                       s_ref, m_ref, l_ref, acc_ref)


def _fox_attention(fq, fk_t, fv, cum_q, cum_k):
    b, s, _ = fq.shape
    t = ATT_TILE
    return pl.pallas_call(
        _fox_kernel,
        grid=(b, N_HEAD_BLOCKS, s // t),
        in_specs=[
            pl.BlockSpec((1, t, LANES), lambda bb, hp, i: (bb, i, hp)),
            pl.BlockSpec((1, LANES, s), lambda bb, hp, i: (bb, hp, 0)),
            pl.BlockSpec((1, s, LANES), lambda bb, hp, i: (bb, 0, hp)),
            pl.BlockSpec((1, 1, t, HEADS_PER_BLOCK), lambda bb, hp, i: (bb, hp, i, 0)),
            pl.BlockSpec((1, 1, HEADS_PER_BLOCK, s), lambda bb, hp, i: (bb, hp, 0, 0)),
        ],
        out_specs=pl.BlockSpec((1, t, LANES), lambda bb, hp, i: (bb, i, hp)),
        out_shape=jax.ShapeDtypeStruct((b, s, WIDTH), jnp.float32),
        scratch_shapes=_attend_scratch(t, s),
        compiler_params=pltpu.CompilerParams(
            dimension_semantics=("arbitrary", "arbitrary", "arbitrary"),
            vmem_limit_bytes=VMEM_LIMIT),
        name="fox_attention",
    )(fq, fk_t, fv, cum_q, cum_k)


def _out_kernel(x_ref, ya_ref, yb_ref, gain_ref, wg_ref, bm_ref, wa_ref, wb_ref,
                wo_ref, ngain_ref, o_ref):
    x = x_ref[...]
    h = _rms(x, gain_ref[...]).astype(jnp.bfloat16)
    g = jnp.dot(h, wg_ref[...], preferred_element_type=jnp.float32)
    a_gate = g[:, :WIDTH]
    f_gate = g[:, WIDTH:2 * WIDTH]
    m_logit = g[:, 2 * WIDTH:] + bm_ref[...]
    za = (ya_ref[...] * (a_gate * jax.nn.sigmoid(a_gate))).astype(jnp.bfloat16)
    zb = (yb_ref[...] * (f_gate * jax.nn.sigmoid(f_gate))).astype(jnp.bfloat16)
    ua = jnp.dot(za, wa_ref[...], preferred_element_type=jnp.float32)
    ub = jnp.dot(zb, wb_ref[...], preferred_element_type=jnp.float32)
    gates = jax.nn.sigmoid(m_logit)
    merged = gates[:, :D_MODEL] * ua + gates[:, D_MODEL:] * ub
    y = x + jnp.dot(merged.astype(jnp.bfloat16), wo_ref[...],
                    preferred_element_type=jnp.float32)
    o_ref[...] = _rms(y, ngain_ref[...])


def _output(x2, ya, yb, gain, w_gates, b_merge, w_a, w_b, w_o, next_gain):
    m = x2.shape[0]
    tm = PROJ_ROWS
    row = lambda i: (i, 0)
    fixed = lambda i: (0, 0)
    return pl.pallas_call(
        _out_kernel,
        grid=(m // tm,),
        in_specs=[
            pl.BlockSpec((tm, D_MODEL), row),
            pl.BlockSpec((tm, WIDTH), row),
            pl.BlockSpec((tm, WIDTH), row),
            pl.BlockSpec((1, D_MODEL), fixed),
            pl.BlockSpec(w_gates.shape, fixed),
            pl.BlockSpec((1, 2 * D_MODEL), fixed),
            pl.BlockSpec(w_a.shape, fixed),
            pl.BlockSpec(w_b.shape, fixed),
            pl.BlockSpec(w_o.shape, fixed),
            pl.BlockSpec((1, D_MODEL), fixed),
        ],
        out_specs=pl.BlockSpec((tm, D_MODEL), row),
        out_shape=jax.ShapeDtypeStruct((m, D_MODEL), jnp.float32),
        compiler_params=pltpu.CompilerParams(
            dimension_semantics=("arbitrary",), vmem_limit_bytes=VMEM_LIMIT),
        name="out_proj",
    )(x2, ya, yb, gain, w_gates, b_merge, w_a, w_b, w_o, next_gain)


def _split_w_in(w):
    sizes = (WIDTH, WIDTH, WIDTH, WIDTH, N_IDX_HEADS * IDX_DIM, IDX_DIM, N_IDX_HEADS,
             WIDTH, WIDTH, WIDTH, WIDTH, N_HEADS, 2 * D_MODEL)
    points = np.cumsum(sizes)[:-1]
    return jnp.split(w, points, axis=-1)


def kernel(x, positions, norm_gain, w_in, b_forget, b_merge, w_branch_dsa,
           w_branch_fox, w_out, final_gain):
    b, s, d = x.shape
    depth = w_in.shape[0]
    bf16 = jnp.bfloat16

    half = jnp.arange(ROPE_HALF, dtype=jnp.float32)
    inv_freq = ROPE_THETA ** (-half * 2.0 / ROPE_DIM)
    dim = np.arange(LANES) % HEAD_DIM
    invf = jnp.where(dim < ROPE_DIM, inv_freq[dim % ROPE_HALF], 0.0)[None, :]
    pos2 = positions.reshape(b * s, 1)

    x2 = x.reshape(b * s, d)
    normed = None
    for l in range(depth):
        (w_aq, w_ak, w_av, w_ag, w_iq, w_ik, w_iw,
         w_fq, w_fk, w_fv, w_fg, w_fl, w_m) = _split_w_in(w_in[l])
        pad = jnp.zeros((d, LANES - N_IDX_HEADS - N_HEADS), w_in.dtype)
        w_all = jnp.concatenate(
            [w_aq, w_ak, w_av, w_fq, w_fk, w_fv, w_iq, w_ik, w_ik, w_iw, w_fl, pad],
            axis=1).astype(bf16)
        w_gates = jnp.concatenate([w_ag, w_fg, w_m], axis=1).astype(bf16)
        bf_row = jnp.zeros((1, LANES), jnp.float32).at[
            0, MISC_CUM:MISC_CUM + N_HEADS].set(b_forget[l])
        gain = norm_gain[l][None, :]
        next_gain = (norm_gain[l + 1] if l + 1 < depth else final_gain)[None, :]

        aq, ak, av, fq, fk, fv, iq, ik2, misc = _project(
            x2, pos2, gain, invf, w_all, bf_row, s)
        shape3 = lambda a: a.reshape(b, s, a.shape[-1])
        ya = _dsa_attention(shape3(iq), shape3(ik2).transpose(0, 2, 1), shape3(misc),
                            shape3(aq), shape3(ak).transpose(0, 2, 1), shape3(av))
        cum = misc[:, MISC_CUM:MISC_CUM + N_HEADS].reshape(
            b, s, N_HEAD_BLOCKS, HEADS_PER_BLOCK)
        cum_q = cum.transpose(0, 2, 1, 3)
        cum_k = cum.transpose(0, 2, 3, 1)
        yb = _fox_attention(shape3(fq), shape3(fk).transpose(0, 2, 1), shape3(fv),
                            cum_q, cum_k)
        normed = _output(x2, ya.reshape(b * s, WIDTH), yb.reshape(b * s, WIDTH),
                         gain, w_gates, b_merge[l][None, :],
                         w_branch_dsa[l].astype(bf16), w_branch_fox[l].astype(bf16),
                         w_out[l].astype(bf16), next_gain)
        assert depth == 1, "stacked layers need the un-normalised residual as well"
    return normed.reshape(b, s, d)
```

```python
import functools

import jax
import jax.numpy as jnp
import numpy as np
from jax import lax
from jax.experimental import pallas as pl
from jax.experimental.pallas import tpu as pltpu

D_MODEL = 1024
HEAD_DIM = 64
N_HEADS = 8
WIDTH = N_HEADS * HEAD_DIM
N_IDX_HEADS = 4
IDX_DIM = 64
TOPK_MAX = 256
ROPE_THETA = 500000.0
ROPE_DIM = HEAD_DIM // 4
ROPE_HALF = ROPE_DIM // 2
RMS_EPS = 1e-6
NEG = -1e30
LOG2_E = float(np.log2(np.e))

LANES = 128
HEADS_PER_BLOCK = LANES // HEAD_DIM
N_HEAD_BLOCKS = N_HEADS // HEADS_PER_BLOCK
VMEM_LIMIT = 48 * 1024 * 1024

PROJ_ROWS = 512
ATT_TILE = 256
WIDE_KEYS = 1024
COUNT_ROWS = 64

TINY = float(np.finfo(np.float32).tiny)
ALL_THR = -3.0e38
BIG = 1.0e9
MAX_SEARCH = 400
INTERPOLATE_STEPS = 12
MIN_FRACTION = 1.0 / 32.0

MISC_IW = 0
MISC_CUM = N_IDX_HEADS


def _rms(x, gain):
    ms = jnp.mean(x * x, axis=-1, keepdims=True)
    return x * lax.rsqrt(ms + RMS_EPS) * gain


def _proj_kernel(x_ref, pos_ref, gain_ref, invf_ref, w_ref, bf_ref,
                 aq_ref, ak_ref, av_ref, fq_ref, fk_ref, fv_ref, iq_ref, ik_ref,
                 misc_ref, carry_ref, *, tiles_per_batch):
    tm = x_ref.shape[0]
    h = _rms(x_ref[...], gain_ref[...]).astype(jnp.bfloat16)

    ang = pos_ref[...].astype(jnp.float32) * invf_ref[...]
    cos = jnp.cos(ang)
    sin = jnp.sin(ang)
    d = lax.broadcasted_iota(jnp.int32, (tm, LANES), 1) % HEAD_DIM
    sin_lo = jnp.where(d < ROPE_HALF, -sin, 0.0)
    sin_hi = jnp.where(d >= ROPE_HALF, sin, 0.0)

    def rope(t):
        up = pltpu.roll(t, LANES - ROPE_HALF, 1)
        dn = pltpu.roll(t, ROPE_HALF, 1)
        return t * cos + up * sin_lo + dn * sin_hi

    def blocks_of(col0, width, use_rope, scale):
        r = jnp.dot(h, w_ref[:, col0:col0 + width],
                    preferred_element_type=jnp.float32)
        for c in range(width // LANES):
            t = r[:, c * LANES:(c + 1) * LANES]
            if use_rope:
                t = rope(t)
            if scale != 1.0:
                t = t * scale
            yield c, t

    def emit(out_ref, col0, width, use_rope, scale):
        for c, t in blocks_of(col0, width, use_rope, scale):
            out_ref[:, c * LANES:(c + 1) * LANES] = t.astype(out_ref.dtype)

    def emit_transposed(out_ref, col0, width, use_rope):
        for c, t in blocks_of(col0, width, use_rope, 1.0):
            out_ref[0, c * LANES:(c + 1) * LANES, :] = jnp.transpose(t).astype(
                out_ref.dtype)

    def emit_values(out_ref, col0):
        lane = lax.broadcasted_iota(jnp.int32, (tm, LANES), 1)
        tail = jnp.where(lane == HEAD_DIM, 1.0, 0.0)
        for c, t in blocks_of(col0, WIDTH, False, 1.0):
            pair = (t, pltpu.roll(t, HEAD_DIM, 1))
            for k in range(HEADS_PER_BLOCK):
                head = c * HEADS_PER_BLOCK + k
                out_ref[:, head * LANES:(head + 1) * LANES] = jnp.where(
                    lane < HEAD_DIM, pair[k], tail).astype(out_ref.dtype)

    qk_scale = HEAD_DIM ** -0.5 * LOG2_E
    emit(aq_ref, 0 * WIDTH, WIDTH, True, qk_scale)
    emit_transposed(ak_ref, 1 * WIDTH, WIDTH, True)
    emit_values(av_ref, 2 * WIDTH)
    emit(fq_ref, 3 * WIDTH, WIDTH, False, qk_scale)
    emit_transposed(fk_ref, 4 * WIDTH, WIDTH, False)
    emit_values(fv_ref, 5 * WIDTH)
    col = 6 * WIDTH
    emit(iq_ref, col, N_IDX_HEADS * IDX_DIM, True, IDX_DIM ** -0.5)
    col += N_IDX_HEADS * IDX_DIM
    emit_transposed(ik_ref, col, LANES, True)
    col += LANES

    r = jnp.dot(h, w_ref[:, col:col + LANES], preferred_element_type=jnp.float32)
    z = r + bf_ref[...]
    logf = jnp.minimum(z, 0.0) - jnp.log1p(jnp.exp(-jnp.abs(z)))

    @pl.when(pl.program_id(0) % tiles_per_batch == 0)
    def _():
        carry_ref[...] = jnp.zeros_like(carry_ref)

    rows = lax.broadcasted_iota(jnp.int32, (tm, tm), 0)
    cols = lax.broadcasted_iota(jnp.int32, (tm, tm), 1)
    tri = jnp.where(cols <= rows, 1.0, 0.0).astype(jnp.bfloat16)
    hi = logf.astype(jnp.bfloat16)
    rem = logf - hi.astype(jnp.float32)
    mid = rem.astype(jnp.bfloat16)
    lo = (rem - mid.astype(jnp.float32)).astype(jnp.bfloat16)
    cum = (jnp.dot(tri, hi, preferred_element_type=jnp.float32)
           + jnp.dot(tri, mid, preferred_element_type=jnp.float32)
           + jnp.dot(tri, lo, preferred_element_type=jnp.float32)
           + carry_ref[...])
    carry_ref[...] = cum[tm - 1:tm, :]

    lane = lax.broadcasted_iota(jnp.int32, (tm, LANES), 1)
    misc = jnp.where(lane < MISC_CUM, r * (N_IDX_HEADS ** -0.5),
                     jnp.where(lane < MISC_CUM + N_HEADS, cum * LOG2_E, 0.0))
    misc_ref[...] = misc


def _project(x2, pos2, gain, invf, w_all, bf_row, seq):
    m = x2.shape[0]
    tm = PROJ_ROWS
    n_w = w_all.shape[1]
    row = lambda i: (i, 0)
    fixed = lambda i: (0, 0)
    bf16 = jnp.bfloat16
    tiles_per_batch = seq // tm
    batch = m // seq
    transposed = lambda i: (i // tiles_per_batch, 0, i % tiles_per_batch)

    def rows(width, dtype=bf16):
        return jax.ShapeDtypeStruct((m, width), dtype), pl.BlockSpec((tm, width), row)

    def keys(width):
        return (jax.ShapeDtypeStruct((batch, width, seq), bf16),
                pl.BlockSpec((1, width, tm), transposed))

    outs = [rows(WIDTH), keys(WIDTH), rows(N_HEADS * LANES),
            rows(WIDTH), keys(WIDTH), rows(N_HEADS * LANES),
            rows(N_IDX_HEADS * IDX_DIM), keys(LANES),
            rows(LANES, jnp.float32)]
    out_shape = [o[0] for o in outs]
    out_specs = [o[1] for o in outs]
    return pl.pallas_call(
        functools.partial(_proj_kernel, tiles_per_batch=tiles_per_batch),
        grid=(m // tm,),
        in_specs=[
            pl.BlockSpec((tm, D_MODEL), row),
            pl.BlockSpec((tm, 1), row),
            pl.BlockSpec((1, D_MODEL), fixed),
            pl.BlockSpec((1, LANES), fixed),
            pl.BlockSpec((D_MODEL, n_w), fixed),
            pl.BlockSpec((1, LANES), fixed),
        ],
        out_specs=out_specs,
        out_shape=out_shape,
        scratch_shapes=[pltpu.VMEM((1, LANES), jnp.float32)],
        compiler_params=pltpu.CompilerParams(
            dimension_semantics=("arbitrary",), vmem_limit_bytes=VMEM_LIMIT),
        name="in_proj",
    )(x2, pos2, gain, invf, w_all, bf_row)


def _half_select(x, first):
    lane = lax.broadcasted_iota(jnp.int32, x.shape, x.ndim - 1)
    keep = (lane < HEAD_DIM) if first else (lane >= HEAD_DIM)
    return jnp.where(keep, x, jnp.zeros_like(x))


def _stack_heads(q):
    return jnp.concatenate([_half_select(q, True), _half_select(q, False)], axis=0)


def _attend(qs, kt_ref, v_ref, bias, i, s_ref, m_ref, acc_ref):
    t = ATT_TILE
    n_wide = (i * t) // WIDE_KEYS
    first_narrow = n_wide * (WIDE_KEYS // t)

    def phase_a(start, width, diagonal):
        keys = pl.ds(pl.multiple_of(start, width), width)
        s = jnp.dot(qs, kt_ref[0, :, keys], preferred_element_type=jnp.float32)
        s = bias(s, keys, width, diagonal)
        s_ref[:, keys] = s
        m = m_ref[...]
        for c in range(width // LANES):
            m = jnp.maximum(m, s[:, c * LANES:(c + 1) * LANES])
        m_ref[...] = m

    def phase_b(start, width, m):
        keys = pl.ds(pl.multiple_of(start, width), width)
        p = jnp.exp2(s_ref[:, keys] - m).astype(jnp.bfloat16)
        for k in range(HEADS_PER_BLOCK):
            acc_ref[k * t:(k + 1) * t, :] += jnp.dot(
                p[k * t:(k + 1) * t], v_ref[0, keys, k * LANES:(k + 1) * LANES],
                preferred_element_type=jnp.float32)

    def loop(lo, hi, fn):
        def body(j, carry):
            fn(j)
            return carry
        lax.fori_loop(lo, hi, body, 0)

    m_ref[...] = jnp.full(m_ref.shape, NEG, jnp.float32)
    loop(0, n_wide, lambda j: phase_a(j * WIDE_KEYS, WIDE_KEYS, False))
    loop(first_narrow, i, lambda j: phase_a(j * t, t, False))
    phase_a(i * t, t, True)

    m = jnp.max(m_ref[...], axis=1, keepdims=True)
    acc_ref[...] = jnp.zeros(acc_ref.shape, jnp.float32)
    loop(0, n_wide, lambda j: phase_b(j * WIDE_KEYS, WIDE_KEYS, m))
    loop(first_narrow, i + 1, lambda j: phase_b(j * t, t, m))

    acc = acc_ref[...]
    o = acc / acc[:, HEAD_DIM:HEAD_DIM + 1]
    lane = lax.broadcasted_iota(jnp.int32, (t, LANES), 1)
    return jnp.where(lane < HEAD_DIM, o[:t], pltpu.roll(o[t:], HEAD_DIM, 1))


def _attend_scratch(t, s):
    rows = HEADS_PER_BLOCK * t
    return [pltpu.VMEM((rows, s), jnp.float32),
            pltpu.VMEM((rows, LANES), jnp.float32),
            pltpu.VMEM((rows, LANES), jnp.float32)]


def _column_to_lanes(col):
    return jnp.broadcast_to(col, (col.shape[0], LANES))


def _index_mask(iq_ref, ikt_ref, misc_ref, sc_ref, hi_ref, lo_ref, cnt_ref, i):
    t = ATT_TILE
    top_k = float(TOPK_MAX)
    n_wide = (i * t) // WIDE_KEYS
    first_narrow = n_wide * (WIDE_KEYS // t)

    def blocks(fn, carry, last_narrow):
        def wide(j, c):
            return fn(pl.ds(pl.multiple_of(j * WIDE_KEYS, WIDE_KEYS), WIDE_KEYS),
                      WIDE_KEYS, c)

        def narrow(j, c):
            return fn(pl.ds(pl.multiple_of(j * t, t), t), t, c)

        carry = lax.fori_loop(0, n_wide, wide, carry)
        return lax.fori_loop(first_narrow, last_narrow, narrow, carry)

    iq = iq_ref[0]
    qh = [_half_select(iq[:, (h // 2) * LANES:(h // 2 + 1) * LANES], h % 2 == 0)
          for h in range(N_IDX_HEADS)]
    wh = [_column_to_lanes(misc_ref[0, :, MISC_IW + h:MISC_IW + h + 1])
          for h in range(N_IDX_HEADS)]

    def scores(keys, width):
        kt = ikt_ref[0, :, keys]
        rel = [jnp.dot(qh[h], kt, preferred_element_type=jnp.float32)
               for h in range(N_IDX_HEADS)]
        chunks = []
        for c in range(width // LANES):
            lanes = slice(c * LANES, (c + 1) * LANES)
            sc = wh[0] * jnp.maximum(rel[0][:, lanes], 0.0)
            for h in range(1, N_IDX_HEADS):
                sc = sc + wh[h] * jnp.maximum(rel[h][:, lanes], 0.0)
            chunks.append(sc)
        return chunks

    def track(chunks_hi, chunks_lo):
        hi = hi_ref[...]
        lo = lo_ref[...]
        for c_hi, c_lo in zip(chunks_hi, chunks_lo):
            hi = jnp.maximum(hi, c_hi)
            lo = jnp.minimum(lo, c_lo)
        hi_ref[...] = hi
        lo_ref[...] = lo

    def score_block(keys, width, carry):
        chunks = scores(keys, width)
        sc_ref[:, keys] = jnp.concatenate(chunks, axis=1)
        track(chunks, chunks)
        return carry

    hi_ref[...] = jnp.full(hi_ref.shape, -jnp.inf, jnp.float32)
    lo_ref[...] = jnp.full(lo_ref.shape, jnp.inf, jnp.float32)
    blocks(score_block, 0, i)
    diag = pl.ds(pl.multiple_of(i * t, t), t)
    chunks = scores(diag, t)
    row = lax.broadcasted_iota(jnp.int32, (t, LANES), 0)
    lane = lax.broadcasted_iota(jnp.int32, (t, LANES), 1)
    causal = [lane + c * LANES <= row for c in range(t // LANES)]
    masked = [jnp.where(m, c, -jnp.inf) for m, c in zip(causal, chunks)]
    sc_ref[:, diag] = jnp.concatenate(masked, axis=1)
    track(masked, [jnp.where(m, c, jnp.inf) for m, c in zip(causal, chunks)])

    def per_row(row):
        return jnp.transpose(jnp.broadcast_to(row, (LANES, t)))

    def count_ge(thr):
        thr = per_row(thr)
        for r in range(t // COUNT_ROWS):
            rows = slice(r * COUNT_ROWS, (r + 1) * COUNT_ROWS)
            thr_r = thr[rows]

            def block(keys, width, acc, rows=rows, thr_r=thr_r):
                x = sc_ref[rows, keys]
                for c in range(width // LANES):
                    acc = acc + jnp.where(x[:, c * LANES:(c + 1) * LANES] >= thr_r,
                                          1.0, 0.0)
                return acc

            cnt_ref[rows, :] = blocks(
                block, jnp.zeros((COUNT_ROWS, LANES), jnp.float32), i + 1)
        ones = jnp.ones((8, LANES), jnp.bfloat16)
        return lax.dot_general(ones, cnt_ref[...].astype(jnp.bfloat16),
                               (((1,), (1,)), ((), ())),
                               preferred_element_type=jnp.float32)[0:1]

    def flag(cond):
        return jnp.where(cond, 1.0, 0.0)

    rmax = jnp.max(jnp.transpose(hi_ref[...]), axis=0, keepdims=True)
    rmin = jnp.min(jnp.transpose(lo_ref[...]), axis=0, keepdims=True)
    n_causal = (i * t + 1 + lax.broadcasted_iota(jnp.int32, (1, t), 1)
                ).astype(jnp.float32)
    c_pos = count_ge(jnp.full((1, t), TINY, jnp.float32))
    c_zero = count_ge(jnp.zeros((1, t), jnp.float32))
    c_max = count_ge(rmax)
    take_all = n_causal <= top_k
    tie_max = c_max >= top_k
    tie_zero = jnp.where(c_pos < top_k, flag(c_zero >= top_k), 0.0) > 0.5
    done = jnp.where(take_all, 1.0, jnp.where(tie_max, 1.0, jnp.where(
        tie_zero, 1.0, flag(c_pos == top_k))))
    thr = jnp.where(take_all, ALL_THR, jnp.where(tie_max, rmax, jnp.where(
        tie_zero, 0.0, TINY)))
    take = jnp.where(take_all, BIG, jnp.where(tie_max, top_k, jnp.where(
        tie_zero, top_k - c_pos, BIG)))
    positive = c_pos > top_k
    lo = jnp.where(positive, TINY, rmin)
    c_lo = jnp.where(positive, c_pos, n_causal)
    hi = jnp.where(positive, rmax, 0.0)
    c_hi = jnp.where(positive, c_max, c_zero)
    log_target = float(np.log(top_k + 0.5))

    def cond(state):
        it, done = state[0], state[-1]
        return (it < MAX_SEARCH) & (jnp.min(done) < 0.5)

    def step(state):
        it, lo, hi, c_lo, c_hi, thr, take, done = state
        log_lo = jnp.log(c_lo)
        frac = (log_lo - log_target) / (log_lo - jnp.log(jnp.maximum(c_hi, 0.5)))
        frac = jnp.clip(frac, MIN_FRACTION, 1.0 - MIN_FRACTION)
        frac = jnp.where(it < INTERPOLATE_STEPS, frac, 0.5)
        guess = lo + (hi - lo) * frac
        halfway = 0.5 * lo + 0.5 * hi
        mid = jnp.where(guess <= lo, halfway, jnp.where(guess >= hi, halfway, guess))
        spent = jnp.where(mid <= lo, 1.0, flag(mid >= hi))
        c = count_ge(mid)
        finish = (1.0 - done) * jnp.maximum(spent, flag(c == top_k)) > 0.5
        thr = jnp.where(finish, jnp.where(spent > 0.5, lo, mid), thr)
        take = jnp.where(finish, jnp.where(spent > 0.5, top_k - c_hi, BIG), take)
        above = c > top_k
        below = c < top_k
        lo = jnp.where(above, mid, lo)
        c_lo = jnp.where(above, c, c_lo)
        hi = jnp.where(below, mid, hi)
        c_hi = jnp.where(below, c, c_hi)
        done = jnp.where(finish, 1.0, done)
        return it + 1, lo, hi, c_lo, c_hi, thr, take, done

    _, lo, _, _, c_hi, thr, take, done = lax.while_loop(
        cond, step, (jnp.int32(0), lo, hi, c_lo, c_hi, thr, take, done))
    thr = jnp.where(done < 0.5, lo, thr)
    take = jnp.where(done < 0.5, top_k - c_hi, take)

    ranked = jnp.max(flag(take < BIG)) > 0.5

    thr_l = per_row(thr)

    @pl.when(jnp.logical_not(ranked))
    def _():
        def block(keys, width, carry):
            x = sc_ref[:, keys]
            sc_ref[:, keys] = jnp.concatenate(
                [jnp.where(x[:, c * LANES:(c + 1) * LANES] >= thr_l, 0.0, NEG)
                 for c in range(width // LANES)], axis=1)
            return carry

        blocks(block, 0, i + 1)

    @pl.when(ranked)
    def _():
        before = (lax.broadcasted_iota(jnp.int32, (t, t), 0)
                  < lax.broadcasted_iota(jnp.int32, (t, t), 1))
        before = jnp.where(before, 1.0, 0.0).astype(jnp.bfloat16)
        thr_c = thr_l[:, 0:1]
        take_c = per_row(take)[:, 0:1]

        def mask_tile(j, seen):
            keys = pl.ds(pl.multiple_of(j * t, t), t)
            s = sc_ref[:, keys]
            eq = flag(s == thr_c)
            rank = seen + jnp.dot(eq.astype(jnp.bfloat16), before,
                                  preferred_element_type=jnp.float32)
            sc_ref[:, keys] = jnp.where(s > thr_c, 0.0, jnp.where(
                s == thr_c, jnp.where(rank < take_c, 0.0, NEG), NEG))
            return seen + jnp.sum(eq, axis=1, keepdims=True)

        lax.fori_loop(0, i + 1, mask_tile, jnp.zeros((t, 1), jnp.float32))


def _dsa_kernel(iq_ref, ikt_ref, misc_ref, q_ref, kt_ref, v_ref, o_ref, sc_ref,
                hi_ref, lo_ref, cnt_ref, s_ref, m_ref, acc_ref):
    i = pl.program_id(1)

    @pl.when(pl.program_id(2) == 0)
    def _():
        _index_mask(iq_ref, ikt_ref, misc_ref, sc_ref, hi_ref, lo_ref, cnt_ref, i)

    def bias(s, keys, width, diagonal):
        mask = sc_ref[:, keys]
        return s + jnp.concatenate([mask, mask], axis=0)

    o_ref[0] = _attend(_stack_heads(q_ref[0]), kt_ref, v_ref, bias, i,
                       s_ref, m_ref, acc_ref)


def _dsa_attention(iq, ik2_t, misc, aq, ak_t, av):
    b, s, _ = aq.shape
    t = ATT_TILE
    return pl.pallas_call(
        _dsa_kernel,
        grid=(b, s // t, N_HEAD_BLOCKS),
        in_specs=[
            pl.BlockSpec((1, t, N_IDX_HEADS * IDX_DIM), lambda bb, i, hp: (bb, i, 0)),
            pl.BlockSpec((1, LANES, s), lambda bb, i, hp: (bb, 0, 0)),
            pl.BlockSpec((1, t, LANES), lambda bb, i, hp: (bb, i, 0)),
            pl.BlockSpec((1, t, LANES), lambda bb, i, hp: (bb, i, hp)),
            pl.BlockSpec((1, LANES, s), lambda bb, i, hp: (bb, hp, 0)),
            pl.BlockSpec((1, s, HEADS_PER_BLOCK * LANES), lambda bb, i, hp: (bb, 0, hp)),
        ],
        out_specs=pl.BlockSpec((1, t, LANES), lambda bb, i, hp: (bb, i, hp)),
        out_shape=jax.ShapeDtypeStruct((b, s, WIDTH), jnp.float32),
        scratch_shapes=[pltpu.VMEM((t, s), jnp.float32),
                        pltpu.VMEM((t, LANES), jnp.float32),
                        pltpu.VMEM((t, LANES), jnp.float32),
                        pltpu.VMEM((t, LANES), jnp.float32)] + _attend_scratch(t, s),
        compiler_params=pltpu.CompilerParams(
            dimension_semantics=("arbitrary", "arbitrary", "arbitrary"),
            vmem_limit_bytes=VMEM_LIMIT),
        name="dsa_attention",
    )(iq, ik2_t, misc, aq, ak_t, av)


def _fox_kernel(q_ref, kt_ref, v_ref, cq_ref, ck_ref, o_ref, s_ref, m_ref, acc_ref):
    t = ATT_TILE
    cq = jnp.concatenate([cq_ref[0, 0, :, 0:1], cq_ref[0, 0, :, 1:2]], axis=0)

    def bias(s, keys, width, diagonal):
        ck = ck_ref[0, 0, :, keys]
        ck = jnp.concatenate([jnp.broadcast_to(ck[0:1], (t, width)),
                              jnp.broadcast_to(ck[1:2], (t, width))], axis=0)
        s = s + (cq - ck)
        if diagonal:
            below = (lax.broadcasted_iota(jnp.int32, (t, t), 1)
                     <= lax.broadcasted_iota(jnp.int32, (t, t), 0))
            s = jnp.where(jnp.concatenate([below, below], axis=0), s, NEG)
        return s

    o_ref[0] = _attend(_stack_heads(q_ref[0]), kt_ref, v_ref, bias, pl.program_id(2),
                       s_ref, m_ref, acc_ref)


def _fox_attention(fq, fk_t, fv, cum_q, cum_k):
    b, s, _ = fq.shape
    t = ATT_TILE
    return pl.pallas_call(
        _fox_kernel,
        grid=(b, N_HEAD_BLOCKS, s // t),
        in_specs=[
            pl.BlockSpec((1, t, LANES), lambda bb, hp, i: (bb, i, hp)),
            pl.BlockSpec((1, LANES, s), lambda bb, hp, i: (bb, hp, 0)),
            pl.BlockSpec((1, s, HEADS_PER_BLOCK * LANES), lambda bb, hp, i: (bb, 0, hp)),
            pl.BlockSpec((1, 1, t, HEADS_PER_BLOCK), lambda bb, hp, i: (bb, hp, i, 0)),
            pl.BlockSpec((1, 1, HEADS_PER_BLOCK, s), lambda bb, hp, i: (bb, hp, 0, 0)),
        ],
        out_specs=pl.BlockSpec((1, t, LANES), lambda bb, hp, i: (bb, i, hp)),
        out_shape=jax.ShapeDtypeStruct((b, s, WIDTH), jnp.float32),
        scratch_shapes=_attend_scratch(t, s),
        compiler_params=pltpu.CompilerParams(
            dimension_semantics=("arbitrary", "arbitrary", "arbitrary"),
            vmem_limit_bytes=VMEM_LIMIT),
        name="fox_attention",
    )(fq, fk_t, fv, cum_q, cum_k)


def _out_kernel(x_ref, ya_ref, yb_ref, gain_ref, wg_ref, bm_ref, wa_ref, wb_ref,
                wo_ref, ngain_ref, o_ref):
    x = x_ref[...]
    h = _rms(x, gain_ref[...]).astype(jnp.bfloat16)
    g = jnp.dot(h, wg_ref[...], preferred_element_type=jnp.float32)
    a_gate = g[:, :WIDTH]
    f_gate = g[:, WIDTH:2 * WIDTH]
    m_logit = g[:, 2 * WIDTH:] + bm_ref[...]
    za = (ya_ref[...] * (a_gate * jax.nn.sigmoid(a_gate))).astype(jnp.bfloat16)
    zb = (yb_ref[...] * (f_gate * jax.nn.sigmoid(f_gate))).astype(jnp.bfloat16)
    ua = jnp.dot(za, wa_ref[...], preferred_element_type=jnp.float32)
    ub = jnp.dot(zb, wb_ref[...], preferred_element_type=jnp.float32)
    gates = jax.nn.sigmoid(m_logit)
    merged = gates[:, :D_MODEL] * ua + gates[:, D_MODEL:] * ub
    y = x + jnp.dot(merged.astype(jnp.bfloat16), wo_ref[...],
                    preferred_element_type=jnp.float32)
    o_ref[...] = _rms(y, ngain_ref[...])


def _output(x2, ya, yb, gain, w_gates, b_merge, w_a, w_b, w_o, next_gain):
    m = x2.shape[0]
    tm = PROJ_ROWS
    row = lambda i: (i, 0)
    fixed = lambda i: (0, 0)
    return pl.pallas_call(
        _out_kernel,
        grid=(m // tm,),
        in_specs=[
            pl.BlockSpec((tm, D_MODEL), row),
            pl.BlockSpec((tm, WIDTH), row),
            pl.BlockSpec((tm, WIDTH), row),
            pl.BlockSpec((1, D_MODEL), fixed),
            pl.BlockSpec(w_gates.shape, fixed),
            pl.BlockSpec((1, 2 * D_MODEL), fixed),
            pl.BlockSpec(w_a.shape, fixed),
            pl.BlockSpec(w_b.shape, fixed),
            pl.BlockSpec(w_o.shape, fixed),
            pl.BlockSpec((1, D_MODEL), fixed),
        ],
        out_specs=pl.BlockSpec((tm, D_MODEL), row),
        out_shape=jax.ShapeDtypeStruct((m, D_MODEL), jnp.float32),
        compiler_params=pltpu.CompilerParams(
            dimension_semantics=("arbitrary",), vmem_limit_bytes=VMEM_LIMIT),
        name="out_proj",
    )(x2, ya, yb, gain, w_gates, b_merge, w_a, w_b, w_o, next_gain)


def _split_w_in(w):
    sizes = (WIDTH, WIDTH, WIDTH, WIDTH, N_IDX_HEADS * IDX_DIM, IDX_DIM, N_IDX_HEADS,
             WIDTH, WIDTH, WIDTH, WIDTH, N_HEADS, 2 * D_MODEL)
    points = np.cumsum(sizes)[:-1]
    return jnp.split(w, points, axis=-1)


def kernel(x, positions, norm_gain, w_in, b_forget, b_merge, w_branch_dsa,
           w_branch_fox, w_out, final_gain):
    b, s, d = x.shape
    depth = w_in.shape[0]
    bf16 = jnp.bfloat16

    half = jnp.arange(ROPE_HALF, dtype=jnp.float32)
    inv_freq = ROPE_THETA ** (-half * 2.0 / ROPE_DIM)
    dim = np.arange(LANES) % HEAD_DIM
    invf = jnp.where(dim < ROPE_DIM, inv_freq[dim % ROPE_HALF], 0.0)[None, :]
    pos2 = positions.reshape(b * s, 1)

    x2 = x.reshape(b * s, d)
    normed = None
    for l in range(depth):
        (w_aq, w_ak, w_av, w_ag, w_iq, w_ik, w_iw,
         w_fq, w_fk, w_fv, w_fg, w_fl, w_m) = _split_w_in(w_in[l])
        pad = jnp.zeros((d, LANES - N_IDX_HEADS - N_HEADS), w_in.dtype)
        w_all = jnp.concatenate(
            [w_aq, w_ak, w_av, w_fq, w_fk, w_fv, w_iq, w_ik, w_ik, w_iw, w_fl, pad],
            axis=1).astype(bf16)
        w_gates = jnp.concatenate([w_ag, w_fg, w_m], axis=1).astype(bf16)
        bf_row = jnp.zeros((1, LANES), jnp.float32).at[
            0, MISC_CUM:MISC_CUM + N_HEADS].set(b_forget[l])
        gain = norm_gain[l][None, :]
        next_gain = (norm_gain[l + 1] if l + 1 < depth else final_gain)[None, :]

        aq, ak_t, av, fq, fk_t, fv, iq, ik2_t, misc = _project(
            x2, pos2, gain, invf, w_all, bf_row, s)
        shape3 = lambda a: a.reshape(b, s, a.shape[-1])
        ya = _dsa_attention(shape3(iq), ik2_t, shape3(misc), shape3(aq), ak_t, shape3(av))
        cum = misc[:, MISC_CUM:MISC_CUM + N_HEADS].reshape(
            b, s, N_HEAD_BLOCKS, HEADS_PER_BLOCK)
        cum_q = cum.transpose(0, 2, 1, 3)
        cum_k = cum.transpose(0, 2, 3, 1)
        yb = _fox_attention(shape3(fq), fk_t, shape3(fv), cum_q, cum_k)
        normed = _output(x2, ya.reshape(b * s, WIDTH), yb.reshape(b * s, WIDTH),
                         gain, w_gates, b_merge[l][None, :],
                         w_branch_dsa[l].astype(bf16), w_branch_fox[l].astype(bf16),
                         w_out[l].astype(bf16), next_gain)
        assert depth == 1, "stacked layers need the un-normalised residual as well"
    return normed.reshape(b, s, d)
```

```python
import functools

import jax
import jax.numpy as jnp
import numpy as np
from jax import lax
from jax.experimental import pallas as pl
from jax.experimental.pallas import tpu as pltpu

D_MODEL = 1024
HEAD_DIM = 64
N_HEADS = 8
WIDTH = N_HEADS * HEAD_DIM
N_IDX_HEADS = 4
IDX_DIM = 64
TOPK_MAX = 256
ROPE_THETA = 500000.0
ROPE_DIM = HEAD_DIM // 4
ROPE_HALF = ROPE_DIM // 2
RMS_EPS = 1e-6
NEG = -1e30
LOG2_E = float(np.log2(np.e))

LANES = 128
SUBLANES = 8
HEADS_PER_BLOCK = LANES // HEAD_DIM
N_HEAD_BLOCKS = N_HEADS // HEADS_PER_BLOCK
VMEM_LIMIT = 56 * 1024 * 1024

PROJ_ROWS = 512
ATT_TILE = 256
WIDE_KEYS = 1024
COUNT_CHAINS = 4

TINY = float(np.finfo(np.float32).tiny)
ALL_THR = -3.0e38
BIG = 1.0e9
MAX_SEARCH = 400
INTERPOLATE_STEPS = 12
MIN_FRACTION = 1.0 / 32.0

MISC_IW = 0
MISC_CUM = N_IDX_HEADS


def _rms(x, gain):
    ms = jnp.mean(x * x, axis=-1, keepdims=True)
    return x * lax.rsqrt(ms + RMS_EPS) * gain


def _proj_kernel(x_ref, pos_ref, gain_ref, invf_ref, w_ref, bf_ref,
                 aq_ref, ak_ref, av_ref, fq_ref, fk_ref, fv_ref, iq_ref, ik_ref,
                 misc_ref, carry_ref, *, tiles_per_batch):
    tm = x_ref.shape[0]
    h = _rms(x_ref[...], gain_ref[...]).astype(jnp.bfloat16)

    ang = pos_ref[...].astype(jnp.float32) * invf_ref[...]
    cos = jnp.cos(ang)
    sin = jnp.sin(ang)
    d = lax.broadcasted_iota(jnp.int32, (tm, LANES), 1) % HEAD_DIM
    sin_lo = jnp.where(d < ROPE_HALF, -sin, 0.0)
    sin_hi = jnp.where(d >= ROPE_HALF, sin, 0.0)

    def rope(t):
        up = pltpu.roll(t, LANES - ROPE_HALF, 1)
        dn = pltpu.roll(t, ROPE_HALF, 1)
        return t * cos + up * sin_lo + dn * sin_hi

    def blocks_of(col0, width, use_rope, scale):
        r = jnp.dot(h, w_ref[:, col0:col0 + width],
                    preferred_element_type=jnp.float32)
        for c in range(width // LANES):
            t = r[:, c * LANES:(c + 1) * LANES]
            if use_rope:
                t = rope(t)
            if scale != 1.0:
                t = t * scale
            yield c, t

    def emit(out_ref, col0, width, use_rope, scale):
        for c, t in blocks_of(col0, width, use_rope, scale):
            out_ref[:, c * LANES:(c + 1) * LANES] = t.astype(out_ref.dtype)

    def emit_transposed(out_ref, col0, width, use_rope):
        for c, t in blocks_of(col0, width, use_rope, 1.0):
            out_ref[0, c * LANES:(c + 1) * LANES, :] = jnp.transpose(t).astype(
                out_ref.dtype)

    def emit_values(out_ref, col0):
        lane = lax.broadcasted_iota(jnp.int32, (tm, LANES), 1)
        tail = jnp.where(lane == HEAD_DIM, 1.0, 0.0)
        for c, t in blocks_of(col0, WIDTH, False, 1.0):
            pair = (t, pltpu.roll(t, HEAD_DIM, 1))
            for k in range(HEADS_PER_BLOCK):
                head = c * HEADS_PER_BLOCK + k
                out_ref[:, head * LANES:(head + 1) * LANES] = jnp.where(
                    lane < HEAD_DIM, pair[k], tail).astype(out_ref.dtype)

    qk_scale = HEAD_DIM ** -0.5 * LOG2_E
    emit(aq_ref, 0 * WIDTH, WIDTH, True, qk_scale)
    emit_transposed(ak_ref, 1 * WIDTH, WIDTH, True)
    emit_values(av_ref, 2 * WIDTH)
    emit(fq_ref, 3 * WIDTH, WIDTH, False, qk_scale)
    emit_transposed(fk_ref, 4 * WIDTH, WIDTH, False)
    emit_values(fv_ref, 5 * WIDTH)
    col = 6 * WIDTH
    emit(iq_ref, col, N_IDX_HEADS * IDX_DIM, True, IDX_DIM ** -0.5)
    col += N_IDX_HEADS * IDX_DIM
    emit(ik_ref, col, LANES, True, 1.0)
    col += LANES

    r = jnp.dot(h, w_ref[:, col:col + LANES], preferred_element_type=jnp.float32)
    z = r + bf_ref[...]
    logf = jnp.minimum(z, 0.0) - jnp.log1p(jnp.exp(-jnp.abs(z)))

    @pl.when(pl.program_id(0) % tiles_per_batch == 0)
    def _():
        carry_ref[...] = jnp.zeros_like(carry_ref)

    rows = lax.broadcasted_iota(jnp.int32, (tm, tm), 0)
    cols = lax.broadcasted_iota(jnp.int32, (tm, tm), 1)
    tri = jnp.where(cols <= rows, 1.0, 0.0).astype(jnp.bfloat16)
    hi = logf.astype(jnp.bfloat16)
    rem = logf - hi.astype(jnp.float32)
    mid = rem.astype(jnp.bfloat16)
    lo = (rem - mid.astype(jnp.float32)).astype(jnp.bfloat16)
    cum = (jnp.dot(tri, hi, preferred_element_type=jnp.float32)
           + jnp.dot(tri, mid, preferred_element_type=jnp.float32)
           + jnp.dot(tri, lo, preferred_element_type=jnp.float32)
           + carry_ref[...])
    carry_ref[...] = cum[tm - 1:tm, :]

    lane = lax.broadcasted_iota(jnp.int32, (tm, LANES), 1)
    misc = jnp.where(lane < MISC_CUM, r * (N_IDX_HEADS ** -0.5),
                     jnp.where(lane < MISC_CUM + N_HEADS, cum * LOG2_E, 0.0))
    misc_ref[...] = misc


def _project(x2, pos2, gain, invf, w_all, bf_row, seq):
    m = x2.shape[0]
    tm = PROJ_ROWS
    n_w = w_all.shape[1]
    row = lambda i: (i, 0)
    fixed = lambda i: (0, 0)
    bf16 = jnp.bfloat16
    tiles_per_batch = seq // tm
    batch = m // seq
    transposed = lambda i: (i // tiles_per_batch, 0, i % tiles_per_batch)

    def rows(width, dtype=bf16):
        return jax.ShapeDtypeStruct((m, width), dtype), pl.BlockSpec((tm, width), row)

    def keys(width):
        return (jax.ShapeDtypeStruct((batch, width, seq), bf16),
                pl.BlockSpec((1, width, tm), transposed))

    outs = [rows(WIDTH), keys(WIDTH), rows(N_HEADS * LANES),
            rows(WIDTH), keys(WIDTH), rows(N_HEADS * LANES),
            rows(N_IDX_HEADS * IDX_DIM), rows(LANES),
            rows(LANES, jnp.float32)]
    out_shape = [o[0] for o in outs]
    out_specs = [o[1] for o in outs]
    return pl.pallas_call(
        functools.partial(_proj_kernel, tiles_per_batch=tiles_per_batch),
        grid=(m // tm,),
        in_specs=[
            pl.BlockSpec((tm, D_MODEL), row),
            pl.BlockSpec((tm, 1), row),
            pl.BlockSpec((1, D_MODEL), fixed),
            pl.BlockSpec((1, LANES), fixed),
            pl.BlockSpec((D_MODEL, n_w), fixed),
            pl.BlockSpec((1, LANES), fixed),
        ],
        out_specs=out_specs,
        out_shape=out_shape,
        scratch_shapes=[pltpu.VMEM((1, LANES), jnp.float32)],
        compiler_params=pltpu.CompilerParams(
            dimension_semantics=("arbitrary",), vmem_limit_bytes=VMEM_LIMIT),
        name="in_proj",
    )(x2, pos2, gain, invf, w_all, bf_row)


def _half_select(x, first):
    lane = lax.broadcasted_iota(jnp.int32, x.shape, x.ndim - 1)
    keep = (lane < HEAD_DIM) if first else (lane >= HEAD_DIM)
    return jnp.where(keep, x, jnp.zeros_like(x))


def _stack_heads(q):
    return jnp.concatenate([_half_select(q, True), _half_select(q, False)], axis=0)


def _attend(qs, kt_ref, v_ref, bias, i, s_ref, m_ref, acc_ref):
    t = ATT_TILE
    n_wide = (i * t) // WIDE_KEYS
    first_narrow = n_wide * (WIDE_KEYS // t)

    def phase_a(start, width, diagonal):
        keys = pl.ds(pl.multiple_of(start, width), width)
        s = jnp.dot(qs, kt_ref[0, :, keys], preferred_element_type=jnp.float32)
        s = bias(s, keys, width, diagonal)
        s_ref[:, keys] = s
        m = m_ref[...]
        for c in range(width // LANES):
            m = jnp.maximum(m, s[:, c * LANES:(c + 1) * LANES])
        m_ref[...] = m

    def phase_b(start, width, m):
        keys = pl.ds(pl.multiple_of(start, width), width)
        p = jnp.exp2(s_ref[:, keys] - m).astype(jnp.bfloat16)
        for k in range(HEADS_PER_BLOCK):
            acc_ref[k * t:(k + 1) * t, :] += jnp.dot(
                p[k * t:(k + 1) * t], v_ref[0, keys, k * LANES:(k + 1) * LANES],
                preferred_element_type=jnp.float32)

    def loop(lo, hi, fn):
        def body(j, carry):
            fn(j)
            return carry
        lax.fori_loop(lo, hi, body, 0)

    m_ref[...] = jnp.full(m_ref.shape, NEG, jnp.float32)
    loop(0, n_wide, lambda j: phase_a(j * WIDE_KEYS, WIDE_KEYS, False))
    loop(first_narrow, i, lambda j: phase_a(j * t, t, False))
    phase_a(i * t, t, True)

    m = jnp.max(m_ref[...], axis=1, keepdims=True)
    acc_ref[...] = jnp.zeros(acc_ref.shape, jnp.float32)
    loop(0, n_wide, lambda j: phase_b(j * WIDE_KEYS, WIDE_KEYS, m))
    loop(first_narrow, i + 1, lambda j: phase_b(j * t, t, m))

    acc = acc_ref[...]
    o = acc / acc[:, HEAD_DIM:HEAD_DIM + 1]
    lane = lax.broadcasted_iota(jnp.int32, (t, LANES), 1)
    return jnp.where(lane < HEAD_DIM, o[:t], pltpu.roll(o[t:], HEAD_DIM, 1))


def _attend_scratch(t, s):
    rows = HEADS_PER_BLOCK * t
    return [pltpu.VMEM((rows, s), jnp.float32),
            pltpu.VMEM((rows, LANES), jnp.float32),
            pltpu.VMEM((rows, LANES), jnp.float32)]


def _index_mask(iq_ref, ik_ref, misc_ref, sct_ref, sc_ref, hi_ref, lo_ref, i):
    t = ATT_TILE
    top_k = float(TOPK_MAX)
    n_wide = (i * t) // WIDE_KEYS
    first_narrow = n_wide * (WIDE_KEYS // t)

    def blocks(fn, carry, last_narrow):
        def wide(j, c):
            return fn(pl.ds(pl.multiple_of(j * WIDE_KEYS, WIDE_KEYS), WIDE_KEYS),
                      WIDE_KEYS, c)

        def narrow(j, c):
            return fn(pl.ds(pl.multiple_of(j * t, t), t), t, c)

        carry = lax.fori_loop(0, n_wide, wide, carry)
        return lax.fori_loop(first_narrow, last_narrow, narrow, carry)

    def fold(x, op, rows=SUBLANES):
        return op(x.reshape(x.shape[0] // rows, rows, t), axis=0)

    iq = iq_ref[0]
    qt = []
    for h in range(N_IDX_HEADS):
        block = _half_select(iq[:, (h // 2) * LANES:(h // 2 + 1) * LANES], h % 2 == 0)
        qt.append(jnp.transpose(block.astype(jnp.float32)).astype(jnp.bfloat16))
    misc_t = jnp.transpose(misc_ref[0])
    wh = [misc_t[MISC_IW + h:MISC_IW + h + 1, :] for h in range(N_IDX_HEADS)]

    def scores(keys):
        k = ik_ref[0, keys, :]
        sc = None
        for h in range(N_IDX_HEADS):
            rel = jnp.dot(k, qt[h], preferred_element_type=jnp.float32)
            term = wh[h] * jnp.maximum(rel, 0.0)
            sc = term if sc is None else sc + term
        return sc

    def track(for_max, for_min):
        hi_ref[...] = jnp.maximum(hi_ref[...], fold(for_max, jnp.max))
        lo_ref[...] = jnp.minimum(lo_ref[...], fold(for_min, jnp.min))

    def score_block(keys, width, carry):
        sc = scores(keys)
        sct_ref[keys, :] = sc
        track(sc, sc)
        return carry

    hi_ref[...] = jnp.full(hi_ref.shape, -jnp.inf, jnp.float32)
    lo_ref[...] = jnp.full(lo_ref.shape, jnp.inf, jnp.float32)
    blocks(score_block, 0, i)
    diag = pl.ds(pl.multiple_of(i * t, t), t)
    sc = scores(diag)
    causal = (lax.broadcasted_iota(jnp.int32, (t, t), 0)
              <= lax.broadcasted_iota(jnp.int32, (t, t), 1))
    masked = jnp.where(causal, sc, -jnp.inf)
    sct_ref[diag, :] = masked
    track(masked, jnp.where(causal, sc, jnp.inf))

    def count_ge(thr):
        def block(keys, width, acc):
            hit = jnp.where(sct_ref[keys, :] >= thr, 1.0, 0.0)
            return acc + fold(hit, jnp.sum, COUNT_CHAINS * SUBLANES)

        acc = blocks(block, jnp.zeros((COUNT_CHAINS * SUBLANES, t), jnp.float32), i + 1)
        return jnp.sum(acc, axis=0, keepdims=True)

    def flag(cond):
        return jnp.where(cond, 1.0, 0.0)

    rmax = jnp.max(hi_ref[...], axis=0, keepdims=True)
    rmin = jnp.min(lo_ref[...], axis=0, keepdims=True)
    n_causal = (i * t + 1 + lax.broadcasted_iota(jnp.int32, (1, t), 1)
                ).astype(jnp.float32)
    c_pos = count_ge(jnp.full((1, t), TINY, jnp.float32))
    c_zero = count_ge(jnp.zeros((1, t), jnp.float32))
    c_max = count_ge(rmax)
    take_all = n_causal <= top_k
    tie_max = c_max >= top_k
    tie_zero = jnp.where(c_pos < top_k, flag(c_zero >= top_k), 0.0) > 0.5
    done = jnp.where(take_all, 1.0, jnp.where(tie_max, 1.0, jnp.where(
        tie_zero, 1.0, flag(c_pos == top_k))))
    thr = jnp.where(take_all, ALL_THR, jnp.where(tie_max, rmax, jnp.where(
        tie_zero, 0.0, TINY)))
    take = jnp.where(take_all, BIG, jnp.where(tie_max, top_k, jnp.where(
        tie_zero, top_k - c_pos, BIG)))
    positive = c_pos > top_k
    lo = jnp.where(positive, TINY, rmin)
    c_lo = jnp.where(positive, c_pos, n_causal)
    hi = jnp.where(positive, rmax, 0.0)
    c_hi = jnp.where(positive, c_max, c_zero)
    log_target = float(np.log(top_k + 0.5))

    def cond(state):
        it, done = state[0], state[-1]
        return (it < MAX_SEARCH) & (jnp.min(done) < 0.5)

    def step(state):
        it, lo, hi, c_lo, c_hi, thr, take, done = state
        log_lo = jnp.log(c_lo)
        frac = (log_lo - log_target) / (log_lo - jnp.log(jnp.maximum(c_hi, 0.5)))
        frac = jnp.clip(frac, MIN_FRACTION, 1.0 - MIN_FRACTION)
        frac = jnp.where(it < INTERPOLATE_STEPS, frac, 0.5)
        guess = lo + (hi - lo) * frac
        halfway = 0.5 * lo + 0.5 * hi
        mid = jnp.where(guess <= lo, halfway, jnp.where(guess >= hi, halfway, guess))
        spent = jnp.where(mid <= lo, 1.0, flag(mid >= hi))
        c = count_ge(mid)
        finish = (1.0 - done) * jnp.maximum(spent, flag(c == top_k)) > 0.5
        thr = jnp.where(finish, jnp.where(spent > 0.5, lo, mid), thr)
        take = jnp.where(finish, jnp.where(spent > 0.5, top_k - c_hi, BIG), take)
        above = c > top_k
        below = c < top_k
        lo = jnp.where(above, mid, lo)
        c_lo = jnp.where(above, c, c_lo)
        hi = jnp.where(below, mid, hi)
        c_hi = jnp.where(below, c, c_hi)
        done = jnp.where(finish, 1.0, done)
        return it + 1, lo, hi, c_lo, c_hi, thr, take, done

    _, lo, _, _, c_hi, thr, take, done = lax.while_loop(
        cond, step, (jnp.int32(0), lo, hi, c_lo, c_hi, thr, take, done))
    thr = jnp.where(done < 0.5, lo, thr)
    take = jnp.where(done < 0.5, top_k - c_hi, take)

    ranked = jnp.max(flag(take < BIG)) > 0.5

    @pl.when(jnp.logical_not(ranked))
    def _():
        def tile(j, carry):
            keys = pl.ds(pl.multiple_of(j * t, t), t)
            sc_ref[:, keys] = jnp.transpose(
                jnp.where(sct_ref[keys, :] >= thr, 0.0, NEG))
            return carry

        lax.fori_loop(0, i + 1, tile, 0)

    @pl.when(ranked)
    def _():
        earlier = (lax.broadcasted_iota(jnp.int32, (t, t), 1)
                   < lax.broadcasted_iota(jnp.int32, (t, t), 0))
        earlier = jnp.where(earlier, 1.0, 0.0).astype(jnp.bfloat16)

        def tile(j, seen):
            keys = pl.ds(pl.multiple_of(j * t, t), t)
            s = sct_ref[keys, :]
            eq = flag(s == thr)
            rank = seen + jnp.dot(earlier, eq.astype(jnp.bfloat16),
                                  preferred_element_type=jnp.float32)
            sc_ref[:, keys] = jnp.transpose(jnp.where(s > thr, 0.0, jnp.where(
                s == thr, jnp.where(rank < take, 0.0, NEG), NEG)))
            return seen + jnp.sum(eq, axis=0, keepdims=True)

        lax.fori_loop(0, i + 1, tile, jnp.zeros((1, t), jnp.float32))


def _dsa_kernel(iq_ref, ik_ref, misc_ref, q_ref, kt_ref, v_ref, o_ref, sct_ref, sc_ref,
                hi_ref, lo_ref, s_ref, m_ref, acc_ref):
    i = pl.program_id(1)

    @pl.when(pl.program_id(2) == 0)
    def _():
        _index_mask(iq_ref, ik_ref, misc_ref, sct_ref, sc_ref, hi_ref, lo_ref, i)

    def bias(s, keys, width, diagonal):
        mask = sc_ref[:, keys]
        return s + jnp.concatenate([mask, mask], axis=0)

    o_ref[0] = _attend(_stack_heads(q_ref[0]), kt_ref, v_ref, bias, i,
                       s_ref, m_ref, acc_ref)


def _dsa_attention(iq, ik2, misc, aq, ak_t, av):
    b, s, _ = aq.shape
    t = ATT_TILE
    return pl.pallas_call(
        _dsa_kernel,
        grid=(b, s // t, N_HEAD_BLOCKS),
        in_specs=[
            pl.BlockSpec((1, t, N_IDX_HEADS * IDX_DIM), lambda bb, i, hp: (bb, i, 0)),
            pl.BlockSpec((1, s, LANES), lambda bb, i, hp: (bb, 0, 0)),
            pl.BlockSpec((1, t, LANES), lambda bb, i, hp: (bb, i, 0)),
            pl.BlockSpec((1, t, LANES), lambda bb, i, hp: (bb, i, hp)),
            pl.BlockSpec((1, LANES, s), lambda bb, i, hp: (bb, hp, 0)),
            pl.BlockSpec((1, s, HEADS_PER_BLOCK * LANES), lambda bb, i, hp: (bb, 0, hp)),
        ],
        out_specs=pl.BlockSpec((1, t, LANES), lambda bb, i, hp: (bb, i, hp)),
        out_shape=jax.ShapeDtypeStruct((b, s, WIDTH), jnp.float32),
        scratch_shapes=[pltpu.VMEM((s, t), jnp.float32),
                        pltpu.VMEM((t, s), jnp.float32),
                        pltpu.VMEM((SUBLANES, t), jnp.float32),
                        pltpu.VMEM((SUBLANES, t), jnp.float32)] + _attend_scratch(t, s),
        compiler_params=pltpu.CompilerParams(
            dimension_semantics=("arbitrary", "arbitrary", "arbitrary"),
            vmem_limit_bytes=VMEM_LIMIT),
        name="dsa_attention",
    )(iq, ik2, misc, aq, ak_t, av)


def _fox_kernel(q_ref, kt_ref, v_ref, cq_ref, ck_ref, o_ref, s_ref, m_ref, acc_ref):
    t = ATT_TILE
    cq = jnp.concatenate([cq_ref[0, 0, :, 0:1], cq_ref[0, 0, :, 1:2]], axis=0)

    def bias(s, keys, width, diagonal):
        ck = ck_ref[0, 0, :, keys]
        ck = jnp.concatenate([jnp.broadcast_to(ck[0:1], (t, width)),
                              jnp.broadcast_to(ck[1:2], (t, width))], axis=0)
        s = s + (cq - ck)
        if diagonal:
            below = (lax.broadcasted_iota(jnp.int32, (t, t), 1)
                     <= lax.broadcasted_iota(jnp.int32, (t, t), 0))
            s = jnp.where(jnp.concatenate([below, below], axis=0), s, NEG)
        return s

    o_ref[0] = _attend(_stack_heads(q_ref[0]), kt_ref, v_ref, bias, pl.program_id(2),
                       s_ref, m_ref, acc_ref)


def _fox_attention(fq, fk_t, fv, cum_q, cum_k):
    b, s, _ = fq.shape
    t = ATT_TILE
    return pl.pallas_call(
        _fox_kernel,
        grid=(b, N_HEAD_BLOCKS, s // t),
        in_specs=[
            pl.BlockSpec((1, t, LANES), lambda bb, hp, i: (bb, i, hp)),
            pl.BlockSpec((1, LANES, s), lambda bb, hp, i: (bb, hp, 0)),
            pl.BlockSpec((1, s, HEADS_PER_BLOCK * LANES), lambda bb, hp, i: (bb, 0, hp)),
            pl.BlockSpec((1, 1, t, HEADS_PER_BLOCK), lambda bb, hp, i: (bb, hp, i, 0)),
            pl.BlockSpec((1, 1, HEADS_PER_BLOCK, s), lambda bb, hp, i: (bb, hp, 0, 0)),
        ],
        out_specs=pl.BlockSpec((1, t, LANES), lambda bb, hp, i: (bb, i, hp)),
        out_shape=jax.ShapeDtypeStruct((b, s, WIDTH), jnp.float32),
        scratch_shapes=_attend_scratch(t, s),
        compiler_params=pltpu.CompilerParams(
            dimension_semantics=("arbitrary", "arbitrary", "arbitrary"),
            vmem_limit_bytes=VMEM_LIMIT),
        name="fox_attention",
    )(fq, fk_t, fv, cum_q, cum_k)


def _out_kernel(x_ref, ya_ref, yb_ref, gain_ref, wg_ref, bm_ref, wa_ref, wb_ref,
                wo_ref, ngain_ref, o_ref):
    x = x_ref[...]
    h = _rms(x, gain_ref[...]).astype(jnp.bfloat16)
    g = jnp.dot(h, wg_ref[...], preferred_element_type=jnp.float32)
    a_gate = g[:, :WIDTH]
    f_gate = g[:, WIDTH:2 * WIDTH]
    m_logit = g[:, 2 * WIDTH:] + bm_ref[...]
    za = (ya_ref[...] * (a_gate * jax.nn.sigmoid(a_gate))).astype(jnp.bfloat16)
    zb = (yb_ref[...] * (f_gate * jax.nn.sigmoid(f_gate))).astype(jnp.bfloat16)
    ua = jnp.dot(za, wa_ref[...], preferred_element_type=jnp.float32)
    ub = jnp.dot(zb, wb_ref[...], preferred_element_type=jnp.float32)
    gates = jax.nn.sigmoid(m_logit)
    merged = gates[:, :D_MODEL] * ua + gates[:, D_MODEL:] * ub
    y = x + jnp.dot(merged.astype(jnp.bfloat16), wo_ref[...],
                    preferred_element_type=jnp.float32)
    o_ref[...] = _rms(y, ngain_ref[...])


def _output(x2, ya, yb, gain, w_gates, b_merge, w_a, w_b, w_o, next_gain):
    m = x2.shape[0]
    tm = PROJ_ROWS
    row = lambda i: (i, 0)
    fixed = lambda i: (0, 0)
    return pl.pallas_call(
        _out_kernel,
        grid=(m // tm,),
        in_specs=[
            pl.BlockSpec((tm, D_MODEL), row),
            pl.BlockSpec((tm, WIDTH), row),
            pl.BlockSpec((tm, WIDTH), row),
            pl.BlockSpec((1, D_MODEL), fixed),
            pl.BlockSpec(w_gates.shape, fixed),
            pl.BlockSpec((1, 2 * D_MODEL), fixed),
            pl.BlockSpec(w_a.shape, fixed),
            pl.BlockSpec(w_b.shape, fixed),
            pl.BlockSpec(w_o.shape, fixed),
            pl.BlockSpec((1, D_MODEL), fixed),
        ],
        out_specs=pl.BlockSpec((tm, D_MODEL), row),
        out_shape=jax.ShapeDtypeStruct((m, D_MODEL), jnp.float32),
        compiler_params=pltpu.CompilerParams(
            dimension_semantics=("arbitrary",), vmem_limit_bytes=VMEM_LIMIT),
        name="out_proj",
    )(x2, ya, yb, gain, w_gates, b_merge, w_a, w_b, w_o, next_gain)


def _split_w_in(w):
    sizes = (WIDTH, WIDTH, WIDTH, WIDTH, N_IDX_HEADS * IDX_DIM, IDX_DIM, N_IDX_HEADS,
             WIDTH, WIDTH, WIDTH, WIDTH, N_HEADS, 2 * D_MODEL)
    points = np.cumsum(sizes)[:-1]
    return jnp.split(w, points, axis=-1)


def kernel(x, positions, norm_gain, w_in, b_forget, b_merge, w_branch_dsa,
           w_branch_fox, w_out, final_gain):
    b, s, d = x.shape
    depth = w_in.shape[0]
    bf16 = jnp.bfloat16

    half = jnp.arange(ROPE_HALF, dtype=jnp.float32)
    inv_freq = ROPE_THETA ** (-half * 2.0 / ROPE_DIM)
    dim = np.arange(LANES) % HEAD_DIM
    invf = jnp.where(dim < ROPE_DIM, inv_freq[dim % ROPE_HALF], 0.0)[None, :]
    pos2 = positions.reshape(b * s, 1)

    x2 = x.reshape(b * s, d)
    normed = None
    for l in range(depth):
        (w_aq, w_ak, w_av, w_ag, w_iq, w_ik, w_iw,
         w_fq, w_fk, w_fv, w_fg, w_fl, w_m) = _split_w_in(w_in[l])
        pad = jnp.zeros((d, LANES - N_IDX_HEADS - N_HEADS), w_in.dtype)
        w_all = jnp.concatenate(
            [w_aq, w_ak, w_av, w_fq, w_fk, w_fv, w_iq, w_ik, w_ik, w_iw, w_fl, pad],
            axis=1).astype(bf16)
        w_gates = jnp.concatenate([w_ag, w_fg, w_m], axis=1).astype(bf16)
        bf_row = jnp.zeros((1, LANES), jnp.float32).at[
            0, MISC_CUM:MISC_CUM + N_HEADS].set(b_forget[l])
        gain = norm_gain[l][None, :]
        next_gain = (norm_gain[l + 1] if l + 1 < depth else final_gain)[None, :]

        aq, ak_t, av, fq, fk_t, fv, iq, ik2, misc = _project(
            x2, pos2, gain, invf, w_all, bf_row, s)
        shape3 = lambda a: a.reshape(b, s, a.shape[-1])
        ya = _dsa_attention(shape3(iq), shape3(ik2), shape3(misc), shape3(aq), ak_t,
                            shape3(av))
        cum = misc[:, MISC_CUM:MISC_CUM + N_HEADS].reshape(
            b, s, N_HEAD_BLOCKS, HEADS_PER_BLOCK)
        cum_q = cum.transpose(0, 2, 1, 3)
        cum_k = cum.transpose(0, 2, 3, 1)
        yb = _fox_attention(shape3(fq), fk_t, shape3(fv), cum_q, cum_k)
        normed = _output(x2, ya.reshape(b * s, WIDTH), yb.reshape(b * s, WIDTH),
                         gain, w_gates, b_merge[l][None, :],
                         w_branch_dsa[l].astype(bf16), w_branch_fox[l].astype(bf16),
                         w_out[l].astype(bf16), next_gain)
        assert depth == 1, "stacked layers need the un-normalised residual as well"
    return normed.reshape(b, s, d)
```

```python
import functools

import jax
import jax.numpy as jnp
import numpy as np
from jax import lax
from jax.experimental import pallas as pl
from jax.experimental.pallas import tpu as pltpu

D_MODEL = 1024
HEAD_DIM = 64
N_HEADS = 8
WIDTH = N_HEADS * HEAD_DIM
N_IDX_HEADS = 4
IDX_DIM = 64
TOPK_MAX = 256
ROPE_THETA = 500000.0
ROPE_DIM = HEAD_DIM // 4
ROPE_HALF = ROPE_DIM // 2
RMS_EPS = 1e-6
NEG = -1e30
LOG2_E = float(np.log2(np.e))

LANES = 128
SUBLANES = 8
HEADS_PER_BLOCK = LANES // HEAD_DIM
N_HEAD_BLOCKS = N_HEADS // HEADS_PER_BLOCK
VMEM_LIMIT = 56 * 1024 * 1024

PROJ_ROWS = 512
ATT_TILE = 256
WIDE_KEYS = 1024
ATTEND_WIDTHS = (2048, 1024, 512, 256)
COUNT_CHAINS = 4

TINY = float(np.finfo(np.float32).tiny)
ALL_THR = -3.0e38
BIG = 1.0e9
MAX_SEARCH = 400
INTERPOLATE_STEPS = 12
MIN_FRACTION = 1.0 / 32.0

MISC_IW = 0
MISC_CUM = N_IDX_HEADS


def _rms(x, gain):
    ms = jnp.mean(x * x, axis=-1, keepdims=True)
    return x * lax.rsqrt(ms + RMS_EPS) * gain


def _proj_kernel(x_ref, pos_ref, gain_ref, invf_ref, w_ref, bf_ref,
                 aq_ref, ak_ref, av_ref, fq_ref, fk_ref, fv_ref, iq_ref, ik_ref,
                 misc_ref, carry_ref, *, tiles_per_batch):
    tm = x_ref.shape[0]
    h = _rms(x_ref[...], gain_ref[...]).astype(jnp.bfloat16)

    ang = pos_ref[...].astype(jnp.float32) * invf_ref[...]
    cos = jnp.cos(ang)
    sin = jnp.sin(ang)
    d = lax.broadcasted_iota(jnp.int32, (tm, LANES), 1) % HEAD_DIM
    sin_lo = jnp.where(d < ROPE_HALF, -sin, 0.0)
    sin_hi = jnp.where(d >= ROPE_HALF, sin, 0.0)

    def rope(t):
        up = pltpu.roll(t, LANES - ROPE_HALF, 1)
        dn = pltpu.roll(t, ROPE_HALF, 1)
        return t * cos + up * sin_lo + dn * sin_hi

    def blocks_of(col0, width, use_rope, scale):
        r = jnp.dot(h, w_ref[:, col0:col0 + width],
                    preferred_element_type=jnp.float32)
        for c in range(width // LANES):
            t = r[:, c * LANES:(c + 1) * LANES]
            if use_rope:
                t = rope(t)
            if scale != 1.0:
                t = t * scale
            yield c, t

    def emit(out_ref, col0, width, use_rope, scale):
        for c, t in blocks_of(col0, width, use_rope, scale):
            out_ref[:, c * LANES:(c + 1) * LANES] = t.astype(out_ref.dtype)

    def emit_transposed(out_ref, col0, width, use_rope):
        for c, t in blocks_of(col0, width, use_rope, 1.0):
            out_ref[0, c * LANES:(c + 1) * LANES, :] = jnp.transpose(t).astype(
                out_ref.dtype)

    def emit_values(out_ref, col0):
        lane = lax.broadcasted_iota(jnp.int32, (tm, LANES), 1)
        tail = jnp.where(lane == HEAD_DIM, 1.0, 0.0)
        for c, t in blocks_of(col0, WIDTH, False, 1.0):
            pair = (t, pltpu.roll(t, HEAD_DIM, 1))
            for k in range(HEADS_PER_BLOCK):
                head = c * HEADS_PER_BLOCK + k
                out_ref[:, head * LANES:(head + 1) * LANES] = jnp.where(
                    lane < HEAD_DIM, pair[k], tail).astype(out_ref.dtype)

    qk_scale = HEAD_DIM ** -0.5 * LOG2_E
    emit(aq_ref, 0 * WIDTH, WIDTH, True, qk_scale)
    emit_transposed(ak_ref, 1 * WIDTH, WIDTH, True)
    emit_values(av_ref, 2 * WIDTH)
    emit(fq_ref, 3 * WIDTH, WIDTH, False, qk_scale)
    emit_transposed(fk_ref, 4 * WIDTH, WIDTH, False)
    emit_values(fv_ref, 5 * WIDTH)
    col = 6 * WIDTH
    emit(iq_ref, col, N_IDX_HEADS * IDX_DIM, True, IDX_DIM ** -0.5)
    col += N_IDX_HEADS * IDX_DIM
    emit(ik_ref, col, LANES, True, 1.0)
    col += LANES

    r = jnp.dot(h, w_ref[:, col:col + LANES], preferred_element_type=jnp.float32)
    z = r + bf_ref[...]
    logf = jnp.minimum(z, 0.0) - jnp.log1p(jnp.exp(-jnp.abs(z)))

    @pl.when(pl.program_id(0) % tiles_per_batch == 0)
    def _():
        carry_ref[...] = jnp.zeros_like(carry_ref)

    rows = lax.broadcasted_iota(jnp.int32, (tm, tm), 0)
    cols = lax.broadcasted_iota(jnp.int32, (tm, tm), 1)
    tri = jnp.where(cols <= rows, 1.0, 0.0).astype(jnp.bfloat16)
    hi = logf.astype(jnp.bfloat16)
    rem = logf - hi.astype(jnp.float32)
    mid = rem.astype(jnp.bfloat16)
    lo = (rem - mid.astype(jnp.float32)).astype(jnp.bfloat16)
    cum = (jnp.dot(tri, hi, preferred_element_type=jnp.float32)
           + jnp.dot(tri, mid, preferred_element_type=jnp.float32)
           + jnp.dot(tri, lo, preferred_element_type=jnp.float32)
           + carry_ref[...])
    carry_ref[...] = cum[tm - 1:tm, :]

    lane = lax.broadcasted_iota(jnp.int32, (tm, LANES), 1)
    misc = jnp.where(lane < MISC_CUM, r * (N_IDX_HEADS ** -0.5),
                     jnp.where(lane < MISC_CUM + N_HEADS, cum * LOG2_E, 0.0))
    misc_ref[...] = misc


def _project(x2, pos2, gain, invf, w_all, bf_row, seq):
    m = x2.shape[0]
    tm = PROJ_ROWS
    n_w = w_all.shape[1]
    row = lambda i: (i, 0)
    fixed = lambda i: (0, 0)
    bf16 = jnp.bfloat16
    tiles_per_batch = seq // tm
    batch = m // seq
    transposed = lambda i: (i // tiles_per_batch, 0, i % tiles_per_batch)

    def rows(width, dtype=bf16):
        return jax.ShapeDtypeStruct((m, width), dtype), pl.BlockSpec((tm, width), row)

    def keys(width):
        return (jax.ShapeDtypeStruct((batch, width, seq), bf16),
                pl.BlockSpec((1, width, tm), transposed))

    outs = [rows(WIDTH), keys(WIDTH), rows(N_HEADS * LANES),
            rows(WIDTH), keys(WIDTH), rows(N_HEADS * LANES),
            rows(N_IDX_HEADS * IDX_DIM), rows(LANES),
            rows(LANES, jnp.float32)]
    out_shape = [o[0] for o in outs]
    out_specs = [o[1] for o in outs]
    return pl.pallas_call(
        functools.partial(_proj_kernel, tiles_per_batch=tiles_per_batch),
        grid=(m // tm,),
        in_specs=[
            pl.BlockSpec((tm, D_MODEL), row),
            pl.BlockSpec((tm, 1), row),
            pl.BlockSpec((1, D_MODEL), fixed),
            pl.BlockSpec((1, LANES), fixed),
            pl.BlockSpec((D_MODEL, n_w), fixed),
            pl.BlockSpec((1, LANES), fixed),
        ],
        out_specs=out_specs,
        out_shape=out_shape,
        scratch_shapes=[pltpu.VMEM((1, LANES), jnp.float32)],
        compiler_params=pltpu.CompilerParams(
            dimension_semantics=("arbitrary",), vmem_limit_bytes=VMEM_LIMIT),
        name="in_proj",
    )(x2, pos2, gain, invf, w_all, bf_row)


def _half_select(x, first):
    lane = lax.broadcasted_iota(jnp.int32, x.shape, x.ndim - 1)
    keep = (lane < HEAD_DIM) if first else (lane >= HEAD_DIM)
    return jnp.where(keep, x, jnp.zeros_like(x))


def _stack_heads(q):
    return jnp.concatenate([_half_select(q, True), _half_select(q, False)], axis=0)


def _attend(qs, kt_ref, v_ref, bias, i, s_ref, m_ref, acc_ref):
    t = ATT_TILE

    def phase_a(start, width, diagonal):
        keys = pl.ds(pl.multiple_of(start, width), width)
        s = jnp.dot(qs, kt_ref[0, :, keys], preferred_element_type=jnp.float32)
        s = bias(s, keys, width, diagonal)
        s_ref[:, keys] = s
        m = m_ref[...]
        for c in range(width // LANES):
            m = jnp.maximum(m, s[:, c * LANES:(c + 1) * LANES])
        m_ref[...] = m

    def phase_b(start, width, m):
        keys = pl.ds(pl.multiple_of(start, width), width)
        p = jnp.exp2(s_ref[:, keys] - m).astype(jnp.bfloat16)
        for k in range(HEADS_PER_BLOCK):
            acc_ref[k * t:(k + 1) * t, :] += jnp.dot(
                p[k * t:(k + 1) * t], v_ref[0, keys, k * LANES:(k + 1) * LANES],
                preferred_element_type=jnp.float32)

    def below_diagonal(fn):
        widest = ATTEND_WIDTHS[0]
        n_widest = (i * t) // widest

        def body(j, carry):
            fn(j * widest, widest)
            return carry

        lax.fori_loop(0, n_widest, body, 0)
        start = n_widest * widest
        rest = i * t - start
        for width in ATTEND_WIDTHS[1:]:
            part = rest & width

            @pl.when(part != 0)
            def _(start=start, width=width):
                fn(start, width)

            start = start + part

    m_ref[...] = jnp.full(m_ref.shape, NEG, jnp.float32)
    below_diagonal(lambda start, width: phase_a(start, width, False))
    phase_a(i * t, t, True)

    m = jnp.max(m_ref[...], axis=1, keepdims=True)
    acc_ref[...] = jnp.zeros(acc_ref.shape, jnp.float32)
    below_diagonal(lambda start, width: phase_b(start, width, m))
    phase_b(i * t, t, m)

    acc = acc_ref[...]
    o = acc / acc[:, HEAD_DIM:HEAD_DIM + 1]
    lane = lax.broadcasted_iota(jnp.int32, (t, LANES), 1)
    return jnp.where(lane < HEAD_DIM, o[:t], pltpu.roll(o[t:], HEAD_DIM, 1))


def _attend_scratch(t, s):
    rows = HEADS_PER_BLOCK * t
    return [pltpu.VMEM((rows, s), jnp.float32),
            pltpu.VMEM((rows, LANES), jnp.float32),
            pltpu.VMEM((rows, LANES), jnp.float32)]


def _index_mask(iq_ref, ik_ref, misc_ref, sct_ref, sc_ref, hi_ref, lo_ref, i):
    t = ATT_TILE
    top_k = float(TOPK_MAX)
    n_wide = (i * t) // WIDE_KEYS
    first_narrow = n_wide * (WIDE_KEYS // t)

    def blocks(fn, carry, last_narrow):
        def wide(j, c):
            return fn(pl.ds(pl.multiple_of(j * WIDE_KEYS, WIDE_KEYS), WIDE_KEYS),
                      WIDE_KEYS, c)

        def narrow(j, c):
            return fn(pl.ds(pl.multiple_of(j * t, t), t), t, c)

        carry = lax.fori_loop(0, n_wide, wide, carry)
        return lax.fori_loop(first_narrow, last_narrow, narrow, carry)

    def fold(x, op, rows=SUBLANES):
        return op(x.reshape(x.shape[0] // rows, rows, t), axis=0)

    iq = iq_ref[0]
    qt = []
    for h in range(N_IDX_HEADS):
        block = _half_select(iq[:, (h // 2) * LANES:(h // 2 + 1) * LANES], h % 2 == 0)
        qt.append(jnp.transpose(block.astype(jnp.float32)).astype(jnp.bfloat16))
    misc_t = jnp.transpose(misc_ref[0])
    wh = [misc_t[MISC_IW + h:MISC_IW + h + 1, :] for h in range(N_IDX_HEADS)]

    def scores(keys):
        k = ik_ref[0, keys, :]
        sc = None
        for h in range(N_IDX_HEADS):
            rel = jnp.dot(k, qt[h], preferred_element_type=jnp.float32)
            term = wh[h] * jnp.maximum(rel, 0.0)
            sc = term if sc is None else sc + term
        return sc

    def track(for_max, for_min):
        hi_ref[...] = jnp.maximum(hi_ref[...], fold(for_max, jnp.max))
        lo_ref[...] = jnp.minimum(lo_ref[...], fold(for_min, jnp.min))

    def score_block(keys, width, carry):
        sc = scores(keys)
        sct_ref[keys, :] = sc
        track(sc, sc)
        return carry

    hi_ref[...] = jnp.full(hi_ref.shape, -jnp.inf, jnp.float32)
    lo_ref[...] = jnp.full(lo_ref.shape, jnp.inf, jnp.float32)
    blocks(score_block, 0, i)
    diag = pl.ds(pl.multiple_of(i * t, t), t)
    sc = scores(diag)
    causal = (lax.broadcasted_iota(jnp.int32, (t, t), 0)
              <= lax.broadcasted_iota(jnp.int32, (t, t), 1))
    masked = jnp.where(causal, sc, -jnp.inf)
    sct_ref[diag, :] = masked
    track(masked, jnp.where(causal, sc, jnp.inf))

    def count_ge(thr):
        def block(keys, width, acc):
            hit = jnp.where(sct_ref[keys, :] >= thr, 1.0, 0.0)
            return acc + fold(hit, jnp.sum, COUNT_CHAINS * SUBLANES)

        acc = blocks(block, jnp.zeros((COUNT_CHAINS * SUBLANES, t), jnp.float32), i + 1)
        return jnp.sum(acc, axis=0, keepdims=True)

    def flag(cond):
        return jnp.where(cond, 1.0, 0.0)

    rmax = jnp.max(hi_ref[...], axis=0, keepdims=True)
    rmin = jnp.min(lo_ref[...], axis=0, keepdims=True)
    n_causal = (i * t + 1 + lax.broadcasted_iota(jnp.int32, (1, t), 1)
                ).astype(jnp.float32)
    c_pos = count_ge(jnp.full((1, t), TINY, jnp.float32))
    c_zero = count_ge(jnp.zeros((1, t), jnp.float32))
    c_max = count_ge(rmax)
    take_all = n_causal <= top_k
    tie_max = c_max >= top_k
    tie_zero = jnp.where(c_pos < top_k, flag(c_zero >= top_k), 0.0) > 0.5
    done = jnp.where(take_all, 1.0, jnp.where(tie_max, 1.0, jnp.where(
        tie_zero, 1.0, flag(c_pos == top_k))))
    thr = jnp.where(take_all, ALL_THR, jnp.where(tie_max, rmax, jnp.where(
        tie_zero, 0.0, TINY)))
    take = jnp.where(take_all, BIG, jnp.where(tie_max, top_k, jnp.where(
        tie_zero, top_k - c_pos, BIG)))
    positive = c_pos > top_k
    lo = jnp.where(positive, TINY, rmin)
    c_lo = jnp.where(positive, c_pos, n_causal)
    hi = jnp.where(positive, rmax, 0.0)
    c_hi = jnp.where(positive, c_max, c_zero)
    log_target = float(np.log(top_k + 0.5))

    def cond(state):
        it, done = state[0], state[-1]
        return (it < MAX_SEARCH) & (jnp.min(done) < 0.5)

    def step(state):
        it, lo, hi, c_lo, c_hi, thr, take, done = state
        log_lo = jnp.log(c_lo)
        frac = (log_lo - log_target) / (log_lo - jnp.log(jnp.maximum(c_hi, 0.5)))
        frac = jnp.clip(frac, MIN_FRACTION, 1.0 - MIN_FRACTION)
        frac = jnp.where(it < INTERPOLATE_STEPS, frac, 0.5)
        guess = lo + (hi - lo) * frac
        halfway = 0.5 * lo + 0.5 * hi
        mid = jnp.where(guess <= lo, halfway, jnp.where(guess >= hi, halfway, guess))
        spent = jnp.where(mid <= lo, 1.0, flag(mid >= hi))
        c = count_ge(mid)
        finish = (1.0 - done) * jnp.maximum(spent, flag(c == top_k)) > 0.5
        thr = jnp.where(finish, jnp.where(spent > 0.5, lo, mid), thr)
        take = jnp.where(finish, jnp.where(spent > 0.5, top_k - c_hi, BIG), take)
        above = c > top_k
        below = c < top_k
        lo = jnp.where(above, mid, lo)
        c_lo = jnp.where(above, c, c_lo)
        hi = jnp.where(below, mid, hi)
        c_hi = jnp.where(below, c, c_hi)
        done = jnp.where(finish, 1.0, done)
        return it + 1, lo, hi, c_lo, c_hi, thr, take, done

    _, lo, _, _, c_hi, thr, take, done = lax.while_loop(
        cond, step, (jnp.int32(0), lo, hi, c_lo, c_hi, thr, take, done))
    thr = jnp.where(done < 0.5, lo, thr)
    take = jnp.where(done < 0.5, top_k - c_hi, take)

    ranked = jnp.max(flag(take < BIG)) > 0.5

    @pl.when(jnp.logical_not(ranked))
    def _():
        def tile(j, carry):
            keys = pl.ds(pl.multiple_of(j * t, t), t)
            sc_ref[:, keys] = jnp.transpose(
                jnp.where(sct_ref[keys, :] >= thr, 0.0, NEG))
            return carry

        lax.fori_loop(0, i + 1, tile, 0)

    @pl.when(ranked)
    def _():
        earlier = (lax.broadcasted_iota(jnp.int32, (t, t), 1)
                   < lax.broadcasted_iota(jnp.int32, (t, t), 0))
        earlier = jnp.where(earlier, 1.0, 0.0).astype(jnp.bfloat16)

        def tile(j, seen):
            keys = pl.ds(pl.multiple_of(j * t, t), t)
            s = sct_ref[keys, :]
            eq = flag(s == thr)
            rank = seen + jnp.dot(earlier, eq.astype(jnp.bfloat16),
                                  preferred_element_type=jnp.float32)
            sc_ref[:, keys] = jnp.transpose(jnp.where(s > thr, 0.0, jnp.where(
                s == thr, jnp.where(rank < take, 0.0, NEG), NEG)))
            return seen + jnp.sum(eq, axis=0, keepdims=True)

        lax.fori_loop(0, i + 1, tile, jnp.zeros((1, t), jnp.float32))


def _dsa_kernel(iq_ref, ik_ref, misc_ref, q_ref, kt_ref, v_ref, o_ref, sct_ref, sc_ref,
                hi_ref, lo_ref, s_ref, m_ref, acc_ref):
    i = pl.program_id(1)

    @pl.when(pl.program_id(2) == 0)
    def _():
        _index_mask(iq_ref, ik_ref, misc_ref, sct_ref, sc_ref, hi_ref, lo_ref, i)

    def bias(s, keys, width, diagonal):
        mask = sc_ref[:, keys]
        return s + jnp.concatenate([mask, mask], axis=0)

    o_ref[0] = _attend(_stack_heads(q_ref[0]), kt_ref, v_ref, bias, i,
                       s_ref, m_ref, acc_ref)


def _dsa_attention(iq, ik2, misc, aq, ak_t, av):
    b, s, _ = aq.shape
    t = ATT_TILE
    return pl.pallas_call(
        _dsa_kernel,
        grid=(b, s // t, N_HEAD_BLOCKS),
        in_specs=[
            pl.BlockSpec((1, t, N_IDX_HEADS * IDX_DIM), lambda bb, i, hp: (bb, i, 0)),
            pl.BlockSpec((1, s, LANES), lambda bb, i, hp: (bb, 0, 0)),
            pl.BlockSpec((1, t, LANES), lambda bb, i, hp: (bb, i, 0)),
            pl.BlockSpec((1, t, LANES), lambda bb, i, hp: (bb, i, hp)),
            pl.BlockSpec((1, LANES, s), lambda bb, i, hp: (bb, hp, 0)),
            pl.BlockSpec((1, s, HEADS_PER_BLOCK * LANES), lambda bb, i, hp: (bb, 0, hp)),
        ],
        out_specs=pl.BlockSpec((1, t, LANES), lambda bb, i, hp: (bb, i, hp)),
        out_shape=jax.ShapeDtypeStruct((b, s, WIDTH), jnp.float32),
        scratch_shapes=[pltpu.VMEM((s, t), jnp.float32),
                        pltpu.VMEM((t, s), jnp.float32),
                        pltpu.VMEM((SUBLANES, t), jnp.float32),
                        pltpu.VMEM((SUBLANES, t), jnp.float32)] + _attend_scratch(t, s),
        compiler_params=pltpu.CompilerParams(
            dimension_semantics=("arbitrary", "arbitrary", "arbitrary"),
            vmem_limit_bytes=VMEM_LIMIT),
        name="dsa_attention",
    )(iq, ik2, misc, aq, ak_t, av)


def _fox_kernel(q_ref, kt_ref, v_ref, cq_ref, ck_ref, o_ref, s_ref, m_ref, acc_ref):
    t = ATT_TILE
    cq = jnp.concatenate([cq_ref[0, 0, :, 0:1], cq_ref[0, 0, :, 1:2]], axis=0)

    def bias(s, keys, width, diagonal):
        ck = ck_ref[0, 0, :, keys]
        ck = jnp.concatenate([jnp.broadcast_to(ck[0:1], (t, width)),
                              jnp.broadcast_to(ck[1:2], (t, width))], axis=0)
        s = s + (cq - ck)
        if diagonal:
            below = (lax.broadcasted_iota(jnp.int32, (t, t), 1)
                     <= lax.broadcasted_iota(jnp.int32, (t, t), 0))
            s = jnp.where(jnp.concatenate([below, below], axis=0), s, NEG)
        return s

    o_ref[0] = _attend(_stack_heads(q_ref[0]), kt_ref, v_ref, bias, pl.program_id(2),
                       s_ref, m_ref, acc_ref)


def _fox_attention(fq, fk_t, fv, cum_q, cum_k):
    b, s, _ = fq.shape
    t = ATT_TILE
    return pl.pallas_call(
        _fox_kernel,
        grid=(b, N_HEAD_BLOCKS, s // t),
        in_specs=[
            pl.BlockSpec((1, t, LANES), lambda bb, hp, i: (bb, i, hp)),
            pl.BlockSpec((1, LANES, s), lambda bb, hp, i: (bb, hp, 0)),
            pl.BlockSpec((1, s, HEADS_PER_BLOCK * LANES), lambda bb, hp, i: (bb, 0, hp)),
            pl.BlockSpec((1, 1, t, HEADS_PER_BLOCK), lambda bb, hp, i: (bb, hp, i, 0)),
            pl.BlockSpec((1, 1, HEADS_PER_BLOCK, s), lambda bb, hp, i: (bb, hp, 0, 0)),
        ],
        out_specs=pl.BlockSpec((1, t, LANES), lambda bb, hp, i: (bb, i, hp)),
        out_shape=jax.ShapeDtypeStruct((b, s, WIDTH), jnp.float32),
        scratch_shapes=_attend_scratch(t, s),
        compiler_params=pltpu.CompilerParams(
            dimension_semantics=("arbitrary", "arbitrary", "arbitrary"),
            vmem_limit_bytes=VMEM_LIMIT),
        name="fox_attention",
    )(fq, fk_t, fv, cum_q, cum_k)


def _out_kernel(x_ref, ya_ref, yb_ref, gain_ref, wg_ref, bm_ref, wa_ref, wb_ref,
                wo_ref, ngain_ref, o_ref):
    x = x_ref[...]
    h = _rms(x, gain_ref[...]).astype(jnp.bfloat16)
    g = jnp.dot(h, wg_ref[...], preferred_element_type=jnp.float32)
    a_gate = g[:, :WIDTH]
    f_gate = g[:, WIDTH:2 * WIDTH]
    m_logit = g[:, 2 * WIDTH:] + bm_ref[...]
    za = (ya_ref[...] * (a_gate * jax.nn.sigmoid(a_gate))).astype(jnp.bfloat16)
    zb = (yb_ref[...] * (f_gate * jax.nn.sigmoid(f_gate))).astype(jnp.bfloat16)
    ua = jnp.dot(za, wa_ref[...], preferred_element_type=jnp.float32)
    ub = jnp.dot(zb, wb_ref[...], preferred_element_type=jnp.float32)
    gates = jax.nn.sigmoid(m_logit)
    merged = gates[:, :D_MODEL] * ua + gates[:, D_MODEL:] * ub
    y = x + jnp.dot(merged.astype(jnp.bfloat16), wo_ref[...],
                    preferred_element_type=jnp.float32)
    o_ref[...] = _rms(y, ngain_ref[...])


def _output(x2, ya, yb, gain, w_gates, b_merge, w_a, w_b, w_o, next_gain):
    m = x2.shape[0]
    tm = PROJ_ROWS
    row = lambda i: (i, 0)
    fixed = lambda i: (0, 0)
    return pl.pallas_call(
        _out_kernel,
        grid=(m // tm,),
        in_specs=[
            pl.BlockSpec((tm, D_MODEL), row),
            pl.BlockSpec((tm, WIDTH), row),
            pl.BlockSpec((tm, WIDTH), row),
            pl.BlockSpec((1, D_MODEL), fixed),
            pl.BlockSpec(w_gates.shape, fixed),
            pl.BlockSpec((1, 2 * D_MODEL), fixed),
            pl.BlockSpec(w_a.shape, fixed),
            pl.BlockSpec(w_b.shape, fixed),
            pl.BlockSpec(w_o.shape, fixed),
            pl.BlockSpec((1, D_MODEL), fixed),
        ],
        out_specs=pl.BlockSpec((tm, D_MODEL), row),
        out_shape=jax.ShapeDtypeStruct((m, D_MODEL), jnp.float32),
        compiler_params=pltpu.CompilerParams(
            dimension_semantics=("arbitrary",), vmem_limit_bytes=VMEM_LIMIT),
        name="out_proj",
    )(x2, ya, yb, gain, w_gates, b_merge, w_a, w_b, w_o, next_gain)


def _split_w_in(w):
    sizes = (WIDTH, WIDTH, WIDTH, WIDTH, N_IDX_HEADS * IDX_DIM, IDX_DIM, N_IDX_HEADS,
             WIDTH, WIDTH, WIDTH, WIDTH, N_HEADS, 2 * D_MODEL)
    points = np.cumsum(sizes)[:-1]
    return jnp.split(w, points, axis=-1)


def kernel(x, positions, norm_gain, w_in, b_forget, b_merge, w_branch_dsa,
           w_branch_fox, w_out, final_gain):
    b, s, d = x.shape
    depth = w_in.shape[0]
    bf16 = jnp.bfloat16

    half = jnp.arange(ROPE_HALF, dtype=jnp.float32)
    inv_freq = ROPE_THETA ** (-half * 2.0 / ROPE_DIM)
    dim = np.arange(LANES) % HEAD_DIM
    invf = jnp.where(dim < ROPE_DIM, inv_freq[dim % ROPE_HALF], 0.0)[None, :]
    pos2 = positions.reshape(b * s, 1)

    x2 = x.reshape(b * s, d)
    normed = None
    for l in range(depth):
        (w_aq, w_ak, w_av, w_ag, w_iq, w_ik, w_iw,
         w_fq, w_fk, w_fv, w_fg, w_fl, w_m) = _split_w_in(w_in[l])
        pad = jnp.zeros((d, LANES - N_IDX_HEADS - N_HEADS), w_in.dtype)
        w_all = jnp.concatenate(
            [w_aq, w_ak, w_av, w_fq, w_fk, w_fv, w_iq, w_ik, w_ik, w_iw, w_fl, pad],
            axis=1).astype(bf16)
        w_gates = jnp.concatenate([w_ag, w_fg, w_m], axis=1).astype(bf16)
        bf_row = jnp.zeros((1, LANES), jnp.float32).at[
            0, MISC_CUM:MISC_CUM + N_HEADS].set(b_forget[l])
        gain = norm_gain[l][None, :]
        next_gain = (norm_gain[l + 1] if l + 1 < depth else final_gain)[None, :]

        aq, ak_t, av, fq, fk_t, fv, iq, ik2, misc = _project(
            x2, pos2, gain, invf, w_all, bf_row, s)
        shape3 = lambda a: a.reshape(b, s, a.shape[-1])
        ya = _dsa_attention(shape3(iq), shape3(ik2), shape3(misc), shape3(aq), ak_t,
                            shape3(av))
        cum = misc[:, MISC_CUM:MISC_CUM + N_HEADS].reshape(
            b, s, N_HEAD_BLOCKS, HEADS_PER_BLOCK)
        cum_q = cum.transpose(0, 2, 1, 3)
        cum_k = cum.transpose(0, 2, 3, 1)
        yb = _fox_attention(shape3(fq), fk_t, shape3(fv), cum_q, cum_k)
        normed = _output(x2, ya.reshape(b * s, WIDTH), yb.reshape(b * s, WIDTH),
                         gain, w_gates, b_merge[l][None, :],
                         w_branch_dsa[l].astype(bf16), w_branch_fox[l].astype(bf16),
                         w_out[l].astype(bf16), next_gain)
        assert depth == 1, "stacked layers need the un-normalised residual as well"
    return normed.reshape(b, s, d)
```

```python
import functools

import jax
import jax.numpy as jnp
import numpy as np
from jax import lax
from jax.experimental import pallas as pl
from jax.experimental.pallas import tpu as pltpu

D_MODEL = 1024
HEAD_DIM = 64
N_HEADS = 8
WIDTH = N_HEADS * HEAD_DIM
N_IDX_HEADS = 4
IDX_DIM = 64
TOPK_MAX = 256
ROPE_THETA = 500000.0
ROPE_DIM = HEAD_DIM // 4
ROPE_HALF = ROPE_DIM // 2
RMS_EPS = 1e-6
NEG = -1e30
LOG2_E = float(np.log2(np.e))

LANES = 128
SUBLANES = 8
HEADS_PER_BLOCK = LANES // HEAD_DIM
N_HEAD_BLOCKS = N_HEADS // HEADS_PER_BLOCK
VMEM_LIMIT = 56 * 1024 * 1024

PROJ_ROWS = 512
ATT_TILE = 256
WIDE_KEYS = 1024
DSA_WIDTHS = (2048, 1024, 512, 256)
FOX_TILE = 512
FOX_WIDTHS = (2048, 1024, 512)
COUNT_CHAINS = 4

TINY = float(np.finfo(np.float32).tiny)
ALL_THR = -3.0e38
BIG = 1.0e9
MAX_SEARCH = 400
INTERPOLATE_STEPS = 12
MIN_FRACTION = 1.0 / 32.0

MISC_IW = 0
MISC_CUM = N_IDX_HEADS


def _rms(x, gain):
    ms = jnp.mean(x * x, axis=-1, keepdims=True)
    return x * lax.rsqrt(ms + RMS_EPS) * gain


def _proj_kernel(x_ref, pos_ref, gain_ref, invf_ref, w_ref, bf_ref,
                 aq_ref, ak_ref, av_ref, fq_ref, fk_ref, fv_ref, iq_ref, ik_ref,
                 misc_ref, carry_ref, *, tiles_per_batch):
    tm = x_ref.shape[0]
    h = _rms(x_ref[...], gain_ref[...]).astype(jnp.bfloat16)

    ang = pos_ref[...].astype(jnp.float32) * invf_ref[...]
    cos = jnp.cos(ang)
    sin = jnp.sin(ang)
    d = lax.broadcasted_iota(jnp.int32, (tm, LANES), 1) % HEAD_DIM
    sin_lo = jnp.where(d < ROPE_HALF, -sin, 0.0)
    sin_hi = jnp.where(d >= ROPE_HALF, sin, 0.0)

    def rope(t):
        up = pltpu.roll(t, LANES - ROPE_HALF, 1)
        dn = pltpu.roll(t, ROPE_HALF, 1)
        return t * cos + up * sin_lo + dn * sin_hi

    def blocks_of(col0, width, use_rope, scale):
        r = jnp.dot(h, w_ref[:, col0:col0 + width],
                    preferred_element_type=jnp.float32)
        for c in range(width // LANES):
            t = r[:, c * LANES:(c + 1) * LANES]
            if use_rope:
                t = rope(t)
            if scale != 1.0:
                t = t * scale
            yield c, t

    def emit(out_ref, col0, width, use_rope, scale):
        for c, t in blocks_of(col0, width, use_rope, scale):
            out_ref[:, c * LANES:(c + 1) * LANES] = t.astype(out_ref.dtype)

    def emit_transposed(out_ref, col0, width, use_rope):
        for c, t in blocks_of(col0, width, use_rope, 1.0):
            out_ref[0, c * LANES:(c + 1) * LANES, :] = jnp.transpose(t).astype(
                out_ref.dtype)

    def emit_values(out_ref, col0):
        lane = lax.broadcasted_iota(jnp.int32, (tm, LANES), 1)
        tail = jnp.where(lane == HEAD_DIM, 1.0, 0.0)
        for c, t in blocks_of(col0, WIDTH, False, 1.0):
            pair = (t, pltpu.roll(t, HEAD_DIM, 1))
            for k in range(HEADS_PER_BLOCK):
                head = c * HEADS_PER_BLOCK + k
                out_ref[:, head * LANES:(head + 1) * LANES] = jnp.where(
                    lane < HEAD_DIM, pair[k], tail).astype(out_ref.dtype)

    qk_scale = HEAD_DIM ** -0.5 * LOG2_E
    emit(aq_ref, 0 * WIDTH, WIDTH, True, qk_scale)
    emit_transposed(ak_ref, 1 * WIDTH, WIDTH, True)
    emit_values(av_ref, 2 * WIDTH)
    emit(fq_ref, 3 * WIDTH, WIDTH, False, qk_scale)
    emit_transposed(fk_ref, 4 * WIDTH, WIDTH, False)
    emit_values(fv_ref, 5 * WIDTH)
    col = 6 * WIDTH
    emit(iq_ref, col, N_IDX_HEADS * IDX_DIM, True, IDX_DIM ** -0.5)
    col += N_IDX_HEADS * IDX_DIM
    emit(ik_ref, col, LANES, True, 1.0)
    col += LANES

    r = jnp.dot(h, w_ref[:, col:col + LANES], preferred_element_type=jnp.float32)
    z = r + bf_ref[...]
    logf = jnp.minimum(z, 0.0) - jnp.log1p(jnp.exp(-jnp.abs(z)))

    @pl.when(pl.program_id(0) % tiles_per_batch == 0)
    def _():
        carry_ref[...] = jnp.zeros_like(carry_ref)

    rows = lax.broadcasted_iota(jnp.int32, (tm, tm), 0)
    cols = lax.broadcasted_iota(jnp.int32, (tm, tm), 1)
    tri = jnp.where(cols <= rows, 1.0, 0.0).astype(jnp.bfloat16)
    hi = logf.astype(jnp.bfloat16)
    rem = logf - hi.astype(jnp.float32)
    mid = rem.astype(jnp.bfloat16)
    lo = (rem - mid.astype(jnp.float32)).astype(jnp.bfloat16)
    cum = (jnp.dot(tri, hi, preferred_element_type=jnp.float32)
           + jnp.dot(tri, mid, preferred_element_type=jnp.float32)
           + jnp.dot(tri, lo, preferred_element_type=jnp.float32)
           + carry_ref[...])
    carry_ref[...] = cum[tm - 1:tm, :]

    lane = lax.broadcasted_iota(jnp.int32, (tm, LANES), 1)
    misc = jnp.where(lane < MISC_CUM, r * (N_IDX_HEADS ** -0.5),
                     jnp.where(lane < MISC_CUM + N_HEADS, cum * LOG2_E, 0.0))
    misc_ref[...] = misc


def _project(x2, pos2, gain, invf, w_all, bf_row, seq):
    m = x2.shape[0]
    tm = PROJ_ROWS
    n_w = w_all.shape[1]
    row = lambda i: (i, 0)
    fixed = lambda i: (0, 0)
    bf16 = jnp.bfloat16
    tiles_per_batch = seq // tm
    batch = m // seq
    transposed = lambda i: (i // tiles_per_batch, 0, i % tiles_per_batch)

    def rows(width, dtype=bf16):
        return jax.ShapeDtypeStruct((m, width), dtype), pl.BlockSpec((tm, width), row)

    def keys(width):
        return (jax.ShapeDtypeStruct((batch, width, seq), bf16),
                pl.BlockSpec((1, width, tm), transposed))

    outs = [rows(WIDTH), keys(WIDTH), rows(N_HEADS * LANES),
            rows(WIDTH), keys(WIDTH), rows(N_HEADS * LANES),
            rows(N_IDX_HEADS * IDX_DIM), rows(LANES),
            rows(LANES, jnp.float32)]
    out_shape = [o[0] for o in outs]
    out_specs = [o[1] for o in outs]
    return pl.pallas_call(
        functools.partial(_proj_kernel, tiles_per_batch=tiles_per_batch),
        grid=(m // tm,),
        in_specs=[
            pl.BlockSpec((tm, D_MODEL), row),
            pl.BlockSpec((tm, 1), row),
            pl.BlockSpec((1, D_MODEL), fixed),
            pl.BlockSpec((1, LANES), fixed),
            pl.BlockSpec((D_MODEL, n_w), fixed),
            pl.BlockSpec((1, LANES), fixed),
        ],
        out_specs=out_specs,
        out_shape=out_shape,
        scratch_shapes=[pltpu.VMEM((1, LANES), jnp.float32)],
        compiler_params=pltpu.CompilerParams(
            dimension_semantics=("arbitrary",), vmem_limit_bytes=VMEM_LIMIT),
        name="in_proj",
    )(x2, pos2, gain, invf, w_all, bf_row)


def _half_select(x, first):
    lane = lax.broadcasted_iota(jnp.int32, x.shape, x.ndim - 1)
    keep = (lane < HEAD_DIM) if first else (lane >= HEAD_DIM)
    return jnp.where(keep, x, jnp.zeros_like(x))


def _stack_heads(q):
    return jnp.concatenate([_half_select(q, True), _half_select(q, False)], axis=0)


def _attend(qs, kt_ref, v_ref, bias, i, widths, s_ref, m_ref, acc_ref):
    t = qs.shape[0] // HEADS_PER_BLOCK
    assert widths[-1] == t and all(a == 2 * b for a, b in zip(widths, widths[1:]))

    def phase_a(start, width, diagonal):
        keys = pl.ds(pl.multiple_of(start, width), width)
        s = jnp.dot(qs, kt_ref[0, :, keys], preferred_element_type=jnp.float32)
        s = bias(s, keys, width, diagonal)
        s_ref[:, keys] = s
        m = m_ref[...]
        for c in range(width // LANES):
            m = jnp.maximum(m, s[:, c * LANES:(c + 1) * LANES])
        m_ref[...] = m

    def phase_b(start, width, m):
        keys = pl.ds(pl.multiple_of(start, width), width)
        p = jnp.exp2(s_ref[:, keys] - m).astype(jnp.bfloat16)
        for k in range(HEADS_PER_BLOCK):
            acc_ref[k * t:(k + 1) * t, :] += jnp.dot(
                p[k * t:(k + 1) * t], v_ref[0, keys, k * LANES:(k + 1) * LANES],
                preferred_element_type=jnp.float32)

    def below_diagonal(fn):
        widest = widths[0]
        n_widest = (i * t) // widest

        def body(j, carry):
            fn(j * widest, widest)
            return carry

        lax.fori_loop(0, n_widest, body, 0)
        start = n_widest * widest
        rest = i * t - start
        for width in widths[1:]:
            part = rest & width

            @pl.when(part != 0)
            def _(start=start, width=width):
                fn(start, width)

            start = start + part

    m_ref[...] = jnp.full(m_ref.shape, NEG, jnp.float32)
    below_diagonal(lambda start, width: phase_a(start, width, False))
    phase_a(i * t, t, True)

    m = jnp.max(m_ref[...], axis=1, keepdims=True)
    acc_ref[...] = jnp.zeros(acc_ref.shape, jnp.float32)
    below_diagonal(lambda start, width: phase_b(start, width, m))
    phase_b(i * t, t, m)

    acc = acc_ref[...]
    o = acc / acc[:, HEAD_DIM:HEAD_DIM + 1]
    lane = lax.broadcasted_iota(jnp.int32, (t, LANES), 1)
    return jnp.where(lane < HEAD_DIM, o[:t], pltpu.roll(o[t:], HEAD_DIM, 1))


def _attend_scratch(t, s):
    rows = HEADS_PER_BLOCK * t
    return [pltpu.VMEM((rows, s), jnp.float32),
            pltpu.VMEM((rows, LANES), jnp.float32),
            pltpu.VMEM((rows, LANES), jnp.float32)]


def _index_mask(iq_ref, ik_ref, misc_ref, sct_ref, sc_ref, hi_ref, lo_ref, i):
    t = ATT_TILE
    top_k = float(TOPK_MAX)
    n_wide = (i * t) // WIDE_KEYS
    first_narrow = n_wide * (WIDE_KEYS // t)

    def blocks(fn, carry, last_narrow):
        def wide(j, c):
            return fn(pl.ds(pl.multiple_of(j * WIDE_KEYS, WIDE_KEYS), WIDE_KEYS),
                      WIDE_KEYS, c)

        def narrow(j, c):
            return fn(pl.ds(pl.multiple_of(j * t, t), t), t, c)

        carry = lax.fori_loop(0, n_wide, wide, carry)
        return lax.fori_loop(first_narrow, last_narrow, narrow, carry)

    def fold(x, op, rows=SUBLANES):
        return op(x.reshape(x.shape[0] // rows, rows, t), axis=0)

    iq = iq_ref[0]
    qt = []
    for h in range(N_IDX_HEADS):
        block = _half_select(iq[:, (h // 2) * LANES:(h // 2 + 1) * LANES], h % 2 == 0)
        qt.append(jnp.transpose(block.astype(jnp.float32)).astype(jnp.bfloat16))
    misc_t = jnp.transpose(misc_ref[0])
    wh = [misc_t[MISC_IW + h:MISC_IW + h + 1, :] for h in range(N_IDX_HEADS)]

    def scores(keys):
        k = ik_ref[0, keys, :]
        sc = None
        for h in range(N_IDX_HEADS):
            rel = jnp.dot(k, qt[h], preferred_element_type=jnp.float32)
            term = wh[h] * jnp.maximum(rel, 0.0)
            sc = term if sc is None else sc + term
        return sc

    def track(for_max, for_min):
        hi_ref[...] = jnp.maximum(hi_ref[...], fold(for_max, jnp.max))
        lo_ref[...] = jnp.minimum(lo_ref[...], fold(for_min, jnp.min))

    def score_block(keys, width, carry):
        sc = scores(keys)
        sct_ref[keys, :] = sc
        track(sc, sc)
        return carry

    hi_ref[...] = jnp.full(hi_ref.shape, -jnp.inf, jnp.float32)
    lo_ref[...] = jnp.full(lo_ref.shape, jnp.inf, jnp.float32)
    blocks(score_block, 0, i)
    diag = pl.ds(pl.multiple_of(i * t, t), t)
    sc = scores(diag)
    causal = (lax.broadcasted_iota(jnp.int32, (t, t), 0)
              <= lax.broadcasted_iota(jnp.int32, (t, t), 1))
    masked = jnp.where(causal, sc, -jnp.inf)
    sct_ref[diag, :] = masked
    track(masked, jnp.where(causal, sc, jnp.inf))

    def count_ge(thr):
        def block(keys, width, acc):
            hit = jnp.where(sct_ref[keys, :] >= thr, 1.0, 0.0)
            return acc + fold(hit, jnp.sum, COUNT_CHAINS * SUBLANES)

        acc = blocks(block, jnp.zeros((COUNT_CHAINS * SUBLANES, t), jnp.float32), i + 1)
        return jnp.sum(acc, axis=0, keepdims=True)

    def flag(cond):
        return jnp.where(cond, 1.0, 0.0)

    rmax = jnp.max(hi_ref[...], axis=0, keepdims=True)
    rmin = jnp.min(lo_ref[...], axis=0, keepdims=True)
    n_causal = (i * t + 1 + lax.broadcasted_iota(jnp.int32, (1, t), 1)
                ).astype(jnp.float32)
    c_pos = count_ge(jnp.full((1, t), TINY, jnp.float32))
    c_zero = count_ge(jnp.zeros((1, t), jnp.float32))
    c_max = count_ge(rmax)
    take_all = n_causal <= top_k
    tie_max = c_max >= top_k
    tie_zero = jnp.where(c_pos < top_k, flag(c_zero >= top_k), 0.0) > 0.5
    done = jnp.where(take_all, 1.0, jnp.where(tie_max, 1.0, jnp.where(
        tie_zero, 1.0, flag(c_pos == top_k))))
    thr = jnp.where(take_all, ALL_THR, jnp.where(tie_max, rmax, jnp.where(
        tie_zero, 0.0, TINY)))
    take = jnp.where(take_all, BIG, jnp.where(tie_max, top_k, jnp.where(
        tie_zero, top_k - c_pos, BIG)))
    positive = c_pos > top_k
    lo = jnp.where(positive, TINY, rmin)
    c_lo = jnp.where(positive, c_pos, n_causal)
    hi = jnp.where(positive, rmax, 0.0)
    c_hi = jnp.where(positive, c_max, c_zero)
    log_target = float(np.log(top_k + 0.5))

    def cond(state):
        it, done = state[0], state[-1]
        return (it < MAX_SEARCH) & (jnp.min(done) < 0.5)

    def step(state):
        it, lo, hi, c_lo, c_hi, thr, take, done = state
        log_lo = jnp.log(c_lo)
        frac = (log_lo - log_target) / (log_lo - jnp.log(jnp.maximum(c_hi, 0.5)))
        frac = jnp.clip(frac, MIN_FRACTION, 1.0 - MIN_FRACTION)
        frac = jnp.where(it < INTERPOLATE_STEPS, frac, 0.5)
        guess = lo + (hi - lo) * frac
        halfway = 0.5 * lo + 0.5 * hi
        mid = jnp.where(guess <= lo, halfway, jnp.where(guess >= hi, halfway, guess))
        spent = jnp.where(mid <= lo, 1.0, flag(mid >= hi))
        c = count_ge(mid)
        finish = (1.0 - done) * jnp.maximum(spent, flag(c == top_k)) > 0.5
        thr = jnp.where(finish, jnp.where(spent > 0.5, lo, mid), thr)
        take = jnp.where(finish, jnp.where(spent > 0.5, top_k - c_hi, BIG), take)
        above = c > top_k
        below = c < top_k
        lo = jnp.where(above, mid, lo)
        c_lo = jnp.where(above, c, c_lo)
        hi = jnp.where(below, mid, hi)
        c_hi = jnp.where(below, c, c_hi)
        done = jnp.where(finish, 1.0, done)
        return it + 1, lo, hi, c_lo, c_hi, thr, take, done

    _, lo, _, _, c_hi, thr, take, done = lax.while_loop(
        cond, step, (jnp.int32(0), lo, hi, c_lo, c_hi, thr, take, done))
    thr = jnp.where(done < 0.5, lo, thr)
    take = jnp.where(done < 0.5, top_k - c_hi, take)

    ranked = jnp.max(flag(take < BIG)) > 0.5

    @pl.when(jnp.logical_not(ranked))
    def _():
        def tile(j, carry):
            keys = pl.ds(pl.multiple_of(j * t, t), t)
            sc_ref[:, keys] = jnp.transpose(
                jnp.where(sct_ref[keys, :] >= thr, 0.0, NEG))
            return carry

        lax.fori_loop(0, i + 1, tile, 0)

    @pl.when(ranked)
    def _():
        earlier = (lax.broadcasted_iota(jnp.int32, (t, t), 1)
                   < lax.broadcasted_iota(jnp.int32, (t, t), 0))
        earlier = jnp.where(earlier, 1.0, 0.0).astype(jnp.bfloat16)

        def tile(j, seen):
            keys = pl.ds(pl.multiple_of(j * t, t), t)
            s = sct_ref[keys, :]
            eq = flag(s == thr)
            rank = seen + jnp.dot(earlier, eq.astype(jnp.bfloat16),
                                  preferred_element_type=jnp.float32)
            sc_ref[:, keys] = jnp.transpose(jnp.where(s > thr, 0.0, jnp.where(
                s == thr, jnp.where(rank < take, 0.0, NEG), NEG)))
            return seen + jnp.sum(eq, axis=0, keepdims=True)

        lax.fori_loop(0, i + 1, tile, jnp.zeros((1, t), jnp.float32))


def _dsa_kernel(iq_ref, ik_ref, misc_ref, q_ref, kt_ref, v_ref, o_ref, sct_ref, sc_ref,
                hi_ref, lo_ref, s_ref, m_ref, acc_ref):
    i = pl.program_id(1)

    @pl.when(pl.program_id(2) == 0)
    def _():
        _index_mask(iq_ref, ik_ref, misc_ref, sct_ref, sc_ref, hi_ref, lo_ref, i)

    def bias(s, keys, width, diagonal):
        mask = sc_ref[:, keys]
        return s + jnp.concatenate([mask, mask], axis=0)

    o_ref[0] = _attend(_stack_heads(q_ref[0]), kt_ref, v_ref, bias, i, DSA_WIDTHS,
                       s_ref, m_ref, acc_ref)


def _dsa_attention(iq, ik2, misc, aq, ak_t, av):
    b, s, _ = aq.shape
    t = ATT_TILE
    return pl.pallas_call(
        _dsa_kernel,
        grid=(b, s // t, N_HEAD_BLOCKS),
        in_specs=[
            pl.BlockSpec((1, t, N_IDX_HEADS * IDX_DIM), lambda bb, i, hp: (bb, i, 0)),
            pl.BlockSpec((1, s, LANES), lambda bb, i, hp: (bb, 0, 0)),
            pl.BlockSpec((1, t, LANES), lambda bb, i, hp: (bb, i, 0)),
            pl.BlockSpec((1, t, LANES), lambda bb, i, hp: (bb, i, hp)),
            pl.BlockSpec((1, LANES, s), lambda bb, i, hp: (bb, hp, 0)),
            pl.BlockSpec((1, s, HEADS_PER_BLOCK * LANES), lambda bb, i, hp: (bb, 0, hp)),
        ],
        out_specs=pl.BlockSpec((1, t, LANES), lambda bb, i, hp: (bb, i, hp)),
        out_shape=jax.ShapeDtypeStruct((b, s, WIDTH), jnp.float32),
        scratch_shapes=[pltpu.VMEM((s, t), jnp.float32),
                        pltpu.VMEM((t, s), jnp.float32),
                        pltpu.VMEM((SUBLANES, t), jnp.float32),
                        pltpu.VMEM((SUBLANES, t), jnp.float32)] + _attend_scratch(t, s),
        compiler_params=pltpu.CompilerParams(
            dimension_semantics=("arbitrary", "arbitrary", "arbitrary"),
            vmem_limit_bytes=VMEM_LIMIT),
        name="dsa_attention",
    )(iq, ik2, misc, aq, ak_t, av)


def _fox_kernel(q_ref, kt_ref, v_ref, cq_ref, ck_ref, o_ref, s_ref, m_ref, acc_ref):
    t = q_ref.shape[1]
    cq = jnp.concatenate([cq_ref[0, 0, :, 0:1], cq_ref[0, 0, :, 1:2]], axis=0)

    def bias(s, keys, width, diagonal):
        ck = ck_ref[0, 0, :, keys]
        ck = jnp.concatenate([jnp.broadcast_to(ck[0:1], (t, width)),
                              jnp.broadcast_to(ck[1:2], (t, width))], axis=0)
        s = s + (cq - ck)
        if diagonal:
            below = (lax.broadcasted_iota(jnp.int32, (t, t), 1)
                     <= lax.broadcasted_iota(jnp.int32, (t, t), 0))
            s = jnp.where(jnp.concatenate([below, below], axis=0), s, NEG)
        return s

    o_ref[0] = _attend(_stack_heads(q_ref[0]), kt_ref, v_ref, bias, pl.program_id(2),
                       FOX_WIDTHS, s_ref, m_ref, acc_ref)


def _fox_attention(fq, fk_t, fv, cum_q, cum_k):
    b, s, _ = fq.shape
    t = FOX_TILE
    return pl.pallas_call(
        _fox_kernel,
        grid=(b, N_HEAD_BLOCKS, s // t),
        in_specs=[
            pl.BlockSpec((1, t, LANES), lambda bb, hp, i: (bb, i, hp)),
            pl.BlockSpec((1, LANES, s), lambda bb, hp, i: (bb, hp, 0)),
            pl.BlockSpec((1, s, HEADS_PER_BLOCK * LANES), lambda bb, hp, i: (bb, 0, hp)),
            pl.BlockSpec((1, 1, t, HEADS_PER_BLOCK), lambda bb, hp, i: (bb, hp, i, 0)),
            pl.BlockSpec((1, 1, HEADS_PER_BLOCK, s), lambda bb, hp, i: (bb, hp, 0, 0)),
        ],
        out_specs=pl.BlockSpec((1, t, LANES), lambda bb, hp, i: (bb, i, hp)),
        out_shape=jax.ShapeDtypeStruct((b, s, WIDTH), jnp.float32),
        scratch_shapes=_attend_scratch(t, s),
        compiler_params=pltpu.CompilerParams(
            dimension_semantics=("arbitrary", "arbitrary", "arbitrary"),
            vmem_limit_bytes=VMEM_LIMIT),
        name="fox_attention",
    )(fq, fk_t, fv, cum_q, cum_k)


def _out_kernel(x_ref, ya_ref, yb_ref, gain_ref, wg_ref, bm_ref, wa_ref, wb_ref,
                wo_ref, ngain_ref, o_ref):
    x = x_ref[...]
    h = _rms(x, gain_ref[...]).astype(jnp.bfloat16)
    g = jnp.dot(h, wg_ref[...], preferred_element_type=jnp.float32)
    a_gate = g[:, :WIDTH]
    f_gate = g[:, WIDTH:2 * WIDTH]
    m_logit = g[:, 2 * WIDTH:] + bm_ref[...]
    za = (ya_ref[...] * (a_gate * jax.nn.sigmoid(a_gate))).astype(jnp.bfloat16)
    zb = (yb_ref[...] * (f_gate * jax.nn.sigmoid(f_gate))).astype(jnp.bfloat16)
    ua = jnp.dot(za, wa_ref[...], preferred_element_type=jnp.float32)
    ub = jnp.dot(zb, wb_ref[...], preferred_element_type=jnp.float32)
    gates = jax.nn.sigmoid(m_logit)
    merged = gates[:, :D_MODEL] * ua + gates[:, D_MODEL:] * ub
    y = x + jnp.dot(merged.astype(jnp.bfloat16), wo_ref[...],
                    preferred_element_type=jnp.float32)
    o_ref[...] = _rms(y, ngain_ref[...])


def _output(x2, ya, yb, gain, w_gates, b_merge, w_a, w_b, w_o, next_gain):
    m = x2.shape[0]
    tm = PROJ_ROWS
    row = lambda i: (i, 0)
    fixed = lambda i: (0, 0)
    return pl.pallas_call(
        _out_kernel,
        grid=(m // tm,),
        in_specs=[
            pl.BlockSpec((tm, D_MODEL), row),
            pl.BlockSpec((tm, WIDTH), row),
            pl.BlockSpec((tm, WIDTH), row),
            pl.BlockSpec((1, D_MODEL), fixed),
            pl.BlockSpec(w_gates.shape, fixed),
            pl.BlockSpec((1, 2 * D_MODEL), fixed),
            pl.BlockSpec(w_a.shape, fixed),
            pl.BlockSpec(w_b.shape, fixed),
            pl.BlockSpec(w_o.shape, fixed),
            pl.BlockSpec((1, D_MODEL), fixed),
        ],
        out_specs=pl.BlockSpec((tm, D_MODEL), row),
        out_shape=jax.ShapeDtypeStruct((m, D_MODEL), jnp.float32),
        compiler_params=pltpu.CompilerParams(
            dimension_semantics=("arbitrary",), vmem_limit_bytes=VMEM_LIMIT),
        name="out_proj",
    )(x2, ya, yb, gain, w_gates, b_merge, w_a, w_b, w_o, next_gain)


def _split_w_in(w):
    sizes = (WIDTH, WIDTH, WIDTH, WIDTH, N_IDX_HEADS * IDX_DIM, IDX_DIM, N_IDX_HEADS,
             WIDTH, WIDTH, WIDTH, WIDTH, N_HEADS, 2 * D_MODEL)
    points = np.cumsum(sizes)[:-1]
    return jnp.split(w, points, axis=-1)


def kernel(x, positions, norm_gain, w_in, b_forget, b_merge, w_branch_dsa,
           w_branch_fox, w_out, final_gain):
    b, s, d = x.shape
    depth = w_in.shape[0]
    bf16 = jnp.bfloat16

    half = jnp.arange(ROPE_HALF, dtype=jnp.float32)
    inv_freq = ROPE_THETA ** (-half * 2.0 / ROPE_DIM)
    dim = np.arange(LANES) % HEAD_DIM
    invf = jnp.where(dim < ROPE_DIM, inv_freq[dim % ROPE_HALF], 0.0)[None, :]
    pos2 = positions.reshape(b * s, 1)

    x2 = x.reshape(b * s, d)
    normed = None
    for l in range(depth):
        (w_aq, w_ak, w_av, w_ag, w_iq, w_ik, w_iw,
         w_fq, w_fk, w_fv, w_fg, w_fl, w_m) = _split_w_in(w_in[l])
        pad = jnp.zeros((d, LANES - N_IDX_HEADS - N_HEADS), w_in.dtype)
        w_all = jnp.concatenate(
            [w_aq, w_ak, w_av, w_fq, w_fk, w_fv, w_iq, w_ik, w_ik, w_iw, w_fl, pad],
            axis=1).astype(bf16)
        w_gates = jnp.concatenate([w_ag, w_fg, w_m], axis=1).astype(bf16)
        bf_row = jnp.zeros((1, LANES), jnp.float32).at[
            0, MISC_CUM:MISC_CUM + N_HEADS].set(b_forget[l])
        gain = norm_gain[l][None, :]
        next_gain = (norm_gain[l + 1] if l + 1 < depth else final_gain)[None, :]

        aq, ak_t, av, fq, fk_t, fv, iq, ik2, misc = _project(
            x2, pos2, gain, invf, w_all, bf_row, s)
        shape3 = lambda a: a.reshape(b, s, a.shape[-1])
        ya = _dsa_attention(shape3(iq), shape3(ik2), shape3(misc), shape3(aq), ak_t,
                            shape3(av))
        cum = misc[:, MISC_CUM:MISC_CUM + N_HEADS].reshape(
            b, s, N_HEAD_BLOCKS, HEADS_PER_BLOCK)
        cum_q = cum.transpose(0, 2, 1, 3)
        cum_k = cum.transpose(0, 2, 3, 1)
        yb = _fox_attention(shape3(fq), fk_t, shape3(fv), cum_q, cum_k)
        normed = _output(x2, ya.reshape(b * s, WIDTH), yb.reshape(b * s, WIDTH),
                         gain, w_gates, b_merge[l][None, :],
                         w_branch_dsa[l].astype(bf16), w_branch_fox[l].astype(bf16),
                         w_out[l].astype(bf16), next_gain)
        assert depth == 1, "stacked layers need the un-normalised residual as well"
    return normed.reshape(b, s, d)
```

```python
import functools

import jax
import jax.numpy as jnp
import numpy as np
from jax import lax
from jax.experimental import pallas as pl
from jax.experimental.pallas import tpu as pltpu

D_MODEL = 1024
HEAD_DIM = 64
N_HEADS = 8
WIDTH = N_HEADS * HEAD_DIM
N_IDX_HEADS = 4
IDX_DIM = 64
TOPK_MAX = 256
ROPE_THETA = 500000.0
ROPE_DIM = HEAD_DIM // 4
ROPE_HALF = ROPE_DIM // 2
RMS_EPS = 1e-6
NEG = -1e30
LOG2_E = float(np.log2(np.e))

LANES = 128
SUBLANES = 8
HEADS_PER_BLOCK = LANES // HEAD_DIM
N_HEAD_BLOCKS = N_HEADS // HEADS_PER_BLOCK
VMEM_LIMIT = 56 * 1024 * 1024

PROJ_ROWS = 512
ATT_TILE = 256
INDEX_WIDTHS = (2048, 1024, 512, 256)
DSA_WIDTHS = (2048, 1024, 512, 256)
FOX_TILE = 512
FOX_WIDTHS = (2048, 1024, 512)
COUNT_CHAINS = 4

TINY = float(np.finfo(np.float32).tiny)
ALL_THR = -3.0e38
BIG = 1.0e9
MAX_SEARCH = 400
INTERPOLATE_STEPS = 32
MIN_FRACTION = 1.0 / 32.0

MISC_IW = 0
MISC_CUM = N_IDX_HEADS


def _rms(x, gain):
    ms = jnp.mean(x * x, axis=-1, keepdims=True)
    return x * lax.rsqrt(ms + RMS_EPS) * gain


def _proj_kernel(x_ref, pos_ref, gain_ref, invf_ref, w_ref, bf_ref,
                 aq_ref, ak_ref, av_ref, fq_ref, fk_ref, fv_ref, iq_ref, ik_ref,
                 misc_ref, carry_ref, *, tiles_per_batch):
    tm = x_ref.shape[0]
    h = _rms(x_ref[...], gain_ref[...]).astype(jnp.bfloat16)

    ang = pos_ref[...].astype(jnp.float32) * invf_ref[...]
    cos = jnp.cos(ang)
    sin = jnp.sin(ang)
    d = lax.broadcasted_iota(jnp.int32, (tm, LANES), 1) % HEAD_DIM
    sin_lo = jnp.where(d < ROPE_HALF, -sin, 0.0)
    sin_hi = jnp.where(d >= ROPE_HALF, sin, 0.0)

    def rope(t):
        up = pltpu.roll(t, LANES - ROPE_HALF, 1)
        dn = pltpu.roll(t, ROPE_HALF, 1)
        return t * cos + up * sin_lo + dn * sin_hi

    def blocks_of(col0, width, use_rope, scale):
        r = jnp.dot(h, w_ref[:, col0:col0 + width],
                    preferred_element_type=jnp.float32)
        for c in range(width // LANES):
            t = r[:, c * LANES:(c + 1) * LANES]
            if use_rope:
                t = rope(t)
            if scale != 1.0:
                t = t * scale
            yield c, t

    def emit(out_ref, col0, width, use_rope, scale):
        for c, t in blocks_of(col0, width, use_rope, scale):
            out_ref[:, c * LANES:(c + 1) * LANES] = t.astype(out_ref.dtype)

    def emit_transposed(out_ref, col0, width, use_rope):
        for c, t in blocks_of(col0, width, use_rope, 1.0):
            out_ref[0, c * LANES:(c + 1) * LANES, :] = jnp.transpose(t).astype(
                out_ref.dtype)

    def emit_values(out_ref, col0):
        lane = lax.broadcasted_iota(jnp.int32, (tm, LANES), 1)
        tail = jnp.where(lane == HEAD_DIM, 1.0, 0.0)
        for c, t in blocks_of(col0, WIDTH, False, 1.0):
            pair = (t, pltpu.roll(t, HEAD_DIM, 1))
            for k in range(HEADS_PER_BLOCK):
                head = c * HEADS_PER_BLOCK + k
                out_ref[:, head * LANES:(head + 1) * LANES] = jnp.where(
                    lane < HEAD_DIM, pair[k], tail).astype(out_ref.dtype)

    qk_scale = HEAD_DIM ** -0.5 * LOG2_E
    emit(aq_ref, 0 * WIDTH, WIDTH, True, qk_scale)
    emit_transposed(ak_ref, 1 * WIDTH, WIDTH, True)
    emit_values(av_ref, 2 * WIDTH)
    emit(fq_ref, 3 * WIDTH, WIDTH, False, qk_scale)
    emit_transposed(fk_ref, 4 * WIDTH, WIDTH, False)
    emit_values(fv_ref, 5 * WIDTH)
    col = 6 * WIDTH
    emit(iq_ref, col, N_IDX_HEADS * IDX_DIM, True, IDX_DIM ** -0.5)
    col += N_IDX_HEADS * IDX_DIM
    emit(ik_ref, col, LANES, True, 1.0)
    col += LANES

    r = jnp.dot(h, w_ref[:, col:col + LANES], preferred_element_type=jnp.float32)
    z = r + bf_ref[...]
    logf = jnp.minimum(z, 0.0) - jnp.log1p(jnp.exp(-jnp.abs(z)))

    @pl.when(pl.program_id(0) % tiles_per_batch == 0)
    def _():
        carry_ref[...] = jnp.zeros_like(carry_ref)

    rows = lax.broadcasted_iota(jnp.int32, (tm, tm), 0)
    cols = lax.broadcasted_iota(jnp.int32, (tm, tm), 1)
    tri = jnp.where(cols <= rows, 1.0, 0.0).astype(jnp.bfloat16)
    hi = logf.astype(jnp.bfloat16)
    rem = logf - hi.astype(jnp.float32)
    mid = rem.astype(jnp.bfloat16)
    lo = (rem - mid.astype(jnp.float32)).astype(jnp.bfloat16)
    cum = (jnp.dot(tri, hi, preferred_element_type=jnp.float32)
           + jnp.dot(tri, mid, preferred_element_type=jnp.float32)
           + jnp.dot(tri, lo, preferred_element_type=jnp.float32)
           + carry_ref[...])
    carry_ref[...] = cum[tm - 1:tm, :]

    lane = lax.broadcasted_iota(jnp.int32, (tm, LANES), 1)
    misc = jnp.where(lane < MISC_CUM, r * (N_IDX_HEADS ** -0.5),
                     jnp.where(lane < MISC_CUM + N_HEADS, cum * LOG2_E, 0.0))
    misc_ref[...] = misc


def _project(x2, pos2, gain, invf, w_all, bf_row, seq):
    m = x2.shape[0]
    tm = PROJ_ROWS
    n_w = w_all.shape[1]
    row = lambda i: (i, 0)
    fixed = lambda i: (0, 0)
    bf16 = jnp.bfloat16
    tiles_per_batch = seq // tm
    batch = m // seq
    transposed = lambda i: (i // tiles_per_batch, 0, i % tiles_per_batch)

    def rows(width, dtype=bf16):
        return jax.ShapeDtypeStruct((m, width), dtype), pl.BlockSpec((tm, width), row)

    def keys(width):
        return (jax.ShapeDtypeStruct((batch, width, seq), bf16),
                pl.BlockSpec((1, width, tm), transposed))

    outs = [rows(WIDTH), keys(WIDTH), rows(N_HEADS * LANES),
            rows(WIDTH), keys(WIDTH), rows(N_HEADS * LANES),
            rows(N_IDX_HEADS * IDX_DIM), rows(LANES),
            rows(LANES, jnp.float32)]
    out_shape = [o[0] for o in outs]
    out_specs = [o[1] for o in outs]
    return pl.pallas_call(
        functools.partial(_proj_kernel, tiles_per_batch=tiles_per_batch),
        grid=(m // tm,),
        in_specs=[
            pl.BlockSpec((tm, D_MODEL), row),
            pl.BlockSpec((tm, 1), row),
            pl.BlockSpec((1, D_MODEL), fixed),
            pl.BlockSpec((1, LANES), fixed),
            pl.BlockSpec((D_MODEL, n_w), fixed),
            pl.BlockSpec((1, LANES), fixed),
        ],
        out_specs=out_specs,
        out_shape=out_shape,
        scratch_shapes=[pltpu.VMEM((1, LANES), jnp.float32)],
        compiler_params=pltpu.CompilerParams(
            dimension_semantics=("arbitrary",), vmem_limit_bytes=VMEM_LIMIT),
        name="in_proj",
    )(x2, pos2, gain, invf, w_all, bf_row)


def _half_select(x, first):
    lane = lax.broadcasted_iota(jnp.int32, x.shape, x.ndim - 1)
    keep = (lane < HEAD_DIM) if first else (lane >= HEAD_DIM)
    return jnp.where(keep, x, jnp.zeros_like(x))


def _stack_heads(q):
    return jnp.concatenate([_half_select(q, True), _half_select(q, False)], axis=0)


def _cover(limit, widths, fn, carry=0):
    widest = widths[0]
    n_widest = limit // widest

    def keys(start, width):
        return pl.ds(pl.multiple_of(start, width), width)

    carry = lax.fori_loop(
        0, n_widest, lambda j, c: fn(keys(j * widest, widest), widest, c), carry)
    start = n_widest * widest
    rest = limit - start
    for width in widths[1:]:
        part = rest & width
        carry = lax.cond(part != 0,
                         lambda c, start=start, width=width: fn(keys(start, width), width, c),
                         lambda c: c, carry)
        start = start + part
    return carry


def _attend(qs, kt_ref, v_ref, bias, i, widths, s_ref, m_ref, acc_ref):
    t = qs.shape[0] // HEADS_PER_BLOCK
    assert widths[-1] == t and all(a == 2 * b for a, b in zip(widths, widths[1:]))

    def phase_a(keys, width, diagonal):
        s = jnp.dot(qs, kt_ref[0, :, keys], preferred_element_type=jnp.float32)
        s = bias(s, keys, width, diagonal)
        s_ref[:, keys] = s
        m = m_ref[...]
        for c in range(width // LANES):
            m = jnp.maximum(m, s[:, c * LANES:(c + 1) * LANES])
        m_ref[...] = m

    def phase_b(keys, m):
        p = jnp.exp2(s_ref[:, keys] - m).astype(jnp.bfloat16)
        for k in range(HEADS_PER_BLOCK):
            acc_ref[k * t:(k + 1) * t, :] += jnp.dot(
                p[k * t:(k + 1) * t], v_ref[0, keys, k * LANES:(k + 1) * LANES],
                preferred_element_type=jnp.float32)

    def below_a(keys, width, carry):
        phase_a(keys, width, False)
        return carry

    diagonal = pl.ds(pl.multiple_of(i * t, t), t)
    m_ref[...] = jnp.full(m_ref.shape, NEG, jnp.float32)
    _cover(i * t, widths, below_a)
    phase_a(diagonal, t, True)

    m = jnp.max(m_ref[...], axis=1, keepdims=True)

    def below_b(keys, width, carry):
        phase_b(keys, m)
        return carry

    acc_ref[...] = jnp.zeros(acc_ref.shape, jnp.float32)
    _cover(i * t, widths, below_b)
    phase_b(diagonal, m)

    acc = acc_ref[...]
    o = acc / acc[:, HEAD_DIM:HEAD_DIM + 1]
    lane = lax.broadcasted_iota(jnp.int32, (t, LANES), 1)
    return jnp.where(lane < HEAD_DIM, o[:t], pltpu.roll(o[t:], HEAD_DIM, 1))


def _attend_scratch(t, s):
    rows = HEADS_PER_BLOCK * t
    return [pltpu.VMEM((rows, s), jnp.float32),
            pltpu.VMEM((rows, LANES), jnp.float32),
            pltpu.VMEM((rows, LANES), jnp.float32)]


def _index_mask(iq_ref, ik_ref, misc_ref, sct_ref, sc_ref, hi_ref, lo_ref, i):
    t = ATT_TILE
    top_k = float(TOPK_MAX)

    def fold(x, op, rows=SUBLANES):
        return op(x.reshape(x.shape[0] // rows, rows, t), axis=0)

    iq = iq_ref[0]
    qt = []
    for h in range(N_IDX_HEADS):
        block = _half_select(iq[:, (h // 2) * LANES:(h // 2 + 1) * LANES], h % 2 == 0)
        qt.append(jnp.transpose(block.astype(jnp.float32)).astype(jnp.bfloat16))
    misc_t = jnp.transpose(misc_ref[0])
    wh = [misc_t[MISC_IW + h:MISC_IW + h + 1, :] for h in range(N_IDX_HEADS)]

    def scores(keys):
        k = ik_ref[0, keys, :]
        sc = None
        for h in range(N_IDX_HEADS):
            rel = jnp.dot(k, qt[h], preferred_element_type=jnp.float32)
            term = wh[h] * jnp.maximum(rel, 0.0)
            sc = term if sc is None else sc + term
        return sc

    def track(for_max, for_min):
        hi_ref[...] = jnp.maximum(hi_ref[...], fold(for_max, jnp.max))
        lo_ref[...] = jnp.minimum(lo_ref[...], fold(for_min, jnp.min))

    def score_block(keys, width, carry):
        sc = scores(keys)
        sct_ref[keys, :] = sc
        track(sc, sc)
        return carry

    hi_ref[...] = jnp.full(hi_ref.shape, -jnp.inf, jnp.float32)
    lo_ref[...] = jnp.full(lo_ref.shape, jnp.inf, jnp.float32)
    _cover(i * t, INDEX_WIDTHS, score_block)
    diag = pl.ds(pl.multiple_of(i * t, t), t)
    sc = scores(diag)
    causal = (lax.broadcasted_iota(jnp.int32, (t, t), 0)
              <= lax.broadcasted_iota(jnp.int32, (t, t), 1))
    masked = jnp.where(causal, sc, -jnp.inf)
    sct_ref[diag, :] = masked
    track(masked, jnp.where(causal, sc, jnp.inf))

    def count_ge(thr):
        def block(keys, width, acc):
            hit = jnp.where(sct_ref[keys, :] >= thr, 1.0, 0.0)
            return acc + fold(hit, jnp.sum, COUNT_CHAINS * SUBLANES)

        acc = _cover((i + 1) * t, INDEX_WIDTHS, block,
                     jnp.zeros((COUNT_CHAINS * SUBLANES, t), jnp.float32))
        return jnp.sum(acc, axis=0, keepdims=True)

    def flag(cond):
        return jnp.where(cond, 1.0, 0.0)

    rmax = jnp.max(hi_ref[...], axis=0, keepdims=True)
    rmin = jnp.min(lo_ref[...], axis=0, keepdims=True)
    n_causal = (i * t + 1 + lax.broadcasted_iota(jnp.int32, (1, t), 1)
                ).astype(jnp.float32)
    c_pos = count_ge(jnp.full((1, t), TINY, jnp.float32))
    c_zero = count_ge(jnp.zeros((1, t), jnp.float32))
    c_max = count_ge(rmax)
    take_all = n_causal <= top_k
    tie_max = c_max >= top_k
    tie_zero = jnp.where(c_pos < top_k, flag(c_zero >= top_k), 0.0) > 0.5
    done = jnp.where(take_all, 1.0, jnp.where(tie_max, 1.0, jnp.where(
        tie_zero, 1.0, flag(c_pos == top_k))))
    thr = jnp.where(take_all, ALL_THR, jnp.where(tie_max, rmax, jnp.where(
        tie_zero, 0.0, TINY)))
    take = jnp.where(take_all, BIG, jnp.where(tie_max, top_k, jnp.where(
        tie_zero, top_k - c_pos, BIG)))
    positive = c_pos > top_k
    lo = jnp.where(positive, TINY, rmin)
    c_lo = jnp.where(positive, c_pos, n_causal)
    hi = jnp.where(positive, rmax, 0.0)
    c_hi = jnp.where(positive, c_max, c_zero)
    log_target = float(np.log(top_k + 0.5))

    def cond(state):
        it, done = state[0], state[-1]
        return (it < MAX_SEARCH) & (jnp.min(done) < 0.5)

    def step(state):
        it, lo, hi, c_lo, c_hi, w_lo, w_hi, last, thr, take, done = state
        f_lo = (jnp.log(c_lo) - log_target) * w_lo
        f_hi = (log_target - jnp.log(jnp.maximum(c_hi, 0.5))) * w_hi
        frac = f_lo / (f_lo + f_hi)
        frac = jnp.clip(frac, MIN_FRACTION, 1.0 - MIN_FRACTION)
        frac = jnp.where(it < INTERPOLATE_STEPS, frac, 0.5)
        guess = lo + (hi - lo) * frac
        halfway = 0.5 * lo + 0.5 * hi
        mid = jnp.where(guess <= lo, halfway, jnp.where(guess >= hi, halfway, guess))
        spent = jnp.where(mid <= lo, 1.0, flag(mid >= hi))
        c = count_ge(mid)
        finish = (1.0 - done) * jnp.maximum(spent, flag(c == top_k)) > 0.5
        thr = jnp.where(finish, jnp.where(spent > 0.5, lo, mid), thr)
        take = jnp.where(finish, jnp.where(spent > 0.5, top_k - c_hi, BIG), take)
        above = c > top_k
        below = c < top_k
        lo = jnp.where(above, mid, lo)
        c_lo = jnp.where(above, c, c_lo)
        hi = jnp.where(below, mid, hi)
        c_hi = jnp.where(below, c, c_hi)
        w_hi = jnp.where(above, jnp.where(last > 0.5, 0.5 * w_hi, 1.0),
                         jnp.where(below, 1.0, w_hi))
        w_lo = jnp.where(below, jnp.where(last < -0.5, 0.5 * w_lo, 1.0),
                         jnp.where(above, 1.0, w_lo))
        last = jnp.where(above, 1.0, jnp.where(below, -1.0, last))
        done = jnp.where(finish, 1.0, done)
        return it + 1, lo, hi, c_lo, c_hi, w_lo, w_hi, last, thr, take, done

    one = jnp.ones((1, t), jnp.float32)
    _, lo, _, _, c_hi, _, _, _, thr, take, done = lax.while_loop(
        cond, step, (jnp.int32(0), lo, hi, c_lo, c_hi, one, one, 0.0 * one,
                     thr, take, done))
    thr = jnp.where(done < 0.5, lo, thr)
    take = jnp.where(done < 0.5, top_k - c_hi, take)

    ranked = jnp.max(flag(take < BIG)) > 0.5

    @pl.when(jnp.logical_not(ranked))
    def _():
        def tile(j, carry):
            keys = pl.ds(pl.multiple_of(j * t, t), t)
            sc_ref[:, keys] = jnp.transpose(
                jnp.where(sct_ref[keys, :] >= thr, 0.0, NEG))
            return carry

        lax.fori_loop(0, i + 1, tile, 0)

    @pl.when(ranked)
    def _():
        earlier = (lax.broadcasted_iota(jnp.int32, (t, t), 1)
                   < lax.broadcasted_iota(jnp.int32, (t, t), 0))
        earlier = jnp.where(earlier, 1.0, 0.0).astype(jnp.bfloat16)

        def tile(j, seen):
            keys = pl.ds(pl.multiple_of(j * t, t), t)
            s = sct_ref[keys, :]
            eq = flag(s == thr)
            rank = seen + jnp.dot(earlier, eq.astype(jnp.bfloat16),
                                  preferred_element_type=jnp.float32)
            sc_ref[:, keys] = jnp.transpose(jnp.where(s > thr, 0.0, jnp.where(
                s == thr, jnp.where(rank < take, 0.0, NEG), NEG)))
            return seen + jnp.sum(eq, axis=0, keepdims=True)

        lax.fori_loop(0, i + 1, tile, jnp.zeros((1, t), jnp.float32))


def _dsa_kernel(iq_ref, ik_ref, misc_ref, q_ref, kt_ref, v_ref, o_ref, sct_ref, sc_ref,
                hi_ref, lo_ref, s_ref, m_ref, acc_ref):
    i = pl.program_id(1)

    @pl.when(pl.program_id(2) == 0)
    def _():
        _index_mask(iq_ref, ik_ref, misc_ref, sct_ref, sc_ref, hi_ref, lo_ref, i)

    def bias(s, keys, width, diagonal):
        mask = sc_ref[:, keys]
        return s + jnp.concatenate([mask, mask], axis=0)

    o_ref[0] = _attend(_stack_heads(q_ref[0]), kt_ref, v_ref, bias, i, DSA_WIDTHS,
                       s_ref, m_ref, acc_ref)


def _dsa_attention(iq, ik2, misc, aq, ak_t, av):
    b, s, _ = aq.shape
    t = ATT_TILE
    return pl.pallas_call(
        _dsa_kernel,
        grid=(b, s // t, N_HEAD_BLOCKS),
        in_specs=[
            pl.BlockSpec((1, t, N_IDX_HEADS * IDX_DIM), lambda bb, i, hp: (bb, i, 0)),
            pl.BlockSpec((1, s, LANES), lambda bb, i, hp: (bb, 0, 0)),
            pl.BlockSpec((1, t, LANES), lambda bb, i, hp: (bb, i, 0)),
            pl.BlockSpec((1, t, LANES), lambda bb, i, hp: (bb, i, hp)),
            pl.BlockSpec((1, LANES, s), lambda bb, i, hp: (bb, hp, 0)),
            pl.BlockSpec((1, s, HEADS_PER_BLOCK * LANES), lambda bb, i, hp: (bb, 0, hp)),
        ],
        out_specs=pl.BlockSpec((1, t, LANES), lambda bb, i, hp: (bb, i, hp)),
        out_shape=jax.ShapeDtypeStruct((b, s, WIDTH), jnp.float32),
        scratch_shapes=[pltpu.VMEM((s, t), jnp.float32),
                        pltpu.VMEM((t, s), jnp.float32),
                        pltpu.VMEM((SUBLANES, t), jnp.float32),
                        pltpu.VMEM((SUBLANES, t), jnp.float32)] + _attend_scratch(t, s),
        compiler_params=pltpu.CompilerParams(
            dimension_semantics=("arbitrary", "arbitrary", "arbitrary"),
            vmem_limit_bytes=VMEM_LIMIT),
        name="dsa_attention",
    )(iq, ik2, misc, aq, ak_t, av)


def _fox_kernel(q_ref, kt_ref, v_ref, cq_ref, ck_ref, o_ref, s_ref, m_ref, acc_ref):
    t = q_ref.shape[1]
    cq = jnp.concatenate([cq_ref[0, 0, :, 0:1], cq_ref[0, 0, :, 1:2]], axis=0)

    def bias(s, keys, width, diagonal):
        ck = ck_ref[0, 0, :, keys]
        ck = jnp.concatenate([jnp.broadcast_to(ck[0:1], (t, width)),
                              jnp.broadcast_to(ck[1:2], (t, width))], axis=0)
        s = s + (cq - ck)
        if diagonal:
            below = (lax.broadcasted_iota(jnp.int32, (t, t), 1)
                     <= lax.broadcasted_iota(jnp.int32, (t, t), 0))
            s = jnp.where(jnp.concatenate([below, below], axis=0), s, NEG)
        return s

    o_ref[0] = _attend(_stack_heads(q_ref[0]), kt_ref, v_ref, bias, pl.program_id(2),
                       FOX_WIDTHS, s_ref, m_ref, acc_ref)


def _fox_attention(fq, fk_t, fv, cum_q, cum_k):
    b, s, _ = fq.shape
    t = FOX_TILE
    return pl.pallas_call(
        _fox_kernel,
        grid=(b, N_HEAD_BLOCKS, s // t),
        in_specs=[
            pl.BlockSpec((1, t, LANES), lambda bb, hp, i: (bb, i, hp)),
            pl.BlockSpec((1, LANES, s), lambda bb, hp, i: (bb, hp, 0)),
            pl.BlockSpec((1, s, HEADS_PER_BLOCK * LANES), lambda bb, hp, i: (bb, 0, hp)),
            pl.BlockSpec((1, 1, t, HEADS_PER_BLOCK), lambda bb, hp, i: (bb, hp, i, 0)),
            pl.BlockSpec((1, 1, HEADS_PER_BLOCK, s), lambda bb, hp, i: (bb, hp, 0, 0)),
        ],
        out_specs=pl.BlockSpec((1, t, LANES), lambda bb, hp, i: (bb, i, hp)),
        out_shape=jax.ShapeDtypeStruct((b, s, WIDTH), jnp.float32),
        scratch_shapes=_attend_scratch(t, s),
        compiler_params=pltpu.CompilerParams(
            dimension_semantics=("arbitrary", "arbitrary", "arbitrary"),
            vmem_limit_bytes=VMEM_LIMIT),
        name="fox_attention",
    )(fq, fk_t, fv, cum_q, cum_k)


def _out_kernel(x_ref, ya_ref, yb_ref, gain_ref, wg_ref, bm_ref, wa_ref, wb_ref,
                wo_ref, ngain_ref, o_ref):
    x = x_ref[...]
    h = _rms(x, gain_ref[...]).astype(jnp.bfloat16)
    g = jnp.dot(h, wg_ref[...], preferred_element_type=jnp.float32)
    a_gate = g[:, :WIDTH]
    f_gate = g[:, WIDTH:2 * WIDTH]
    m_logit = g[:, 2 * WIDTH:] + bm_ref[...]
    za = (ya_ref[...] * (a_gate * jax.nn.sigmoid(a_gate))).astype(jnp.bfloat16)
    zb = (yb_ref[...] * (f_gate * jax.nn.sigmoid(f_gate))).astype(jnp.bfloat16)
    ua = jnp.dot(za, wa_ref[...], preferred_element_type=jnp.float32)
    ub = jnp.dot(zb, wb_ref[...], preferred_element_type=jnp.float32)
    gates = jax.nn.sigmoid(m_logit)
    merged = gates[:, :D_MODEL] * ua + gates[:, D_MODEL:] * ub
    y = x + jnp.dot(merged.astype(jnp.bfloat16), wo_ref[...],
                    preferred_element_type=jnp.float32)
    o_ref[...] = _rms(y, ngain_ref[...])


def _output(x2, ya, yb, gain, w_gates, b_merge, w_a, w_b, w_o, next_gain):
    m = x2.shape[0]
    tm = PROJ_ROWS
    row = lambda i: (i, 0)
    fixed = lambda i: (0, 0)
    return pl.pallas_call(
        _out_kernel,
        grid=(m // tm,),
        in_specs=[
            pl.BlockSpec((tm, D_MODEL), row),
            pl.BlockSpec((tm, WIDTH), row),
            pl.BlockSpec((tm, WIDTH), row),
            pl.BlockSpec((1, D_MODEL), fixed),
            pl.BlockSpec(w_gates.shape, fixed),
            pl.BlockSpec((1, 2 * D_MODEL), fixed),
            pl.BlockSpec(w_a.shape, fixed),
            pl.BlockSpec(w_b.shape, fixed),
            pl.BlockSpec(w_o.shape, fixed),
            pl.BlockSpec((1, D_MODEL), fixed),
        ],
        out_specs=pl.BlockSpec((tm, D_MODEL), row),
        out_shape=jax.ShapeDtypeStruct((m, D_MODEL), jnp.float32),
        compiler_params=pltpu.CompilerParams(
            dimension_semantics=("arbitrary",), vmem_limit_bytes=VMEM_LIMIT),
        name="out_proj",
    )(x2, ya, yb, gain, w_gates, b_merge, w_a, w_b, w_o, next_gain)


def _split_w_in(w):
    sizes = (WIDTH, WIDTH, WIDTH, WIDTH, N_IDX_HEADS * IDX_DIM, IDX_DIM, N_IDX_HEADS,
             WIDTH, WIDTH, WIDTH, WIDTH, N_HEADS, 2 * D_MODEL)
    points = np.cumsum(sizes)[:-1]
    return jnp.split(w, points, axis=-1)


def kernel(x, positions, norm_gain, w_in, b_forget, b_merge, w_branch_dsa,
           w_branch_fox, w_out, final_gain):
    b, s, d = x.shape
    depth = w_in.shape[0]
    bf16 = jnp.bfloat16

    half = jnp.arange(ROPE_HALF, dtype=jnp.float32)
    inv_freq = ROPE_THETA ** (-half * 2.0 / ROPE_DIM)
    dim = np.arange(LANES) % HEAD_DIM
    invf = jnp.where(dim < ROPE_DIM, inv_freq[dim % ROPE_HALF], 0.0)[None, :]
    pos2 = positions.reshape(b * s, 1)

    x2 = x.reshape(b * s, d)
    normed = None
    for l in range(depth):
        (w_aq, w_ak, w_av, w_ag, w_iq, w_ik, w_iw,
         w_fq, w_fk, w_fv, w_fg, w_fl, w_m) = _split_w_in(w_in[l])
        pad = jnp.zeros((d, LANES - N_IDX_HEADS - N_HEADS), w_in.dtype)
        w_all = jnp.concatenate(
            [w_aq, w_ak, w_av, w_fq, w_fk, w_fv, w_iq, w_ik, w_ik, w_iw, w_fl, pad],
            axis=1).astype(bf16)
        w_gates = jnp.concatenate([w_ag, w_fg, w_m], axis=1).astype(bf16)
        bf_row = jnp.zeros((1, LANES), jnp.float32).at[
            0, MISC_CUM:MISC_CUM + N_HEADS].set(b_forget[l])
        gain = norm_gain[l][None, :]
        next_gain = (norm_gain[l + 1] if l + 1 < depth else final_gain)[None, :]

        aq, ak_t, av, fq, fk_t, fv, iq, ik2, misc = _project(
            x2, pos2, gain, invf, w_all, bf_row, s)
        shape3 = lambda a: a.reshape(b, s, a.shape[-1])
        ya = _dsa_attention(shape3(iq), shape3(ik2), shape3(misc), shape3(aq), ak_t,
                            shape3(av))
        cum = misc[:, MISC_CUM:MISC_CUM + N_HEADS].reshape(
            b, s, N_HEAD_BLOCKS, HEADS_PER_BLOCK)
        cum_q = cum.transpose(0, 2, 1, 3)
        cum_k = cum.transpose(0, 2, 3, 1)
        yb = _fox_attention(shape3(fq), fk_t, shape3(fv), cum_q, cum_k)
        normed = _output(x2, ya.reshape(b * s, WIDTH), yb.reshape(b * s, WIDTH),
                         gain, w_gates, b_merge[l][None, :],
                         w_branch_dsa[l].astype(bf16), w_branch_fox[l].astype(bf16),
                         w_out[l].astype(bf16), next_gain)
        assert depth == 1, "stacked layers need the un-normalised residual as well"
    return normed.reshape(b, s, d)
```

```python
import functools

import jax
import jax.numpy as jnp
import numpy as np
from jax import lax
from jax.experimental import pallas as pl
from jax.experimental.pallas import tpu as pltpu

D_MODEL = 1024
HEAD_DIM = 64
N_HEADS = 8
WIDTH = N_HEADS * HEAD_DIM
N_IDX_HEADS = 4
IDX_DIM = 64
TOPK_MAX = 256
ROPE_THETA = 500000.0
ROPE_DIM = HEAD_DIM // 4
ROPE_HALF = ROPE_DIM // 2
RMS_EPS = 1e-6
NEG = -1e30
LOG2_E = float(np.log2(np.e))

LANES = 128
SUBLANES = 8
HEADS_PER_BLOCK = LANES // HEAD_DIM
N_HEAD_BLOCKS = N_HEADS // HEADS_PER_BLOCK
VMEM_LIMIT = 56 * 1024 * 1024

PROJ_ROWS = 512
ATT_TILE = 256
INDEX_WIDTHS = (2048, 1024, 512, 256)
DSA_WIDTHS = (2048, 1024, 512, 256)
FOX_TILE = 512
FOX_WIDTHS = (2048, 1024, 512)
COUNT_CHAINS = 4

TINY = float(np.finfo(np.float32).tiny)
ALL_THR = -3.0e38
BIG = 1.0e9
MAX_SEARCH = 400
INTERPOLATE_STEPS = 32
MIN_FRACTION = 1.0 / 32.0

MISC_IW = 0
MISC_CUM = N_IDX_HEADS


def _rms(x, gain):
    ms = jnp.mean(x * x, axis=-1, keepdims=True)
    return x * lax.rsqrt(ms + RMS_EPS) * gain


def _proj_kernel(x_ref, pos_ref, gain_ref, invf_ref, w_ref, bf_ref,
                 aq_ref, ak_ref, av_ref, fq_ref, fk_ref, fv_ref, iq_ref, ik_ref,
                 misc_ref, carry_ref, *, tiles_per_batch):
    tm = x_ref.shape[0]
    h = _rms(x_ref[...], gain_ref[...]).astype(jnp.bfloat16)

    ang = pos_ref[...].astype(jnp.float32) * invf_ref[...]
    cos = jnp.cos(ang)
    sin = jnp.sin(ang)
    d = lax.broadcasted_iota(jnp.int32, (tm, LANES), 1) % HEAD_DIM
    sin_lo = jnp.where(d < ROPE_HALF, -sin, 0.0)
    sin_hi = jnp.where(d >= ROPE_HALF, sin, 0.0)

    def rope(t):
        up = pltpu.roll(t, LANES - ROPE_HALF, 1)
        dn = pltpu.roll(t, ROPE_HALF, 1)
        return t * cos + up * sin_lo + dn * sin_hi

    def blocks_of(col0, width, use_rope, scale):
        r = jnp.dot(h, w_ref[:, col0:col0 + width],
                    preferred_element_type=jnp.float32)
        for c in range(width // LANES):
            t = r[:, c * LANES:(c + 1) * LANES]
            if use_rope:
                t = rope(t)
            if scale != 1.0:
                t = t * scale
            yield c, t

    def emit(out_ref, col0, width, use_rope, scale):
        for c, t in blocks_of(col0, width, use_rope, scale):
            out_ref[:, c * LANES:(c + 1) * LANES] = t.astype(out_ref.dtype)

    def emit_transposed(out_ref, col0, width, use_rope):
        for c, t in blocks_of(col0, width, use_rope, 1.0):
            out_ref[0, c * LANES:(c + 1) * LANES, :] = jnp.transpose(t).astype(
                out_ref.dtype)

    def emit_values(out_ref, col0):
        lane = lax.broadcasted_iota(jnp.int32, (tm, LANES), 1)
        tail = jnp.where(lane == HEAD_DIM, 1.0, 0.0)
        for c, t in blocks_of(col0, WIDTH, False, 1.0):
            pair = (t, pltpu.roll(t, HEAD_DIM, 1))
            for k in range(HEADS_PER_BLOCK):
                head = c * HEADS_PER_BLOCK + k
                out_ref[:, head * LANES:(head + 1) * LANES] = jnp.where(
                    lane < HEAD_DIM, pair[k], tail).astype(out_ref.dtype)

    qk_scale = HEAD_DIM ** -0.5 * LOG2_E
    emit(aq_ref, 0 * WIDTH, WIDTH, True, qk_scale)
    emit_transposed(ak_ref, 1 * WIDTH, WIDTH, True)
    emit_values(av_ref, 2 * WIDTH)
    emit(fq_ref, 3 * WIDTH, WIDTH, False, qk_scale)
    emit_transposed(fk_ref, 4 * WIDTH, WIDTH, False)
    emit_values(fv_ref, 5 * WIDTH)
    col = 6 * WIDTH
    emit(iq_ref, col, N_IDX_HEADS * IDX_DIM, True, IDX_DIM ** -0.5)
    col += N_IDX_HEADS * IDX_DIM
    emit(ik_ref, col, LANES, True, 1.0)
    col += LANES

    r = jnp.dot(h, w_ref[:, col:col + LANES], preferred_element_type=jnp.float32)
    z = r + bf_ref[...]
    logf = jnp.minimum(z, 0.0) - jnp.log1p(jnp.exp(-jnp.abs(z)))

    @pl.when(pl.program_id(0) % tiles_per_batch == 0)
    def _():
        carry_ref[...] = jnp.zeros_like(carry_ref)

    rows = lax.broadcasted_iota(jnp.int32, (tm, tm), 0)
    cols = lax.broadcasted_iota(jnp.int32, (tm, tm), 1)
    tri = jnp.where(cols <= rows, 1.0, 0.0).astype(jnp.bfloat16)
    hi = logf.astype(jnp.bfloat16)
    rem = logf - hi.astype(jnp.float32)
    mid = rem.astype(jnp.bfloat16)
    lo = (rem - mid.astype(jnp.float32)).astype(jnp.bfloat16)
    cum = (jnp.dot(tri, hi, preferred_element_type=jnp.float32)
           + jnp.dot(tri, mid, preferred_element_type=jnp.float32)
           + jnp.dot(tri, lo, preferred_element_type=jnp.float32)
           + carry_ref[...])
    carry_ref[...] = cum[tm - 1:tm, :]

    lane = lax.broadcasted_iota(jnp.int32, (tm, LANES), 1)
    misc = jnp.where(lane < MISC_CUM, r * (N_IDX_HEADS ** -0.5),
                     jnp.where(lane < MISC_CUM + N_HEADS, cum * LOG2_E, 0.0))
    misc_ref[...] = misc


def _project(x2, pos2, gain, invf, w_all, bf_row, seq):
    m = x2.shape[0]
    tm = PROJ_ROWS
    n_w = w_all.shape[1]
    row = lambda i: (i, 0)
    fixed = lambda i: (0, 0)
    bf16 = jnp.bfloat16
    tiles_per_batch = seq // tm
    batch = m // seq
    transposed = lambda i: (i // tiles_per_batch, 0, i % tiles_per_batch)

    def rows(width, dtype=bf16):
        return jax.ShapeDtypeStruct((m, width), dtype), pl.BlockSpec((tm, width), row)

    def keys(width):
        return (jax.ShapeDtypeStruct((batch, width, seq), bf16),
                pl.BlockSpec((1, width, tm), transposed))

    outs = [rows(WIDTH), keys(WIDTH), rows(N_HEADS * LANES),
            rows(WIDTH), keys(WIDTH), rows(N_HEADS * LANES),
            rows(N_IDX_HEADS * IDX_DIM), rows(LANES),
            rows(LANES, jnp.float32)]
    out_shape = [o[0] for o in outs]
    out_specs = [o[1] for o in outs]
    return pl.pallas_call(
        functools.partial(_proj_kernel, tiles_per_batch=tiles_per_batch),
        grid=(m // tm,),
        in_specs=[
            pl.BlockSpec((tm, D_MODEL), row),
            pl.BlockSpec((tm, 1), row),
            pl.BlockSpec((1, D_MODEL), fixed),
            pl.BlockSpec((1, LANES), fixed),
            pl.BlockSpec((D_MODEL, n_w), fixed),
            pl.BlockSpec((1, LANES), fixed),
        ],
        out_specs=out_specs,
        out_shape=out_shape,
        scratch_shapes=[pltpu.VMEM((1, LANES), jnp.float32)],
        compiler_params=pltpu.CompilerParams(
            dimension_semantics=("arbitrary",), vmem_limit_bytes=VMEM_LIMIT),
        name="in_proj",
    )(x2, pos2, gain, invf, w_all, bf_row)


def _half_select(x, first):
    lane = lax.broadcasted_iota(jnp.int32, x.shape, x.ndim - 1)
    keep = (lane < HEAD_DIM) if first else (lane >= HEAD_DIM)
    return jnp.where(keep, x, jnp.zeros_like(x))


def _stack_heads(q):
    return jnp.concatenate([_half_select(q, True), _half_select(q, False)], axis=0)


def _cover(limit, widths, fn, carry=0):
    widest = widths[0]
    n_widest = limit // widest

    def keys(start, width):
        return pl.ds(pl.multiple_of(start, width), width)

    carry = lax.fori_loop(
        0, n_widest, lambda j, c: fn(keys(j * widest, widest), widest, c), carry)
    start = n_widest * widest
    rest = limit - start
    for width in widths[1:]:
        part = rest & width
        carry = lax.cond(part != 0,
                         lambda c, start=start, width=width: fn(keys(start, width), width, c),
                         lambda c: c, carry)
        start = start + part
    return carry


def _attend(qs, kt_ref, v_ref, bias, i, widths, s_ref, m_ref, acc_ref):
    t = qs.shape[0] // HEADS_PER_BLOCK
    assert widths[-1] == t and all(a == 2 * b for a, b in zip(widths, widths[1:]))

    def phase_a(keys, width, diagonal):
        s = jnp.dot(qs, kt_ref[0, :, keys], preferred_element_type=jnp.float32)
        s = bias(s, keys, width, diagonal)
        s_ref[:, keys] = s
        m = m_ref[...]
        for c in range(width // LANES):
            m = jnp.maximum(m, s[:, c * LANES:(c + 1) * LANES])
        m_ref[...] = m

    def phase_b(keys, m):
        p = jnp.exp2(s_ref[:, keys] - m).astype(jnp.bfloat16)
        for k in range(HEADS_PER_BLOCK):
            acc_ref[k * t:(k + 1) * t, :] += jnp.dot(
                p[k * t:(k + 1) * t], v_ref[0, keys, k * LANES:(k + 1) * LANES],
                preferred_element_type=jnp.float32)

    def below_a(keys, width, carry):
        phase_a(keys, width, False)
        return carry

    diagonal = pl.ds(pl.multiple_of(i * t, t), t)
    m_ref[...] = jnp.full(m_ref.shape, NEG, jnp.float32)
    _cover(i * t, widths, below_a)
    phase_a(diagonal, t, True)

    m = jnp.max(m_ref[...], axis=1, keepdims=True)

    def below_b(keys, width, carry):
        phase_b(keys, m)
        return carry

    acc_ref[...] = jnp.zeros(acc_ref.shape, jnp.float32)
    _cover(i * t, widths, below_b)
    phase_b(diagonal, m)

    acc = acc_ref[...]
    o = acc / acc[:, HEAD_DIM:HEAD_DIM + 1]
    lane = lax.broadcasted_iota(jnp.int32, (t, LANES), 1)
    return jnp.where(lane < HEAD_DIM, o[:t], pltpu.roll(o[t:], HEAD_DIM, 1))


def _attend_scratch(t, s):
    rows = HEADS_PER_BLOCK * t
    return [pltpu.VMEM((rows, s), jnp.float32),
            pltpu.VMEM((rows, LANES), jnp.float32),
            pltpu.VMEM((rows, LANES), jnp.float32)]


def _index_mask(iq_ref, ik_ref, misc_ref, sct_ref, sc_ref, hi_ref, lo_ref, i):
    t = ATT_TILE
    top_k = float(TOPK_MAX)

    def fold(x, op, rows=SUBLANES):
        return op(x.reshape(x.shape[0] // rows, rows, t), axis=0)

    iq = iq_ref[0]
    qt = []
    for h in range(N_IDX_HEADS):
        block = _half_select(iq[:, (h // 2) * LANES:(h // 2 + 1) * LANES], h % 2 == 0)
        qt.append(jnp.transpose(block.astype(jnp.float32)).astype(jnp.bfloat16))
    misc_t = jnp.transpose(misc_ref[0])
    wh = [misc_t[MISC_IW + h:MISC_IW + h + 1, :] for h in range(N_IDX_HEADS)]

    def scores(keys):
        k = ik_ref[0, keys, :]
        sc = None
        for h in range(N_IDX_HEADS):
            rel = jnp.dot(k, qt[h], preferred_element_type=jnp.float32)
            term = wh[h] * jnp.maximum(rel, 0.0)
            sc = term if sc is None else sc + term
        return sc

    def track(for_max, for_min):
        hi_ref[...] = jnp.maximum(hi_ref[...], fold(for_max, jnp.max))
        lo_ref[...] = jnp.minimum(lo_ref[...], fold(for_min, jnp.min))

    def score_block(keys, width, carry):
        sc = scores(keys)
        sct_ref[keys, :] = sc
        track(sc, sc)
        return carry

    hi_ref[...] = jnp.full(hi_ref.shape, -jnp.inf, jnp.float32)
    lo_ref[...] = jnp.full(lo_ref.shape, jnp.inf, jnp.float32)
    _cover(i * t, INDEX_WIDTHS, score_block)
    diag = pl.ds(pl.multiple_of(i * t, t), t)
    sc = scores(diag)
    causal = (lax.broadcasted_iota(jnp.int32, (t, t), 0)
              <= lax.broadcasted_iota(jnp.int32, (t, t), 1))
    masked = jnp.where(causal, sc, -jnp.inf)
    sct_ref[diag, :] = masked
    track(masked, jnp.where(causal, sc, jnp.inf))

    def count_ge(thr):
        rows = COUNT_CHAINS * SUBLANES

        def block(keys, width, acc):
            for r in range(width // rows):
                part = pl.ds(pl.multiple_of(keys.start + r * rows, rows), rows)
                acc = acc + jnp.where(sct_ref[part, :] >= thr, 1.0, 0.0)
            return acc

        acc = _cover((i + 1) * t, INDEX_WIDTHS, block,
                     jnp.zeros((rows, t), jnp.float32))
        return jnp.sum(acc, axis=0, keepdims=True)

    def flag(cond):
        return jnp.where(cond, 1.0, 0.0)

    rmax = jnp.max(hi_ref[...], axis=0, keepdims=True)
    rmin = jnp.min(lo_ref[...], axis=0, keepdims=True)
    n_causal = (i * t + 1 + lax.broadcasted_iota(jnp.int32, (1, t), 1)
                ).astype(jnp.float32)
    c_pos = count_ge(jnp.full((1, t), TINY, jnp.float32))
    c_zero = count_ge(jnp.zeros((1, t), jnp.float32))
    c_max = count_ge(rmax)
    take_all = n_causal <= top_k
    tie_max = c_max >= top_k
    tie_zero = jnp.where(c_pos < top_k, flag(c_zero >= top_k), 0.0) > 0.5
    done = jnp.where(take_all, 1.0, jnp.where(tie_max, 1.0, jnp.where(
        tie_zero, 1.0, flag(c_pos == top_k))))
    thr = jnp.where(take_all, ALL_THR, jnp.where(tie_max, rmax, jnp.where(
        tie_zero, 0.0, TINY)))
    take = jnp.where(take_all, BIG, jnp.where(tie_max, top_k, jnp.where(
        tie_zero, top_k - c_pos, BIG)))
    positive = c_pos > top_k
    lo = jnp.where(positive, TINY, rmin)
    c_lo = jnp.where(positive, c_pos, n_causal)
    hi = jnp.where(positive, rmax, 0.0)
    c_hi = jnp.where(positive, c_max, c_zero)
    log_target = float(np.log(top_k + 0.5))

    def cond(state):
        it, done = state[0], state[-1]
        return (it < MAX_SEARCH) & (jnp.min(done) < 0.5)

    def step(state):
        it, lo, hi, c_lo, c_hi, w_lo, w_hi, last, thr, take, done = state
        f_lo = (jnp.log(c_lo) - log_target) * w_lo
        f_hi = (log_target - jnp.log(jnp.maximum(c_hi, 0.5))) * w_hi
        frac = f_lo / (f_lo + f_hi)
        frac = jnp.clip(frac, MIN_FRACTION, 1.0 - MIN_FRACTION)
        frac = jnp.where(it < INTERPOLATE_STEPS, frac, 0.5)
        guess = lo + (hi - lo) * frac
        halfway = 0.5 * lo + 0.5 * hi
        mid = jnp.where(guess <= lo, halfway, jnp.where(guess >= hi, halfway, guess))
        spent = jnp.where(mid <= lo, 1.0, flag(mid >= hi))
        c = count_ge(mid)
        finish = (1.0 - done) * jnp.maximum(spent, flag(c == top_k)) > 0.5
        thr = jnp.where(finish, jnp.where(spent > 0.5, lo, mid), thr)
        take = jnp.where(finish, jnp.where(spent > 0.5, top_k - c_hi, BIG), take)
        above = c > top_k
        below = c < top_k
        lo = jnp.where(above, mid, lo)
        c_lo = jnp.where(above, c, c_lo)
        hi = jnp.where(below, mid, hi)
        c_hi = jnp.where(below, c, c_hi)
        w_hi = jnp.where(above, jnp.where(last > 0.5, 0.5 * w_hi, 1.0),
                         jnp.where(below, 1.0, w_hi))
        w_lo = jnp.where(below, jnp.where(last < -0.5, 0.5 * w_lo, 1.0),
                         jnp.where(above, 1.0, w_lo))
        last = jnp.where(above, 1.0, jnp.where(below, -1.0, last))
        done = jnp.where(finish, 1.0, done)
        return it + 1, lo, hi, c_lo, c_hi, w_lo, w_hi, last, thr, take, done

    one = jnp.ones((1, t), jnp.float32)
    _, lo, _, _, c_hi, _, _, _, thr, take, done = lax.while_loop(
        cond, step, (jnp.int32(0), lo, hi, c_lo, c_hi, one, one, 0.0 * one,
                     thr, take, done))
    thr = jnp.where(done < 0.5, lo, thr)
    take = jnp.where(done < 0.5, top_k - c_hi, take)

    ranked = jnp.max(flag(take < BIG)) > 0.5

    @pl.when(jnp.logical_not(ranked))
    def _():
        def tile(j, carry):
            keys = pl.ds(pl.multiple_of(j * t, t), t)
            sc_ref[:, keys] = jnp.transpose(
                jnp.where(sct_ref[keys, :] >= thr, 0.0, NEG))
            return carry

        lax.fori_loop(0, i + 1, tile, 0)

    @pl.when(ranked)
    def _():
        earlier = (lax.broadcasted_iota(jnp.int32, (t, t), 1)
                   < lax.broadcasted_iota(jnp.int32, (t, t), 0))
        earlier = jnp.where(earlier, 1.0, 0.0).astype(jnp.bfloat16)

        def tile(j, seen):
            keys = pl.ds(pl.multiple_of(j * t, t), t)
            s = sct_ref[keys, :]
            eq = flag(s == thr)
            rank = seen + jnp.dot(earlier, eq.astype(jnp.bfloat16),
                                  preferred_element_type=jnp.float32)
            sc_ref[:, keys] = jnp.transpose(jnp.where(s > thr, 0.0, jnp.where(
                s == thr, jnp.where(rank < take, 0.0, NEG), NEG)))
            return seen + jnp.sum(eq, axis=0, keepdims=True)

        lax.fori_loop(0, i + 1, tile, jnp.zeros((1, t), jnp.float32))


def _dsa_kernel(iq_ref, ik_ref, misc_ref, q_ref, kt_ref, v_ref, o_ref, sct_ref, sc_ref,
                hi_ref, lo_ref, s_ref, m_ref, acc_ref):
    i = pl.program_id(1)

    @pl.when(pl.program_id(2) == 0)
    def _():
        _index_mask(iq_ref, ik_ref, misc_ref, sct_ref, sc_ref, hi_ref, lo_ref, i)

    def bias(s, keys, width, diagonal):
        mask = sc_ref[:, keys]
        return s + jnp.concatenate([mask, mask], axis=0)

    o_ref[0] = _attend(_stack_heads(q_ref[0]), kt_ref, v_ref, bias, i, DSA_WIDTHS,
                       s_ref, m_ref, acc_ref)


def _dsa_attention(iq, ik2, misc, aq, ak_t, av):
    b, s, _ = aq.shape
    t = ATT_TILE
    return pl.pallas_call(
        _dsa_kernel,
        grid=(b, s // t, N_HEAD_BLOCKS),
        in_specs=[
            pl.BlockSpec((1, t, N_IDX_HEADS * IDX_DIM), lambda bb, i, hp: (bb, i, 0)),
            pl.BlockSpec((1, s, LANES), lambda bb, i, hp: (bb, 0, 0)),
            pl.BlockSpec((1, t, LANES), lambda bb, i, hp: (bb, i, 0)),
            pl.BlockSpec((1, t, LANES), lambda bb, i, hp: (bb, i, hp)),
            pl.BlockSpec((1, LANES, s), lambda bb, i, hp: (bb, hp, 0)),
            pl.BlockSpec((1, s, HEADS_PER_BLOCK * LANES), lambda bb, i, hp: (bb, 0, hp)),
        ],
        out_specs=pl.BlockSpec((1, t, LANES), lambda bb, i, hp: (bb, i, hp)),
        out_shape=jax.ShapeDtypeStruct((b, s, WIDTH), jnp.float32),
        scratch_shapes=[pltpu.VMEM((s, t), jnp.float32),
                        pltpu.VMEM((t, s), jnp.float32),
                        pltpu.VMEM((SUBLANES, t), jnp.float32),
                        pltpu.VMEM((SUBLANES, t), jnp.float32)] + _attend_scratch(t, s),
        compiler_params=pltpu.CompilerParams(
            dimension_semantics=("arbitrary", "arbitrary", "arbitrary"),
            vmem_limit_bytes=VMEM_LIMIT),
        name="dsa_attention",
    )(iq, ik2, misc, aq, ak_t, av)


def _fox_kernel(q_ref, kt_ref, v_ref, cq_ref, ck_ref, o_ref, s_ref, m_ref, acc_ref):
    t = q_ref.shape[1]
    cq = jnp.concatenate([cq_ref[0, 0, :, 0:1], cq_ref[0, 0, :, 1:2]], axis=0)

    def bias(s, keys, width, diagonal):
        ck = ck_ref[0, 0, :, keys]
        ck = jnp.concatenate([jnp.broadcast_to(ck[0:1], (t, width)),
                              jnp.broadcast_to(ck[1:2], (t, width))], axis=0)
        s = s + (cq - ck)
        if diagonal:
            below = (lax.broadcasted_iota(jnp.int32, (t, t), 1)
                     <= lax.broadcasted_iota(jnp.int32, (t, t), 0))
            s = jnp.where(jnp.concatenate([below, below], axis=0), s, NEG)
        return s

    o_ref[0] = _attend(_stack_heads(q_ref[0]), kt_ref, v_ref, bias, pl.program_id(2),
                       FOX_WIDTHS, s_ref, m_ref, acc_ref)


def _fox_attention(fq, fk_t, fv, cum_q, cum_k):
    b, s, _ = fq.shape
    t = FOX_TILE
    return pl.pallas_call(
        _fox_kernel,
        grid=(b, N_HEAD_BLOCKS, s // t),
        in_specs=[
            pl.BlockSpec((1, t, LANES), lambda bb, hp, i: (bb, i, hp)),
            pl.BlockSpec((1, LANES, s), lambda bb, hp, i: (bb, hp, 0)),
            pl.BlockSpec((1, s, HEADS_PER_BLOCK * LANES), lambda bb, hp, i: (bb, 0, hp)),
            pl.BlockSpec((1, 1, t, HEADS_PER_BLOCK), lambda bb, hp, i: (bb, hp, i, 0)),
            pl.BlockSpec((1, 1, HEADS_PER_BLOCK, s), lambda bb, hp, i: (bb, hp, 0, 0)),
        ],
        out_specs=pl.BlockSpec((1, t, LANES), lambda bb, hp, i: (bb, i, hp)),
        out_shape=jax.ShapeDtypeStruct((b, s, WIDTH), jnp.float32),
        scratch_shapes=_attend_scratch(t, s),
        compiler_params=pltpu.CompilerParams(
            dimension_semantics=("arbitrary", "arbitrary", "arbitrary"),
            vmem_limit_bytes=VMEM_LIMIT),
        name="fox_attention",
    )(fq, fk_t, fv, cum_q, cum_k)


def _out_kernel(x_ref, ya_ref, yb_ref, gain_ref, wg_ref, bm_ref, wa_ref, wb_ref,
                wo_ref, ngain_ref, o_ref):
    x = x_ref[...]
    h = _rms(x, gain_ref[...]).astype(jnp.bfloat16)
    g = jnp.dot(h, wg_ref[...], preferred_element_type=jnp.float32)
    a_gate = g[:, :WIDTH]
    f_gate = g[:, WIDTH:2 * WIDTH]
    m_logit = g[:, 2 * WIDTH:] + bm_ref[...]
    za = (ya_ref[...] * (a_gate * jax.nn.sigmoid(a_gate))).astype(jnp.bfloat16)
    zb = (yb_ref[...] * (f_gate * jax.nn.sigmoid(f_gate))).astype(jnp.bfloat16)
    ua = jnp.dot(za, wa_ref[...], preferred_element_type=jnp.float32)
    ub = jnp.dot(zb, wb_ref[...], preferred_element_type=jnp.float32)
    gates = jax.nn.sigmoid(m_logit)
    merged = gates[:, :D_MODEL] * ua + gates[:, D_MODEL:] * ub
    y = x + jnp.dot(merged.astype(jnp.bfloat16), wo_ref[...],
                    preferred_element_type=jnp.float32)
    o_ref[...] = _rms(y, ngain_ref[...])


def _output(x2, ya, yb, gain, w_gates, b_merge, w_a, w_b, w_o, next_gain):
    m = x2.shape[0]
    tm = PROJ_ROWS
    row = lambda i: (i, 0)
    fixed = lambda i: (0, 0)
    return pl.pallas_call(
        _out_kernel,
        grid=(m // tm,),
        in_specs=[
            pl.BlockSpec((tm, D_MODEL), row),
            pl.BlockSpec((tm, WIDTH), row),
            pl.BlockSpec((tm, WIDTH), row),
            pl.BlockSpec((1, D_MODEL), fixed),
            pl.BlockSpec(w_gates.shape, fixed),
            pl.BlockSpec((1, 2 * D_MODEL), fixed),
            pl.BlockSpec(w_a.shape, fixed),
            pl.BlockSpec(w_b.shape, fixed),
            pl.BlockSpec(w_o.shape, fixed),
            pl.BlockSpec((1, D_MODEL), fixed),
        ],
        out_specs=pl.BlockSpec((tm, D_MODEL), row),
        out_shape=jax.ShapeDtypeStruct((m, D_MODEL), jnp.float32),
        compiler_params=pltpu.CompilerParams(
            dimension_semantics=("arbitrary",), vmem_limit_bytes=VMEM_LIMIT),
        name="out_proj",
    )(x2, ya, yb, gain, w_gates, b_merge, w_a, w_b, w_o, next_gain)


def _split_w_in(w):
    sizes = (WIDTH, WIDTH, WIDTH, WIDTH, N_IDX_HEADS * IDX_DIM, IDX_DIM, N_IDX_HEADS,
             WIDTH, WIDTH, WIDTH, WIDTH, N_HEADS, 2 * D_MODEL)
    points = np.cumsum(sizes)[:-1]
    return jnp.split(w, points, axis=-1)


def kernel(x, positions, norm_gain, w_in, b_forget, b_merge, w_branch_dsa,
           w_branch_fox, w_out, final_gain):
    b, s, d = x.shape
    depth = w_in.shape[0]
    bf16 = jnp.bfloat16

    half = jnp.arange(ROPE_HALF, dtype=jnp.float32)
    inv_freq = ROPE_THETA ** (-half * 2.0 / ROPE_DIM)
    dim = np.arange(LANES) % HEAD_DIM
    invf = jnp.where(dim < ROPE_DIM, inv_freq[dim % ROPE_HALF], 0.0)[None, :]
    pos2 = positions.reshape(b * s, 1)

    x2 = x.reshape(b * s, d)
    normed = None
    for l in range(depth):
        (w_aq, w_ak, w_av, w_ag, w_iq, w_ik, w_iw,
         w_fq, w_fk, w_fv, w_fg, w_fl, w_m) = _split_w_in(w_in[l])
        pad = jnp.zeros((d, LANES - N_IDX_HEADS - N_HEADS), w_in.dtype)
        w_all = jnp.concatenate(
            [w_aq, w_ak, w_av, w_fq, w_fk, w_fv, w_iq, w_ik, w_ik, w_iw, w_fl, pad],
            axis=1).astype(bf16)
        w_gates = jnp.concatenate([w_ag, w_fg, w_m], axis=1).astype(bf16)
        bf_row = jnp.zeros((1, LANES), jnp.float32).at[
            0, MISC_CUM:MISC_CUM + N_HEADS].set(b_forget[l])
        gain = norm_gain[l][None, :]
        next_gain = (norm_gain[l + 1] if l + 1 < depth else final_gain)[None, :]

        aq, ak_t, av, fq, fk_t, fv, iq, ik2, misc = _project(
            x2, pos2, gain, invf, w_all, bf_row, s)
        shape3 = lambda a: a.reshape(b, s, a.shape[-1])
        ya = _dsa_attention(shape3(iq), shape3(ik2), shape3(misc), shape3(aq), ak_t,
                            shape3(av))
        cum = misc[:, MISC_CUM:MISC_CUM + N_HEADS].reshape(
            b, s, N_HEAD_BLOCKS, HEADS_PER_BLOCK)
        cum_q = cum.transpose(0, 2, 1, 3)
        cum_k = cum.transpose(0, 2, 3, 1)
        yb = _fox_attention(shape3(fq), fk_t, shape3(fv), cum_q, cum_k)
        normed = _output(x2, ya.reshape(b * s, WIDTH), yb.reshape(b * s, WIDTH),
                         gain, w_gates, b_merge[l][None, :],
                         w_branch_dsa[l].astype(bf16), w_branch_fox[l].astype(bf16),
                         w_out[l].astype(bf16), next_gain)
        assert depth == 1, "stacked layers need the un-normalised residual as well"
    return normed.reshape(b, s, d)
```

```python
import functools

import jax
import jax.numpy as jnp
import numpy as np
from jax import lax
from jax.experimental import pallas as pl
from jax.experimental.pallas import tpu as pltpu

D_MODEL = 1024
HEAD_DIM = 64
N_HEADS = 8
WIDTH = N_HEADS * HEAD_DIM
N_IDX_HEADS = 4
IDX_DIM = 64
TOPK_MAX = 256
ROPE_THETA = 500000.0
ROPE_DIM = HEAD_DIM // 4
ROPE_HALF = ROPE_DIM // 2
RMS_EPS = 1e-6
NEG = -1e30
LOG2_E = float(np.log2(np.e))

LANES = 128
SUBLANES = 8
HEADS_PER_BLOCK = LANES // HEAD_DIM
N_HEAD_BLOCKS = N_HEADS // HEADS_PER_BLOCK
VMEM_LIMIT = 56 * 1024 * 1024

PROJ_ROWS = 512
ATT_TILE = 256
INDEX_WIDTHS = (2048, 1024, 512, 256)
MASK_WIDTHS = (1024, 512, 256)
DSA_WIDTHS = (2048, 1024, 512, 256)
FOX_TILE = 512
FOX_WIDTHS = (2048, 1024, 512)
COUNT_CHAINS = 4

TINY = float(np.finfo(np.float32).tiny)
ALL_THR = -3.0e38
BIG = 1.0e9
MAX_SEARCH = 400
INTERPOLATE_STEPS = 32
MIN_FRACTION = 1.0 / 32.0

MISC_IW = 0
MISC_CUM = N_IDX_HEADS


def _rms(x, gain):
    ms = jnp.mean(x * x, axis=-1, keepdims=True)
    return x * lax.rsqrt(ms + RMS_EPS) * gain


def _proj_kernel(x_ref, pos_ref, gain_ref, invf_ref, w_ref, bf_ref,
                 aq_ref, ak_ref, av_ref, fq_ref, fk_ref, fv_ref, iq_ref, ik_ref,
                 misc_ref, carry_ref, *, tiles_per_batch):
    tm = x_ref.shape[0]
    h = _rms(x_ref[...], gain_ref[...]).astype(jnp.bfloat16)

    ang = pos_ref[...].astype(jnp.float32) * invf_ref[...]
    cos = jnp.cos(ang)
    sin = jnp.sin(ang)
    d = lax.broadcasted_iota(jnp.int32, (tm, LANES), 1) % HEAD_DIM
    sin_lo = jnp.where(d < ROPE_HALF, -sin, 0.0)
    sin_hi = jnp.where(d >= ROPE_HALF, sin, 0.0)

    def rope(t):
        up = pltpu.roll(t, LANES - ROPE_HALF, 1)
        dn = pltpu.roll(t, ROPE_HALF, 1)
        return t * cos + up * sin_lo + dn * sin_hi

    def blocks_of(col0, width, use_rope, scale):
        r = jnp.dot(h, w_ref[:, col0:col0 + width],
                    preferred_element_type=jnp.float32)
        for c in range(width // LANES):
            t = r[:, c * LANES:(c + 1) * LANES]
            if use_rope:
                t = rope(t)
            if scale != 1.0:
                t = t * scale
            yield c, t

    def emit(out_ref, col0, width, use_rope, scale):
        for c, t in blocks_of(col0, width, use_rope, scale):
            out_ref[:, c * LANES:(c + 1) * LANES] = t.astype(out_ref.dtype)

    def emit_transposed(out_ref, col0, width, use_rope):
        for c, t in blocks_of(col0, width, use_rope, 1.0):
            out_ref[0, c * LANES:(c + 1) * LANES, :] = jnp.transpose(t).astype(
                out_ref.dtype)

    def emit_values(out_ref, col0):
        lane = lax.broadcasted_iota(jnp.int32, (tm, LANES), 1)
        tail = jnp.where(lane == HEAD_DIM, 1.0, 0.0)
        for c, t in blocks_of(col0, WIDTH, False, 1.0):
            pair = (t, pltpu.roll(t, HEAD_DIM, 1))
            for k in range(HEADS_PER_BLOCK):
                head = c * HEADS_PER_BLOCK + k
                out_ref[:, head * LANES:(head + 1) * LANES] = jnp.where(
                    lane < HEAD_DIM, pair[k], tail).astype(out_ref.dtype)

    qk_scale = HEAD_DIM ** -0.5 * LOG2_E
    emit(aq_ref, 0 * WIDTH, WIDTH, True, qk_scale)
    emit_transposed(ak_ref, 1 * WIDTH, WIDTH, True)
    emit_values(av_ref, 2 * WIDTH)
    emit(fq_ref, 3 * WIDTH, WIDTH, False, qk_scale)
    emit_transposed(fk_ref, 4 * WIDTH, WIDTH, False)
    emit_values(fv_ref, 5 * WIDTH)
    col = 6 * WIDTH
    emit(iq_ref, col, N_IDX_HEADS * IDX_DIM, True, IDX_DIM ** -0.5)
    col += N_IDX_HEADS * IDX_DIM
    emit(ik_ref, col, LANES, True, 1.0)
    col += LANES

    r = jnp.dot(h, w_ref[:, col:col + LANES], preferred_element_type=jnp.float32)
    z = r + bf_ref[...]
    logf = jnp.minimum(z, 0.0) - jnp.log1p(jnp.exp(-jnp.abs(z)))

    @pl.when(pl.program_id(0) % tiles_per_batch == 0)
    def _():
        carry_ref[...] = jnp.zeros_like(carry_ref)

    rows = lax.broadcasted_iota(jnp.int32, (tm, tm), 0)
    cols = lax.broadcasted_iota(jnp.int32, (tm, tm), 1)
    tri = jnp.where(cols <= rows, 1.0, 0.0).astype(jnp.bfloat16)
    hi = logf.astype(jnp.bfloat16)
    rem = logf - hi.astype(jnp.float32)
    mid = rem.astype(jnp.bfloat16)
    lo = (rem - mid.astype(jnp.float32)).astype(jnp.bfloat16)
    cum = (jnp.dot(tri, hi, preferred_element_type=jnp.float32)
           + jnp.dot(tri, mid, preferred_element_type=jnp.float32)
           + jnp.dot(tri, lo, preferred_element_type=jnp.float32)
           + carry_ref[...])
    carry_ref[...] = cum[tm - 1:tm, :]

    lane = lax.broadcasted_iota(jnp.int32, (tm, LANES), 1)
    misc = jnp.where(lane < MISC_CUM, r * (N_IDX_HEADS ** -0.5),
                     jnp.where(lane < MISC_CUM + N_HEADS, cum * LOG2_E, 0.0))
    misc_ref[...] = misc


def _project(x2, pos2, gain, invf, w_all, bf_row, seq):
    m = x2.shape[0]
    tm = PROJ_ROWS
    n_w = w_all.shape[1]
    row = lambda i: (i, 0)
    fixed = lambda i: (0, 0)
    bf16 = jnp.bfloat16
    tiles_per_batch = seq // tm
    batch = m // seq
    transposed = lambda i: (i // tiles_per_batch, 0, i % tiles_per_batch)

    def rows(width, dtype=bf16):
        return jax.ShapeDtypeStruct((m, width), dtype), pl.BlockSpec((tm, width), row)

    def keys(width):
        return (jax.ShapeDtypeStruct((batch, width, seq), bf16),
                pl.BlockSpec((1, width, tm), transposed))

    outs = [rows(WIDTH), keys(WIDTH), rows(N_HEADS * LANES),
            rows(WIDTH), keys(WIDTH), rows(N_HEADS * LANES),
            rows(N_IDX_HEADS * IDX_DIM), rows(LANES),
            rows(LANES, jnp.float32)]
    out_shape = [o[0] for o in outs]
    out_specs = [o[1] for o in outs]
    return pl.pallas_call(
        functools.partial(_proj_kernel, tiles_per_batch=tiles_per_batch),
        grid=(m // tm,),
        in_specs=[
            pl.BlockSpec((tm, D_MODEL), row),
            pl.BlockSpec((tm, 1), row),
            pl.BlockSpec((1, D_MODEL), fixed),
            pl.BlockSpec((1, LANES), fixed),
            pl.BlockSpec((D_MODEL, n_w), fixed),
            pl.BlockSpec((1, LANES), fixed),
        ],
        out_specs=out_specs,
        out_shape=out_shape,
        scratch_shapes=[pltpu.VMEM((1, LANES), jnp.float32)],
        compiler_params=pltpu.CompilerParams(
            dimension_semantics=("arbitrary",), vmem_limit_bytes=VMEM_LIMIT),
        name="in_proj",
    )(x2, pos2, gain, invf, w_all, bf_row)


def _half_select(x, first):
    lane = lax.broadcasted_iota(jnp.int32, x.shape, x.ndim - 1)
    keep = (lane < HEAD_DIM) if first else (lane >= HEAD_DIM)
    return jnp.where(keep, x, jnp.zeros_like(x))


def _stack_heads(q):
    return jnp.concatenate([_half_select(q, True), _half_select(q, False)], axis=0)


def _cover(limit, widths, fn, carry=0):
    widest = widths[0]
    n_widest = limit // widest

    def keys(start, width):
        return pl.ds(pl.multiple_of(start, width), width)

    carry = lax.fori_loop(
        0, n_widest, lambda j, c: fn(keys(j * widest, widest), widest, c), carry)
    start = n_widest * widest
    rest = limit - start
    for width in widths[1:]:
        part = rest & width
        carry = lax.cond(part != 0,
                         lambda c, start=start, width=width: fn(keys(start, width), width, c),
                         lambda c: c, carry)
        start = start + part
    return carry


def _attend(qs, kt_ref, v_ref, bias, i, widths, s_ref, m_ref, acc_ref):
    t = qs.shape[0] // HEADS_PER_BLOCK
    assert widths[-1] == t and all(a == 2 * b for a, b in zip(widths, widths[1:]))

    def phase_a(keys, width, diagonal):
        s = jnp.dot(qs, kt_ref[0, :, keys], preferred_element_type=jnp.float32)
        s = bias(s, keys, width, diagonal)
        s_ref[:, keys] = s
        m = m_ref[...]
        for c in range(width // LANES):
            m = jnp.maximum(m, s[:, c * LANES:(c + 1) * LANES])
        m_ref[...] = m

    def phase_b(keys, m):
        p = jnp.exp2(s_ref[:, keys] - m).astype(jnp.bfloat16)
        for k in range(HEADS_PER_BLOCK):
            acc_ref[k * t:(k + 1) * t, :] += jnp.dot(
                p[k * t:(k + 1) * t], v_ref[0, keys, k * LANES:(k + 1) * LANES],
                preferred_element_type=jnp.float32)

    def below_a(keys, width, carry):
        phase_a(keys, width, False)
        return carry

    diagonal = pl.ds(pl.multiple_of(i * t, t), t)
    m_ref[...] = jnp.full(m_ref.shape, NEG, jnp.float32)
    _cover(i * t, widths, below_a)
    phase_a(diagonal, t, True)

    m = jnp.max(m_ref[...], axis=1, keepdims=True)

    def below_b(keys, width, carry):
        phase_b(keys, m)
        return carry

    acc_ref[...] = jnp.zeros(acc_ref.shape, jnp.float32)
    _cover(i * t, widths, below_b)
    phase_b(diagonal, m)

    acc = acc_ref[...]
    o = acc / acc[:, HEAD_DIM:HEAD_DIM + 1]
    lane = lax.broadcasted_iota(jnp.int32, (t, LANES), 1)
    return jnp.where(lane < HEAD_DIM, o[:t], pltpu.roll(o[t:], HEAD_DIM, 1))


def _attend_scratch(t, s):
    rows = HEADS_PER_BLOCK * t
    return [pltpu.VMEM((rows, s), jnp.float32),
            pltpu.VMEM((rows, LANES), jnp.float32),
            pltpu.VMEM((rows, LANES), jnp.float32)]


def _index_mask(iq_ref, ik_ref, misc_ref, sct_ref, sc_ref, hi_ref, lo_ref, i):
    t = ATT_TILE
    top_k = float(TOPK_MAX)

    def fold(x, op, rows=SUBLANES):
        return op(x.reshape(x.shape[0] // rows, rows, t), axis=0)

    iq = iq_ref[0]
    qt = []
    for h in range(N_IDX_HEADS):
        block = _half_select(iq[:, (h // 2) * LANES:(h // 2 + 1) * LANES], h % 2 == 0)
        qt.append(jnp.transpose(block.astype(jnp.float32)).astype(jnp.bfloat16))
    misc_t = jnp.transpose(misc_ref[0])
    wh = [misc_t[MISC_IW + h:MISC_IW + h + 1, :] for h in range(N_IDX_HEADS)]

    def scores(keys):
        k = ik_ref[0, keys, :]
        sc = None
        for h in range(N_IDX_HEADS):
            rel = jnp.dot(k, qt[h], preferred_element_type=jnp.float32)
            term = wh[h] * jnp.maximum(rel, 0.0)
            sc = term if sc is None else sc + term
        return sc

    def track(for_max, for_min):
        hi_ref[...] = jnp.maximum(hi_ref[...], fold(for_max, jnp.max))
        lo_ref[...] = jnp.minimum(lo_ref[...], fold(for_min, jnp.min))

    def score_block(keys, width, carry):
        sc = scores(keys)
        sct_ref[keys, :] = sc
        track(sc, sc)
        return carry

    hi_ref[...] = jnp.full(hi_ref.shape, -jnp.inf, jnp.float32)
    lo_ref[...] = jnp.full(lo_ref.shape, jnp.inf, jnp.float32)
    _cover(i * t, INDEX_WIDTHS, score_block)
    diag = pl.ds(pl.multiple_of(i * t, t), t)
    sc = scores(diag)
    causal = (lax.broadcasted_iota(jnp.int32, (t, t), 0)
              <= lax.broadcasted_iota(jnp.int32, (t, t), 1))
    masked = jnp.where(causal, sc, -jnp.inf)
    sct_ref[diag, :] = masked
    track(masked, jnp.where(causal, sc, jnp.inf))

    def count_ge(thr):
        rows = COUNT_CHAINS * SUBLANES

        def block(keys, width, acc):
            for r in range(width // rows):
                part = pl.ds(pl.multiple_of(keys.start + r * rows, rows), rows)
                acc = acc + jnp.where(sct_ref[part, :] >= thr, 1.0, 0.0)
            return acc

        acc = _cover((i + 1) * t, INDEX_WIDTHS, block,
                     jnp.zeros((rows, t), jnp.float32))
        return jnp.sum(acc, axis=0, keepdims=True)

    def flag(cond):
        return jnp.where(cond, 1.0, 0.0)

    rmax = jnp.max(hi_ref[...], axis=0, keepdims=True)
    rmin = jnp.min(lo_ref[...], axis=0, keepdims=True)
    n_causal = (i * t + 1 + lax.broadcasted_iota(jnp.int32, (1, t), 1)
                ).astype(jnp.float32)
    c_pos = count_ge(jnp.full((1, t), TINY, jnp.float32))
    c_zero = lax.cond(jnp.min(c_pos) < top_k,
                      lambda: count_ge(jnp.zeros((1, t), jnp.float32)),
                      lambda: c_pos)
    take_all = n_causal <= top_k
    tie_zero = jnp.where(c_pos < top_k, flag(c_zero >= top_k), 0.0) > 0.5
    done = jnp.where(take_all, 1.0, jnp.where(tie_zero, 1.0, flag(c_pos == top_k)))
    thr = jnp.where(take_all, ALL_THR, jnp.where(tie_zero, 0.0, TINY))
    take = jnp.where(take_all, BIG, jnp.where(tie_zero, top_k - c_pos, BIG))
    positive = c_pos > top_k
    lo = jnp.where(positive, TINY, rmin)
    c_lo = jnp.where(positive, c_pos, n_causal)
    hi = jnp.where(positive, rmax * (1.0 + 2.0 ** -20) + TINY, 0.0)
    c_hi = jnp.where(positive, 0.0, c_zero)
    log_target = float(np.log(top_k + 0.5))

    def cond(state):
        it, done = state[0], state[-1]
        return (it < MAX_SEARCH) & (jnp.min(done) < 0.5)

    def step(state):
        it, lo, hi, c_lo, c_hi, w_lo, w_hi, last, thr, take, done = state
        f_lo = (jnp.log(c_lo) - log_target) * w_lo
        f_hi = (log_target - jnp.log(jnp.maximum(c_hi, 0.5))) * w_hi
        frac = f_lo / (f_lo + f_hi)
        frac = jnp.clip(frac, MIN_FRACTION, 1.0 - MIN_FRACTION)
        frac = jnp.where(it < INTERPOLATE_STEPS, frac, 0.5)
        guess = lo + (hi - lo) * frac
        halfway = 0.5 * lo + 0.5 * hi
        mid = jnp.where(guess <= lo, halfway, jnp.where(guess >= hi, halfway, guess))
        spent = jnp.where(mid <= lo, 1.0, flag(mid >= hi))
        c = count_ge(mid)
        finish = (1.0 - done) * jnp.maximum(spent, flag(c == top_k)) > 0.5
        thr = jnp.where(finish, jnp.where(spent > 0.5, lo, mid), thr)
        take = jnp.where(finish, jnp.where(spent > 0.5, top_k - c_hi, BIG), take)
        above = c > top_k
        below = c < top_k
        lo = jnp.where(above, mid, lo)
        c_lo = jnp.where(above, c, c_lo)
        hi = jnp.where(below, mid, hi)
        c_hi = jnp.where(below, c, c_hi)
        w_hi = jnp.where(above, jnp.where(last > 0.5, 0.5 * w_hi, 1.0),
                         jnp.where(below, 1.0, w_hi))
        w_lo = jnp.where(below, jnp.where(last < -0.5, 0.5 * w_lo, 1.0),
                         jnp.where(above, 1.0, w_lo))
        last = jnp.where(above, 1.0, jnp.where(below, -1.0, last))
        done = jnp.where(finish, 1.0, done)
        return it + 1, lo, hi, c_lo, c_hi, w_lo, w_hi, last, thr, take, done

    one = jnp.ones((1, t), jnp.float32)
    _, lo, _, _, c_hi, _, _, _, thr, take, done = lax.while_loop(
        cond, step, (jnp.int32(0), lo, hi, c_lo, c_hi, one, one, 0.0 * one,
                     thr, take, done))
    thr = jnp.where(done < 0.5, lo, thr)
    take = jnp.where(done < 0.5, top_k - c_hi, take)

    ranked = jnp.max(flag(take < BIG)) > 0.5

    @pl.when(jnp.logical_not(ranked))
    def _():
        def block(keys, width, carry):
            for u in range(width // t):
                part = pl.ds(pl.multiple_of(keys.start + u * t, t), t)
                sc_ref[:, part] = jnp.transpose(
                    jnp.where(sct_ref[part, :] >= thr, 0.0, NEG))
            return carry

        _cover((i + 1) * t, MASK_WIDTHS, block)

    @pl.when(ranked)
    def _():
        earlier = (lax.broadcasted_iota(jnp.int32, (t, t), 1)
                   < lax.broadcasted_iota(jnp.int32, (t, t), 0))
        earlier = jnp.where(earlier, 1.0, 0.0).astype(jnp.bfloat16)

        def tile(j, seen):
            keys = pl.ds(pl.multiple_of(j * t, t), t)
            s = sct_ref[keys, :]
            eq = flag(s == thr)
            rank = seen + jnp.dot(earlier, eq.astype(jnp.bfloat16),
                                  preferred_element_type=jnp.float32)
            sc_ref[:, keys] = jnp.transpose(jnp.where(s > thr, 0.0, jnp.where(
                s == thr, jnp.where(rank < take, 0.0, NEG), NEG)))
            return seen + jnp.sum(eq, axis=0, keepdims=True)

        lax.fori_loop(0, i + 1, tile, jnp.zeros((1, t), jnp.float32))


def _dsa_kernel(iq_ref, ik_ref, misc_ref, q_ref, kt_ref, v_ref, o_ref, sct_ref, sc_ref,
                hi_ref, lo_ref, s_ref, m_ref, acc_ref):
    i = pl.program_id(1)

    @pl.when(pl.program_id(2) == 0)
    def _():
        _index_mask(iq_ref, ik_ref, misc_ref, sct_ref, sc_ref, hi_ref, lo_ref, i)

    def bias(s, keys, width, diagonal):
        mask = sc_ref[:, keys]
        return s + jnp.concatenate([mask, mask], axis=0)

    o_ref[0] = _attend(_stack_heads(q_ref[0]), kt_ref, v_ref, bias, i, DSA_WIDTHS,
                       s_ref, m_ref, acc_ref)


def _dsa_attention(iq, ik2, misc, aq, ak_t, av):
    b, s, _ = aq.shape
    t = ATT_TILE
    return pl.pallas_call(
        _dsa_kernel,
        grid=(b, s // t, N_HEAD_BLOCKS),
        in_specs=[
            pl.BlockSpec((1, t, N_IDX_HEADS * IDX_DIM), lambda bb, i, hp: (bb, i, 0)),
            pl.BlockSpec((1, s, LANES), lambda bb, i, hp: (bb, 0, 0)),
            pl.BlockSpec((1, t, LANES), lambda bb, i, hp: (bb, i, 0)),
            pl.BlockSpec((1, t, LANES), lambda bb, i, hp: (bb, i, hp)),
            pl.BlockSpec((1, LANES, s), lambda bb, i, hp: (bb, hp, 0)),
            pl.BlockSpec((1, s, HEADS_PER_BLOCK * LANES), lambda bb, i, hp: (bb, 0, hp)),
        ],
        out_specs=pl.BlockSpec((1, t, LANES), lambda bb, i, hp: (bb, i, hp)),
        out_shape=jax.ShapeDtypeStruct((b, s, WIDTH), jnp.float32),
        scratch_shapes=[pltpu.VMEM((s, t), jnp.float32),
                        pltpu.VMEM((t, s), jnp.float32),
                        pltpu.VMEM((SUBLANES, t), jnp.float32),
                        pltpu.VMEM((SUBLANES, t), jnp.float32)] + _attend_scratch(t, s),
        compiler_params=pltpu.CompilerParams(
            dimension_semantics=("arbitrary", "arbitrary", "arbitrary"),
            vmem_limit_bytes=VMEM_LIMIT),
        name="dsa_attention",
    )(iq, ik2, misc, aq, ak_t, av)


def _fox_kernel(q_ref, kt_ref, v_ref, cq_ref, ck_ref, o_ref, s_ref, m_ref, acc_ref):
    t = q_ref.shape[1]
    cq = jnp.concatenate([cq_ref[0, 0, :, 0:1], cq_ref[0, 0, :, 1:2]], axis=0)

    def bias(s, keys, width, diagonal):
        ck = ck_ref[0, 0, :, keys]
        ck = jnp.concatenate([jnp.broadcast_to(ck[0:1], (t, width)),
                              jnp.broadcast_to(ck[1:2], (t, width))], axis=0)
        s = s + (cq - ck)
        if diagonal:
            below = (lax.broadcasted_iota(jnp.int32, (t, t), 1)
                     <= lax.broadcasted_iota(jnp.int32, (t, t), 0))
            s = jnp.where(jnp.concatenate([below, below], axis=0), s, NEG)
        return s

    o_ref[0] = _attend(_stack_heads(q_ref[0]), kt_ref, v_ref, bias, pl.program_id(2),
                       FOX_WIDTHS, s_ref, m_ref, acc_ref)


def _fox_attention(fq, fk_t, fv, cum_q, cum_k):
    b, s, _ = fq.shape
    t = FOX_TILE
    return pl.pallas_call(
        _fox_kernel,
        grid=(b, N_HEAD_BLOCKS, s // t),
        in_specs=[
            pl.BlockSpec((1, t, LANES), lambda bb, hp, i: (bb, i, hp)),
            pl.BlockSpec((1, LANES, s), lambda bb, hp, i: (bb, hp, 0)),
            pl.BlockSpec((1, s, HEADS_PER_BLOCK * LANES), lambda bb, hp, i: (bb, 0, hp)),
            pl.BlockSpec((1, 1, t, HEADS_PER_BLOCK), lambda bb, hp, i: (bb, hp, i, 0)),
            pl.BlockSpec((1, 1, HEADS_PER_BLOCK, s), lambda bb, hp, i: (bb, hp, 0, 0)),
        ],
        out_specs=pl.BlockSpec((1, t, LANES), lambda bb, hp, i: (bb, i, hp)),
        out_shape=jax.ShapeDtypeStruct((b, s, WIDTH), jnp.float32),
        scratch_shapes=_attend_scratch(t, s),
        compiler_params=pltpu.CompilerParams(
            dimension_semantics=("arbitrary", "arbitrary", "arbitrary"),
            vmem_limit_bytes=VMEM_LIMIT),
        name="fox_attention",
    )(fq, fk_t, fv, cum_q, cum_k)


def _out_kernel(x_ref, ya_ref, yb_ref, gain_ref, wg_ref, bm_ref, wa_ref, wb_ref,
                wo_ref, ngain_ref, o_ref):
    x = x_ref[...]
    h = _rms(x, gain_ref[...]).astype(jnp.bfloat16)
    g = jnp.dot(h, wg_ref[...], preferred_element_type=jnp.float32)
    a_gate = g[:, :WIDTH]
    f_gate = g[:, WIDTH:2 * WIDTH]
    m_logit = g[:, 2 * WIDTH:] + bm_ref[...]
    za = (ya_ref[...] * (a_gate * jax.nn.sigmoid(a_gate))).astype(jnp.bfloat16)
    zb = (yb_ref[...] * (f_gate * jax.nn.sigmoid(f_gate))).astype(jnp.bfloat16)
    ua = jnp.dot(za, wa_ref[...], preferred_element_type=jnp.float32)
    ub = jnp.dot(zb, wb_ref[...], preferred_element_type=jnp.float32)
    gates = jax.nn.sigmoid(m_logit)
    merged = gates[:, :D_MODEL] * ua + gates[:, D_MODEL:] * ub
    y = x + jnp.dot(merged.astype(jnp.bfloat16), wo_ref[...],
                    preferred_element_type=jnp.float32)
    o_ref[...] = _rms(y, ngain_ref[...])


def _output(x2, ya, yb, gain, w_gates, b_merge, w_a, w_b, w_o, next_gain):
    m = x2.shape[0]
    tm = PROJ_ROWS
    row = lambda i: (i, 0)
    fixed = lambda i: (0, 0)
    return pl.pallas_call(
        _out_kernel,
        grid=(m // tm,),
        in_specs=[
            pl.BlockSpec((tm, D_MODEL), row),
            pl.BlockSpec((tm, WIDTH), row),
            pl.BlockSpec((tm, WIDTH), row),
            pl.BlockSpec((1, D_MODEL), fixed),
            pl.BlockSpec(w_gates.shape, fixed),
            pl.BlockSpec((1, 2 * D_MODEL), fixed),
            pl.BlockSpec(w_a.shape, fixed),
            pl.BlockSpec(w_b.shape, fixed),
            pl.BlockSpec(w_o.shape, fixed),
            pl.BlockSpec((1, D_MODEL), fixed),
        ],
        out_specs=pl.BlockSpec((tm, D_MODEL), row),
        out_shape=jax.ShapeDtypeStruct((m, D_MODEL), jnp.float32),
        compiler_params=pltpu.CompilerParams(
            dimension_semantics=("arbitrary",), vmem_limit_bytes=VMEM_LIMIT),
        name="out_proj",
    )(x2, ya, yb, gain, w_gates, b_merge, w_a, w_b, w_o, next_gain)


def _split_w_in(w):
    sizes = (WIDTH, WIDTH, WIDTH, WIDTH, N_IDX_HEADS * IDX_DIM, IDX_DIM, N_IDX_HEADS,
             WIDTH, WIDTH, WIDTH, WIDTH, N_HEADS, 2 * D_MODEL)
    points = np.cumsum(sizes)[:-1]
    return jnp.split(w, points, axis=-1)


def kernel(x, positions, norm_gain, w_in, b_forget, b_merge, w_branch_dsa,
           w_branch_fox, w_out, final_gain):
    b, s, d = x.shape
    depth = w_in.shape[0]
    bf16 = jnp.bfloat16

    half = jnp.arange(ROPE_HALF, dtype=jnp.float32)
    inv_freq = ROPE_THETA ** (-half * 2.0 / ROPE_DIM)
    dim = np.arange(LANES) % HEAD_DIM
    invf = jnp.where(dim < ROPE_DIM, inv_freq[dim % ROPE_HALF], 0.0)[None, :]
    pos2 = positions.reshape(b * s, 1)

    x2 = x.reshape(b * s, d)
    normed = None
    for l in range(depth):
        (w_aq, w_ak, w_av, w_ag, w_iq, w_ik, w_iw,
         w_fq, w_fk, w_fv, w_fg, w_fl, w_m) = _split_w_in(w_in[l])
        pad = jnp.zeros((d, LANES - N_IDX_HEADS - N_HEADS), w_in.dtype)
        w_all = jnp.concatenate(
            [w_aq, w_ak, w_av, w_fq, w_fk, w_fv, w_iq, w_ik, w_ik, w_iw, w_fl, pad],
            axis=1).astype(bf16)
        w_gates = jnp.concatenate([w_ag, w_fg, w_m], axis=1).astype(bf16)
        bf_row = jnp.zeros((1, LANES), jnp.float32).at[
            0, MISC_CUM:MISC_CUM + N_HEADS].set(b_forget[l])
        gain = norm_gain[l][None, :]
        next_gain = (norm_gain[l + 1] if l + 1 < depth else final_gain)[None, :]

        aq, ak_t, av, fq, fk_t, fv, iq, ik2, misc = _project(
            x2, pos2, gain, invf, w_all, bf_row, s)
        shape3 = lambda a: a.reshape(b, s, a.shape[-1])
        ya = _dsa_attention(shape3(iq), shape3(ik2), shape3(misc), shape3(aq), ak_t,
                            shape3(av))
        cum = misc[:, MISC_CUM:MISC_CUM + N_HEADS].reshape(
            b, s, N_HEAD_BLOCKS, HEADS_PER_BLOCK)
        cum_q = cum.transpose(0, 2, 1, 3)
        cum_k = cum.transpose(0, 2, 3, 1)
        yb = _fox_attention(shape3(fq), fk_t, shape3(fv), cum_q, cum_k)
        normed = _output(x2, ya.reshape(b * s, WIDTH), yb.reshape(b * s, WIDTH),
                         gain, w_gates, b_merge[l][None, :],
                         w_branch_dsa[l].astype(bf16), w_branch_fox[l].astype(bf16),
                         w_out[l].astype(bf16), next_gain)
        assert depth == 1, "stacked layers need the un-normalised residual as well"
    return normed.reshape(b, s, d)
```

```python
import functools

import jax
import jax.numpy as jnp
import numpy as np
from jax import lax
from jax.experimental import pallas as pl
from jax.experimental.pallas import tpu as pltpu

D_MODEL = 1024
HEAD_DIM = 64
N_HEADS = 8
WIDTH = N_HEADS * HEAD_DIM
N_IDX_HEADS = 4
IDX_DIM = 64
TOPK_MAX = 256
ROPE_THETA = 500000.0
ROPE_DIM = HEAD_DIM // 4
ROPE_HALF = ROPE_DIM // 2
RMS_EPS = 1e-6
NEG = -1e30
LOG2_E = float(np.log2(np.e))

LANES = 128
SUBLANES = 8
HEADS_PER_BLOCK = LANES // HEAD_DIM
N_HEAD_BLOCKS = N_HEADS // HEADS_PER_BLOCK
VMEM_LIMIT = 56 * 1024 * 1024

PROJ_ROWS = 512
ATT_TILE = 256
INDEX_WIDTHS = (2048, 1024, 512, 256)
MASK_WIDTHS = (1024, 512, 256)
DSA_WIDTHS = (4096, 2048, 1024, 512, 256)
FOX_TILE = 512
FOX_WIDTHS = (4096, 2048, 1024, 512)
COUNT_CHAINS = 4

TINY = float(np.finfo(np.float32).tiny)
ALL_THR = -3.0e38
BIG = 1.0e9
MAX_SEARCH = 400
INTERPOLATE_STEPS = 32
MIN_FRACTION = 1.0 / 32.0

MISC_IW = 0
MISC_CUM = N_IDX_HEADS


def _rms(x, gain):
    ms = jnp.mean(x * x, axis=-1, keepdims=True)
    return x * lax.rsqrt(ms + RMS_EPS) * gain


def _proj_kernel(x_ref, pos_ref, gain_ref, invf_ref, w_ref, bf_ref,
                 aq_ref, ak_ref, av_ref, fq_ref, fk_ref, fv_ref, iq_ref, ik_ref,
                 misc_ref, carry_ref, *, tiles_per_batch):
    tm = x_ref.shape[0]
    h = _rms(x_ref[...], gain_ref[...]).astype(jnp.bfloat16)

    ang = pos_ref[...].astype(jnp.float32) * invf_ref[...]
    cos = jnp.cos(ang)
    sin = jnp.sin(ang)
    d = lax.broadcasted_iota(jnp.int32, (tm, LANES), 1) % HEAD_DIM
    sin_lo = jnp.where(d < ROPE_HALF, -sin, 0.0)
    sin_hi = jnp.where(d >= ROPE_HALF, sin, 0.0)

    def rope(t):
        up = pltpu.roll(t, LANES - ROPE_HALF, 1)
        dn = pltpu.roll(t, ROPE_HALF, 1)
        return t * cos + up * sin_lo + dn * sin_hi

    def blocks_of(col0, width, use_rope, scale):
        r = jnp.dot(h, w_ref[:, col0:col0 + width],
                    preferred_element_type=jnp.float32)
        for c in range(width // LANES):
            t = r[:, c * LANES:(c + 1) * LANES]
            if use_rope:
                t = rope(t)
            if scale != 1.0:
                t = t * scale
            yield c, t

    def emit(out_ref, col0, width, use_rope, scale):
        for c, t in blocks_of(col0, width, use_rope, scale):
            out_ref[:, c * LANES:(c + 1) * LANES] = t.astype(out_ref.dtype)

    def emit_transposed(out_ref, col0, width, use_rope):
        for c, t in blocks_of(col0, width, use_rope, 1.0):
            out_ref[0, c * LANES:(c + 1) * LANES, :] = jnp.transpose(t).astype(
                out_ref.dtype)

    def emit_values(out_ref, col0):
        lane = lax.broadcasted_iota(jnp.int32, (tm, LANES), 1)
        tail = jnp.where(lane == HEAD_DIM, 1.0, 0.0)
        for c, t in blocks_of(col0, WIDTH, False, 1.0):
            pair = (t, pltpu.roll(t, HEAD_DIM, 1))
            for k in range(HEADS_PER_BLOCK):
                head = c * HEADS_PER_BLOCK + k
                out_ref[:, head * LANES:(head + 1) * LANES] = jnp.where(
                    lane < HEAD_DIM, pair[k], tail).astype(out_ref.dtype)

    qk_scale = HEAD_DIM ** -0.5 * LOG2_E
    emit(aq_ref, 0 * WIDTH, WIDTH, True, qk_scale)
    emit_transposed(ak_ref, 1 * WIDTH, WIDTH, True)
    emit_values(av_ref, 2 * WIDTH)
    emit(fq_ref, 3 * WIDTH, WIDTH, False, qk_scale)
    emit_transposed(fk_ref, 4 * WIDTH, WIDTH, False)
    emit_values(fv_ref, 5 * WIDTH)
    col = 6 * WIDTH
    emit(iq_ref, col, N_IDX_HEADS * IDX_DIM, True, IDX_DIM ** -0.5)
    col += N_IDX_HEADS * IDX_DIM
    emit(ik_ref, col, LANES, True, 1.0)
    col += LANES

    r = jnp.dot(h, w_ref[:, col:col + LANES], preferred_element_type=jnp.float32)
    z = r + bf_ref[...]
    logf = jnp.minimum(z, 0.0) - jnp.log1p(jnp.exp(-jnp.abs(z)))

    @pl.when(pl.program_id(0) % tiles_per_batch == 0)
    def _():
        carry_ref[...] = jnp.zeros_like(carry_ref)

    rows = lax.broadcasted_iota(jnp.int32, (tm, tm), 0)
    cols = lax.broadcasted_iota(jnp.int32, (tm, tm), 1)
    tri = jnp.where(cols <= rows, 1.0, 0.0).astype(jnp.bfloat16)
    hi = logf.astype(jnp.bfloat16)
    rem = logf - hi.astype(jnp.float32)
    mid = rem.astype(jnp.bfloat16)
    lo = (rem - mid.astype(jnp.float32)).astype(jnp.bfloat16)
    cum = (jnp.dot(tri, hi, preferred_element_type=jnp.float32)
           + jnp.dot(tri, mid, preferred_element_type=jnp.float32)
           + jnp.dot(tri, lo, preferred_element_type=jnp.float32)
           + carry_ref[...])
    carry_ref[...] = cum[tm - 1:tm, :]

    lane = lax.broadcasted_iota(jnp.int32, (tm, LANES), 1)
    misc = jnp.where(lane < MISC_CUM, r * (N_IDX_HEADS ** -0.5),
                     jnp.where(lane < MISC_CUM + N_HEADS, cum * LOG2_E, 0.0))
    misc_ref[...] = misc


def _project(x2, pos2, gain, invf, w_all, bf_row, seq):
    m = x2.shape[0]
    tm = PROJ_ROWS
    n_w = w_all.shape[1]
    row = lambda i: (i, 0)
    fixed = lambda i: (0, 0)
    bf16 = jnp.bfloat16
    tiles_per_batch = seq // tm
    batch = m // seq
    transposed = lambda i: (i // tiles_per_batch, 0, i % tiles_per_batch)

    def rows(width, dtype=bf16):
        return jax.ShapeDtypeStruct((m, width), dtype), pl.BlockSpec((tm, width), row)

    def keys(width):
        return (jax.ShapeDtypeStruct((batch, width, seq), bf16),
                pl.BlockSpec((1, width, tm), transposed))

    outs = [rows(WIDTH), keys(WIDTH), rows(N_HEADS * LANES),
            rows(WIDTH), keys(WIDTH), rows(N_HEADS * LANES),
            rows(N_IDX_HEADS * IDX_DIM), rows(LANES),
            rows(LANES, jnp.float32)]
    out_shape = [o[0] for o in outs]
    out_specs = [o[1] for o in outs]
    return pl.pallas_call(
        functools.partial(_proj_kernel, tiles_per_batch=tiles_per_batch),
        grid=(m // tm,),
        in_specs=[
            pl.BlockSpec((tm, D_MODEL), row),
            pl.BlockSpec((tm, 1), row),
            pl.BlockSpec((1, D_MODEL), fixed),
            pl.BlockSpec((1, LANES), fixed),
            pl.BlockSpec((D_MODEL, n_w), fixed),
            pl.BlockSpec((1, LANES), fixed),
        ],
        out_specs=out_specs,
        out_shape=out_shape,
        scratch_shapes=[pltpu.VMEM((1, LANES), jnp.float32)],
        compiler_params=pltpu.CompilerParams(
            dimension_semantics=("arbitrary",), vmem_limit_bytes=VMEM_LIMIT),
        name="in_proj",
    )(x2, pos2, gain, invf, w_all, bf_row)


def _half_select(x, first):
    lane = lax.broadcasted_iota(jnp.int32, x.shape, x.ndim - 1)
    keep = (lane < HEAD_DIM) if first else (lane >= HEAD_DIM)
    return jnp.where(keep, x, jnp.zeros_like(x))


def _stack_heads(q):
    return jnp.concatenate([_half_select(q, True), _half_select(q, False)], axis=0)


def _cover(limit, widths, fn, carry=0):
    widest = widths[0]
    n_widest = limit // widest

    def keys(start, width):
        return pl.ds(pl.multiple_of(start, width), width)

    carry = lax.fori_loop(
        0, n_widest, lambda j, c: fn(keys(j * widest, widest), widest, c), carry)
    start = n_widest * widest
    rest = limit - start
    for width in widths[1:]:
        part = rest & width
        carry = lax.cond(part != 0,
                         lambda c, start=start, width=width: fn(keys(start, width), width, c),
                         lambda c: c, carry)
        start = start + part
    return carry


def _attend(qs, kt_ref, v_ref, bias, i, widths, s_ref, m_ref, acc_ref):
    t = qs.shape[0] // HEADS_PER_BLOCK
    assert widths[-1] == t and all(a == 2 * b for a, b in zip(widths, widths[1:]))

    def phase_a(keys, width, diagonal):
        s = jnp.dot(qs, kt_ref[0, :, keys], preferred_element_type=jnp.float32)
        s = bias(s, keys, width, diagonal)
        s_ref[:, keys] = s
        m = m_ref[...]
        for c in range(width // LANES):
            m = jnp.maximum(m, s[:, c * LANES:(c + 1) * LANES])
        m_ref[...] = m

    def phase_b(keys, m):
        p = jnp.exp2(s_ref[:, keys] - m).astype(jnp.bfloat16)
        for k in range(HEADS_PER_BLOCK):
            acc_ref[k * t:(k + 1) * t, :] += jnp.dot(
                p[k * t:(k + 1) * t], v_ref[0, keys, k * LANES:(k + 1) * LANES],
                preferred_element_type=jnp.float32)

    def below_a(keys, width, carry):
        phase_a(keys, width, False)
        return carry

    diagonal = pl.ds(pl.multiple_of(i * t, t), t)
    m_ref[...] = jnp.full(m_ref.shape, NEG, jnp.float32)
    _cover(i * t, widths, below_a)
    phase_a(diagonal, t, True)

    m = jnp.max(m_ref[...], axis=1, keepdims=True)

    def below_b(keys, width, carry):
        phase_b(keys, m)
        return carry

    acc_ref[...] = jnp.zeros(acc_ref.shape, jnp.float32)
    _cover(i * t, widths, below_b)
    phase_b(diagonal, m)

    acc = acc_ref[...]
    o = acc / acc[:, HEAD_DIM:HEAD_DIM + 1]
    lane = lax.broadcasted_iota(jnp.int32, (t, LANES), 1)
    return jnp.where(lane < HEAD_DIM, o[:t], pltpu.roll(o[t:], HEAD_DIM, 1))


def _attend_scratch(t, s):
    rows = HEADS_PER_BLOCK * t
    return [pltpu.VMEM((rows, s), jnp.float32),
            pltpu.VMEM((rows, LANES), jnp.float32),
            pltpu.VMEM((rows, LANES), jnp.float32)]


def _index_mask(iq_ref, ik_ref, misc_ref, sct_ref, sc_ref, hi_ref, lo_ref, i):
    t = ATT_TILE
    top_k = float(TOPK_MAX)

    def fold(x, op, rows=SUBLANES):
        return op(x.reshape(x.shape[0] // rows, rows, t), axis=0)

    iq = iq_ref[0]
    qt = []
    for h in range(N_IDX_HEADS):
        block = _half_select(iq[:, (h // 2) * LANES:(h // 2 + 1) * LANES], h % 2 == 0)
        qt.append(jnp.transpose(block.astype(jnp.float32)).astype(jnp.bfloat16))
    misc_t = jnp.transpose(misc_ref[0])
    wh = [misc_t[MISC_IW + h:MISC_IW + h + 1, :] for h in range(N_IDX_HEADS)]

    def scores(keys):
        k = ik_ref[0, keys, :]
        sc = None
        for h in range(N_IDX_HEADS):
            rel = jnp.dot(k, qt[h], preferred_element_type=jnp.float32)
            term = wh[h] * jnp.maximum(rel, 0.0)
            sc = term if sc is None else sc + term
        return sc

    def track(for_max, for_min):
        hi_ref[...] = jnp.maximum(hi_ref[...], fold(for_max, jnp.max))
        lo_ref[...] = jnp.minimum(lo_ref[...], fold(for_min, jnp.min))

    def score_block(keys, width, carry):
        sc = scores(keys)
        sct_ref[keys, :] = sc
        track(sc, sc)
        return carry

    hi_ref[...] = jnp.full(hi_ref.shape, -jnp.inf, jnp.float32)
    lo_ref[...] = jnp.full(lo_ref.shape, jnp.inf, jnp.float32)
    _cover(i * t, INDEX_WIDTHS, score_block)
    diag = pl.ds(pl.multiple_of(i * t, t), t)
    sc = scores(diag)
    causal = (lax.broadcasted_iota(jnp.int32, (t, t), 0)
              <= lax.broadcasted_iota(jnp.int32, (t, t), 1))
    masked = jnp.where(causal, sc, -jnp.inf)
    sct_ref[diag, :] = masked
    track(masked, jnp.where(causal, sc, jnp.inf))

    def count_ge(thr):
        rows = COUNT_CHAINS * SUBLANES

        def block(keys, width, acc):
            for r in range(width // rows):
                part = pl.ds(pl.multiple_of(keys.start + r * rows, rows), rows)
                acc = acc + jnp.where(sct_ref[part, :] >= thr, 1.0, 0.0)
            return acc

        acc = _cover((i + 1) * t, INDEX_WIDTHS, block,
                     jnp.zeros((rows, t), jnp.float32))
        return jnp.sum(acc, axis=0, keepdims=True)

    def flag(cond):
        return jnp.where(cond, 1.0, 0.0)

    rmax = jnp.max(hi_ref[...], axis=0, keepdims=True)
    rmin = jnp.min(lo_ref[...], axis=0, keepdims=True)
    n_causal = (i * t + 1 + lax.broadcasted_iota(jnp.int32, (1, t), 1)
                ).astype(jnp.float32)
    c_pos = count_ge(jnp.full((1, t), TINY, jnp.float32))
    c_zero = lax.cond(jnp.min(c_pos) < top_k,
                      lambda: count_ge(jnp.zeros((1, t), jnp.float32)),
                      lambda: c_pos)
    take_all = n_causal <= top_k
    tie_zero = jnp.where(c_pos < top_k, flag(c_zero >= top_k), 0.0) > 0.5
    done = jnp.where(take_all, 1.0, jnp.where(tie_zero, 1.0, flag(c_pos == top_k)))
    thr = jnp.where(take_all, ALL_THR, jnp.where(tie_zero, 0.0, TINY))
    take = jnp.where(take_all, BIG, jnp.where(tie_zero, top_k - c_pos, BIG))
    positive = c_pos > top_k
    lo = jnp.where(positive, TINY, rmin)
    c_lo = jnp.where(positive, c_pos, n_causal)
    hi = jnp.where(positive, rmax * (1.0 + 2.0 ** -20) + TINY, 0.0)
    c_hi = jnp.where(positive, 0.0, c_zero)
    log_target = float(np.log(top_k + 0.5))

    def cond(state):
        it, done = state[0], state[-1]
        return (it < MAX_SEARCH) & (jnp.min(done) < 0.5)

    def step(state):
        it, lo, hi, c_lo, c_hi, w_lo, w_hi, last, thr, take, done = state
        f_lo = (jnp.log(c_lo) - log_target) * w_lo
        f_hi = (log_target - jnp.log(jnp.maximum(c_hi, 0.5))) * w_hi
        frac = f_lo / (f_lo + f_hi)
        frac = jnp.clip(frac, MIN_FRACTION, 1.0 - MIN_FRACTION)
        frac = jnp.where(it < INTERPOLATE_STEPS, frac, 0.5)
        guess = lo + (hi - lo) * frac
        halfway = 0.5 * lo + 0.5 * hi
        mid = jnp.where(guess <= lo, halfway, jnp.where(guess >= hi, halfway, guess))
        spent = jnp.where(mid <= lo, 1.0, flag(mid >= hi))
        c = count_ge(mid)
        finish = (1.0 - done) * jnp.maximum(spent, flag(c == top_k)) > 0.5
        thr = jnp.where(finish, jnp.where(spent > 0.5, lo, mid), thr)
        take = jnp.where(finish, jnp.where(spent > 0.5, top_k - c_hi, BIG), take)
        above = c > top_k
        below = c < top_k
        lo = jnp.where(above, mid, lo)
        c_lo = jnp.where(above, c, c_lo)
        hi = jnp.where(below, mid, hi)
        c_hi = jnp.where(below, c, c_hi)
        w_hi = jnp.where(above, jnp.where(last > 0.5, 0.5 * w_hi, 1.0),
                         jnp.where(below, 1.0, w_hi))
        w_lo = jnp.where(below, jnp.where(last < -0.5, 0.5 * w_lo, 1.0),
                         jnp.where(above, 1.0, w_lo))
        last = jnp.where(above, 1.0, jnp.where(below, -1.0, last))
        done = jnp.where(finish, 1.0, done)
        return it + 1, lo, hi, c_lo, c_hi, w_lo, w_hi, last, thr, take, done

    one = jnp.ones((1, t), jnp.float32)
    _, lo, _, _, c_hi, _, _, _, thr, take, done = lax.while_loop(
        cond, step, (jnp.int32(0), lo, hi, c_lo, c_hi, one, one, 0.0 * one,
                     thr, take, done))
    thr = jnp.where(done < 0.5, lo, thr)
    take = jnp.where(done < 0.5, top_k - c_hi, take)

    ranked = jnp.max(flag(take < BIG)) > 0.5

    @pl.when(jnp.logical_not(ranked))
    def _():
        def block(keys, width, carry):
            for u in range(width // t):
                part = pl.ds(pl.multiple_of(keys.start + u * t, t), t)
                sc_ref[:, part] = jnp.transpose(
                    jnp.where(sct_ref[part, :] >= thr, 0.0, NEG))
            return carry

        _cover((i + 1) * t, MASK_WIDTHS, block)

    @pl.when(ranked)
    def _():
        earlier = (lax.broadcasted_iota(jnp.int32, (t, t), 1)
                   < lax.broadcasted_iota(jnp.int32, (t, t), 0))
        earlier = jnp.where(earlier, 1.0, 0.0).astype(jnp.bfloat16)

        def tile(j, seen):
            keys = pl.ds(pl.multiple_of(j * t, t), t)
            s = sct_ref[keys, :]
            eq = flag(s == thr)
            rank = seen + jnp.dot(earlier, eq.astype(jnp.bfloat16),
                                  preferred_element_type=jnp.float32)
            sc_ref[:, keys] = jnp.transpose(jnp.where(s > thr, 0.0, jnp.where(
                s == thr, jnp.where(rank < take, 0.0, NEG), NEG)))
            return seen + jnp.sum(eq, axis=0, keepdims=True)

        lax.fori_loop(0, i + 1, tile, jnp.zeros((1, t), jnp.float32))


def _dsa_kernel(iq_ref, ik_ref, misc_ref, q_ref, kt_ref, v_ref, o_ref, sct_ref, sc_ref,
                hi_ref, lo_ref, s_ref, m_ref, acc_ref):
    i = pl.program_id(1)

    @pl.when(pl.program_id(2) == 0)
    def _():
        _index_mask(iq_ref, ik_ref, misc_ref, sct_ref, sc_ref, hi_ref, lo_ref, i)

    def bias(s, keys, width, diagonal):
        mask = sc_ref[:, keys]
        return s + jnp.concatenate([mask, mask], axis=0)

    o_ref[0] = _attend(_stack_heads(q_ref[0]), kt_ref, v_ref, bias, i, DSA_WIDTHS,
                       s_ref, m_ref, acc_ref)


def _dsa_attention(iq, ik2, misc, aq, ak_t, av):
    b, s, _ = aq.shape
    t = ATT_TILE
    return pl.pallas_call(
        _dsa_kernel,
        grid=(b, s // t, N_HEAD_BLOCKS),
        in_specs=[
            pl.BlockSpec((1, t, N_IDX_HEADS * IDX_DIM), lambda bb, i, hp: (bb, i, 0)),
            pl.BlockSpec((1, s, LANES), lambda bb, i, hp: (bb, 0, 0)),
            pl.BlockSpec((1, t, LANES), lambda bb, i, hp: (bb, i, 0)),
            pl.BlockSpec((1, t, LANES), lambda bb, i, hp: (bb, i, hp)),
            pl.BlockSpec((1, LANES, s), lambda bb, i, hp: (bb, hp, 0)),
            pl.BlockSpec((1, s, HEADS_PER_BLOCK * LANES), lambda bb, i, hp: (bb, 0, hp)),
        ],
        out_specs=pl.BlockSpec((1, t, LANES), lambda bb, i, hp: (bb, i, hp)),
        out_shape=jax.ShapeDtypeStruct((b, s, WIDTH), jnp.float32),
        scratch_shapes=[pltpu.VMEM((s, t), jnp.float32),
                        pltpu.VMEM((t, s), jnp.float32),
                        pltpu.VMEM((SUBLANES, t), jnp.float32),
                        pltpu.VMEM((SUBLANES, t), jnp.float32)] + _attend_scratch(t, s),
        compiler_params=pltpu.CompilerParams(
            dimension_semantics=("arbitrary", "arbitrary", "arbitrary"),
            vmem_limit_bytes=VMEM_LIMIT),
        name="dsa_attention",
    )(iq, ik2, misc, aq, ak_t, av)


def _fox_kernel(q_ref, kt_ref, v_ref, cq_ref, ck_ref, o_ref, s_ref, m_ref, acc_ref):
    t = q_ref.shape[1]
    cq = jnp.concatenate([cq_ref[0, 0, :, 0:1], cq_ref[0, 0, :, 1:2]], axis=0)

    def bias(s, keys, width, diagonal):
        ck = ck_ref[0, 0, :, keys]
        ck = jnp.concatenate([jnp.broadcast_to(ck[0:1], (t, width)),
                              jnp.broadcast_to(ck[1:2], (t, width))], axis=0)
        s = s + (cq - ck)
        if diagonal:
            below = (lax.broadcasted_iota(jnp.int32, (t, t), 1)
                     <= lax.broadcasted_iota(jnp.int32, (t, t), 0))
            s = jnp.where(jnp.concatenate([below, below], axis=0), s, NEG)
        return s

    o_ref[0] = _attend(_stack_heads(q_ref[0]), kt_ref, v_ref, bias, pl.program_id(2),
                       FOX_WIDTHS, s_ref, m_ref, acc_ref)


def _fox_attention(fq, fk_t, fv, cum_q, cum_k):
    b, s, _ = fq.shape
    t = FOX_TILE
    return pl.pallas_call(
        _fox_kernel,
        grid=(b, N_HEAD_BLOCKS, s // t),
        in_specs=[
            pl.BlockSpec((1, t, LANES), lambda bb, hp, i: (bb, i, hp)),
            pl.BlockSpec((1, LANES, s), lambda bb, hp, i: (bb, hp, 0)),
            pl.BlockSpec((1, s, HEADS_PER_BLOCK * LANES), lambda bb, hp, i: (bb, 0, hp)),
            pl.BlockSpec((1, 1, t, HEADS_PER_BLOCK), lambda bb, hp, i: (bb, hp, i, 0)),
            pl.BlockSpec((1, 1, HEADS_PER_BLOCK, s), lambda bb, hp, i: (bb, hp, 0, 0)),
        ],
        out_specs=pl.BlockSpec((1, t, LANES), lambda bb, hp, i: (bb, i, hp)),
        out_shape=jax.ShapeDtypeStruct((b, s, WIDTH), jnp.float32),
        scratch_shapes=_attend_scratch(t, s),
        compiler_params=pltpu.CompilerParams(
            dimension_semantics=("arbitrary", "arbitrary", "arbitrary"),
            vmem_limit_bytes=VMEM_LIMIT),
        name="fox_attention",
    )(fq, fk_t, fv, cum_q, cum_k)


def _out_kernel(x_ref, ya_ref, yb_ref, gain_ref, wg_ref, bm_ref, wa_ref, wb_ref,
                wo_ref, ngain_ref, o_ref):
    x = x_ref[...]
    h = _rms(x, gain_ref[...]).astype(jnp.bfloat16)
    g = jnp.dot(h, wg_ref[...], preferred_element_type=jnp.float32)
    a_gate = g[:, :WIDTH]
    f_gate = g[:, WIDTH:2 * WIDTH]
    m_logit = g[:, 2 * WIDTH:] + bm_ref[...]
    za = (ya_ref[...] * (a_gate * jax.nn.sigmoid(a_gate))).astype(jnp.bfloat16)
    zb = (yb_ref[...] * (f_gate * jax.nn.sigmoid(f_gate))).astype(jnp.bfloat16)
    ua = jnp.dot(za, wa_ref[...], preferred_element_type=jnp.float32)
    ub = jnp.dot(zb, wb_ref[...], preferred_element_type=jnp.float32)
    gates = jax.nn.sigmoid(m_logit)
    merged = gates[:, :D_MODEL] * ua + gates[:, D_MODEL:] * ub
    y = x + jnp.dot(merged.astype(jnp.bfloat16), wo_ref[...],
                    preferred_element_type=jnp.float32)
    o_ref[...] = _rms(y, ngain_ref[...])


def _output(x2, ya, yb, gain, w_gates, b_merge, w_a, w_b, w_o, next_gain):
    m = x2.shape[0]
    tm = PROJ_ROWS
    row = lambda i: (i, 0)
    fixed = lambda i: (0, 0)
    return pl.pallas_call(
        _out_kernel,
        grid=(m // tm,),
        in_specs=[
            pl.BlockSpec((tm, D_MODEL), row),
            pl.BlockSpec((tm, WIDTH), row),
            pl.BlockSpec((tm, WIDTH), row),
            pl.BlockSpec((1, D_MODEL), fixed),
            pl.BlockSpec(w_gates.shape, fixed),
            pl.BlockSpec((1, 2 * D_MODEL), fixed),
            pl.BlockSpec(w_a.shape, fixed),
            pl.BlockSpec(w_b.shape, fixed),
            pl.BlockSpec(w_o.shape, fixed),
            pl.BlockSpec((1, D_MODEL), fixed),
        ],
        out_specs=pl.BlockSpec((tm, D_MODEL), row),
        out_shape=jax.ShapeDtypeStruct((m, D_MODEL), jnp.float32),
        compiler_params=pltpu.CompilerParams(
            dimension_semantics=("arbitrary",), vmem_limit_bytes=VMEM_LIMIT),
        name="out_proj",
    )(x2, ya, yb, gain, w_gates, b_merge, w_a, w_b, w_o, next_gain)


def _split_w_in(w):
    sizes = (WIDTH, WIDTH, WIDTH, WIDTH, N_IDX_HEADS * IDX_DIM, IDX_DIM, N_IDX_HEADS,
             WIDTH, WIDTH, WIDTH, WIDTH, N_HEADS, 2 * D_MODEL)
    points = np.cumsum(sizes)[:-1]
    return jnp.split(w, points, axis=-1)


def kernel(x, positions, norm_gain, w_in, b_forget, b_merge, w_branch_dsa,
           w_branch_fox, w_out, final_gain):
    b, s, d = x.shape
    depth = w_in.shape[0]
    bf16 = jnp.bfloat16

    half = jnp.arange(ROPE_HALF, dtype=jnp.float32)
    inv_freq = ROPE_THETA ** (-half * 2.0 / ROPE_DIM)
    dim = np.arange(LANES) % HEAD_DIM
    invf = jnp.where(dim < ROPE_DIM, inv_freq[dim % ROPE_HALF], 0.0)[None, :]
    pos2 = positions.reshape(b * s, 1)

    x2 = x.reshape(b * s, d)
    normed = None
    for l in range(depth):
        (w_aq, w_ak, w_av, w_ag, w_iq, w_ik, w_iw,
         w_fq, w_fk, w_fv, w_fg, w_fl, w_m) = _split_w_in(w_in[l])
        pad = jnp.zeros((d, LANES - N_IDX_HEADS - N_HEADS), w_in.dtype)
        w_all = jnp.concatenate(
            [w_aq, w_ak, w_av, w_fq, w_fk, w_fv, w_iq, w_ik, w_ik, w_iw, w_fl, pad],
            axis=1).astype(bf16)
        w_gates = jnp.concatenate([w_ag, w_fg, w_m], axis=1).astype(bf16)
        bf_row = jnp.zeros((1, LANES), jnp.float32).at[
            0, MISC_CUM:MISC_CUM + N_HEADS].set(b_forget[l])
        gain = norm_gain[l][None, :]
        next_gain = (norm_gain[l + 1] if l + 1 < depth else final_gain)[None, :]

        aq, ak_t, av, fq, fk_t, fv, iq, ik2, misc = _project(
            x2, pos2, gain, invf, w_all, bf_row, s)
        shape3 = lambda a: a.reshape(b, s, a.shape[-1])
        ya = _dsa_attention(shape3(iq), shape3(ik2), shape3(misc), shape3(aq), ak_t,
                            shape3(av))
        cum = misc[:, MISC_CUM:MISC_CUM + N_HEADS].reshape(
            b, s, N_HEAD_BLOCKS, HEADS_PER_BLOCK)
        cum_q = cum.transpose(0, 2, 1, 3)
        cum_k = cum.transpose(0, 2, 3, 1)
        yb = _fox_attention(shape3(fq), fk_t, shape3(fv), cum_q, cum_k)
        normed = _output(x2, ya.reshape(b * s, WIDTH), yb.reshape(b * s, WIDTH),
                         gain, w_gates, b_merge[l][None, :],
                         w_branch_dsa[l].astype(bf16), w_branch_fox[l].astype(bf16),
                         w_out[l].astype(bf16), next_gain)
        assert depth == 1, "stacked layers need the un-normalised residual as well"
    return normed.reshape(b, s, d)
```

```python
import functools

import jax
import jax.numpy as jnp
import numpy as np
from jax import lax
from jax.experimental import pallas as pl
from jax.experimental.pallas import tpu as pltpu

D_MODEL = 1024
HEAD_DIM = 64
N_HEADS = 8
WIDTH = N_HEADS * HEAD_DIM
N_IDX_HEADS = 4
IDX_DIM = 64
TOPK_MAX = 256
ROPE_THETA = 500000.0
ROPE_DIM = HEAD_DIM // 4
ROPE_HALF = ROPE_DIM // 2
RMS_EPS = 1e-6
NEG = -1e30
LOG2_E = float(np.log2(np.e))

LANES = 128
SUBLANES = 8
HEADS_PER_BLOCK = LANES // HEAD_DIM
N_HEAD_BLOCKS = N_HEADS // HEADS_PER_BLOCK
VMEM_LIMIT = 56 * 1024 * 1024

PROJ_ROWS = 512
ATT_TILE = 256
INDEX_WIDTHS = (2048, 1024, 512, 256)
MASK_WIDTHS = (1024, 512, 256)
DSA_TILE = 512
DSA_PANEL = 4096
DSA_WIDTHS = (4096, 2048, 1024, 512)
FOX_TILE = 512
FOX_WIDTHS = (4096, 2048, 1024, 512)
COUNT_CHAINS = 4

TINY = float(np.finfo(np.float32).tiny)
ALL_THR = -3.0e38
BIG = 1.0e9
MAX_SEARCH = 400
INTERPOLATE_STEPS = 32
MIN_FRACTION = 1.0 / 32.0

MISC_IW = 0
MISC_CUM = N_IDX_HEADS


def _rms(x, gain):
    ms = jnp.mean(x * x, axis=-1, keepdims=True)
    return x * lax.rsqrt(ms + RMS_EPS) * gain


def _proj_kernel(x_ref, pos_ref, gain_ref, invf_ref, w_ref, bf_ref,
                 aq_ref, ak_ref, av_ref, fq_ref, fk_ref, fv_ref, iq_ref, ik_ref,
                 misc_ref, carry_ref, *, tiles_per_batch):
    tm = x_ref.shape[0]
    h = _rms(x_ref[...], gain_ref[...]).astype(jnp.bfloat16)

    ang = pos_ref[...].astype(jnp.float32) * invf_ref[...]
    cos = jnp.cos(ang)
    sin = jnp.sin(ang)
    d = lax.broadcasted_iota(jnp.int32, (tm, LANES), 1) % HEAD_DIM
    sin_lo = jnp.where(d < ROPE_HALF, -sin, 0.0)
    sin_hi = jnp.where(d >= ROPE_HALF, sin, 0.0)

    def rope(t):
        up = pltpu.roll(t, LANES - ROPE_HALF, 1)
        dn = pltpu.roll(t, ROPE_HALF, 1)
        return t * cos + up * sin_lo + dn * sin_hi

    def blocks_of(col0, width, use_rope, scale):
        r = jnp.dot(h, w_ref[:, col0:col0 + width],
                    preferred_element_type=jnp.float32)
        for c in range(width // LANES):
            t = r[:, c * LANES:(c + 1) * LANES]
            if use_rope:
                t = rope(t)
            if scale != 1.0:
                t = t * scale
            yield c, t

    def emit(out_ref, col0, width, use_rope, scale):
        for c, t in blocks_of(col0, width, use_rope, scale):
            out_ref[:, c * LANES:(c + 1) * LANES] = t.astype(out_ref.dtype)

    def emit_transposed(out_ref, col0, width, use_rope):
        for c, t in blocks_of(col0, width, use_rope, 1.0):
            out_ref[0, c * LANES:(c + 1) * LANES, :] = jnp.transpose(t).astype(
                out_ref.dtype)

    def emit_values(out_ref, col0):
        lane = lax.broadcasted_iota(jnp.int32, (tm, LANES), 1)
        tail = jnp.where(lane == HEAD_DIM, 1.0, 0.0)
        for c, t in blocks_of(col0, WIDTH, False, 1.0):
            pair = (t, pltpu.roll(t, HEAD_DIM, 1))
            for k in range(HEADS_PER_BLOCK):
                head = c * HEADS_PER_BLOCK + k
                out_ref[:, head * LANES:(head + 1) * LANES] = jnp.where(
                    lane < HEAD_DIM, pair[k], tail).astype(out_ref.dtype)

    qk_scale = HEAD_DIM ** -0.5 * LOG2_E
    emit(aq_ref, 0 * WIDTH, WIDTH, True, qk_scale)
    emit_transposed(ak_ref, 1 * WIDTH, WIDTH, True)
    emit_values(av_ref, 2 * WIDTH)
    emit(fq_ref, 3 * WIDTH, WIDTH, False, qk_scale)
    emit_transposed(fk_ref, 4 * WIDTH, WIDTH, False)
    emit_values(fv_ref, 5 * WIDTH)
    col = 6 * WIDTH
    emit(iq_ref, col, N_IDX_HEADS * IDX_DIM, True, IDX_DIM ** -0.5)
    col += N_IDX_HEADS * IDX_DIM
    emit(ik_ref, col, LANES, True, 1.0)
    col += LANES

    r = jnp.dot(h, w_ref[:, col:col + LANES], preferred_element_type=jnp.float32)
    z = r + bf_ref[...]
    logf = jnp.minimum(z, 0.0) - jnp.log1p(jnp.exp(-jnp.abs(z)))

    @pl.when(pl.program_id(0) % tiles_per_batch == 0)
    def _():
        carry_ref[...] = jnp.zeros_like(carry_ref)

    rows = lax.broadcasted_iota(jnp.int32, (tm, tm), 0)
    cols = lax.broadcasted_iota(jnp.int32, (tm, tm), 1)
    tri = jnp.where(cols <= rows, 1.0, 0.0).astype(jnp.bfloat16)
    hi = logf.astype(jnp.bfloat16)
    rem = logf - hi.astype(jnp.float32)
    mid = rem.astype(jnp.bfloat16)
    lo = (rem - mid.astype(jnp.float32)).astype(jnp.bfloat16)
    cum = (jnp.dot(tri, hi, preferred_element_type=jnp.float32)
           + jnp.dot(tri, mid, preferred_element_type=jnp.float32)
           + jnp.dot(tri, lo, preferred_element_type=jnp.float32)
           + carry_ref[...])
    carry_ref[...] = cum[tm - 1:tm, :]

    lane = lax.broadcasted_iota(jnp.int32, (tm, LANES), 1)
    misc = jnp.where(lane < MISC_CUM, r * (N_IDX_HEADS ** -0.5),
                     jnp.where(lane < MISC_CUM + N_HEADS, cum * LOG2_E, 0.0))
    misc_ref[...] = misc


def _project(x2, pos2, gain, invf, w_all, bf_row, seq):
    m = x2.shape[0]
    tm = PROJ_ROWS
    n_w = w_all.shape[1]
    row = lambda i: (i, 0)
    fixed = lambda i: (0, 0)
    bf16 = jnp.bfloat16
    tiles_per_batch = seq // tm
    batch = m // seq
    transposed = lambda i: (i // tiles_per_batch, 0, i % tiles_per_batch)

    def rows(width, dtype=bf16):
        return jax.ShapeDtypeStruct((m, width), dtype), pl.BlockSpec((tm, width), row)

    def keys(width):
        return (jax.ShapeDtypeStruct((batch, width, seq), bf16),
                pl.BlockSpec((1, width, tm), transposed))

    outs = [rows(WIDTH), keys(WIDTH), rows(N_HEADS * LANES),
            rows(WIDTH), keys(WIDTH), rows(N_HEADS * LANES),
            rows(N_IDX_HEADS * IDX_DIM), rows(LANES),
            rows(LANES, jnp.float32)]
    out_shape = [o[0] for o in outs]
    out_specs = [o[1] for o in outs]
    return pl.pallas_call(
        functools.partial(_proj_kernel, tiles_per_batch=tiles_per_batch),
        grid=(m // tm,),
        in_specs=[
            pl.BlockSpec((tm, D_MODEL), row),
            pl.BlockSpec((tm, 1), row),
            pl.BlockSpec((1, D_MODEL), fixed),
            pl.BlockSpec((1, LANES), fixed),
            pl.BlockSpec((D_MODEL, n_w), fixed),
            pl.BlockSpec((1, LANES), fixed),
        ],
        out_specs=out_specs,
        out_shape=out_shape,
        scratch_shapes=[pltpu.VMEM((1, LANES), jnp.float32)],
        compiler_params=pltpu.CompilerParams(
            dimension_semantics=("arbitrary",), vmem_limit_bytes=VMEM_LIMIT),
        name="in_proj",
    )(x2, pos2, gain, invf, w_all, bf_row)


def _half_select(x, first):
    lane = lax.broadcasted_iota(jnp.int32, x.shape, x.ndim - 1)
    keep = (lane < HEAD_DIM) if first else (lane >= HEAD_DIM)
    return jnp.where(keep, x, jnp.zeros_like(x))


def _stack_heads(q):
    return jnp.concatenate([_half_select(q, True), _half_select(q, False)], axis=0)


def _cover(limit, widths, fn, carry=0):
    widest = widths[0]
    n_widest = limit // widest

    def keys(start, width):
        return pl.ds(pl.multiple_of(start, width), width)

    carry = lax.fori_loop(
        0, n_widest, lambda j, c: fn(keys(j * widest, widest), widest, c), carry)
    start = n_widest * widest
    rest = limit - start
    for width in widths[1:]:
        part = rest & width
        carry = lax.cond(part != 0,
                         lambda c, start=start, width=width: fn(keys(start, width), width, c),
                         lambda c: c, carry)
        start = start + part
    return carry


def _attend(qs, kt_ref, v_ref, bias, i, widths, s_ref, m_ref, acc_ref):
    t = qs.shape[0] // HEADS_PER_BLOCK
    panel = s_ref.shape[1]
    seq = kt_ref.shape[2]
    assert widths[-1] == t and all(a == 2 * b for a, b in zip(widths, widths[1:]))
    assert widths[0] <= panel and 2 * panel >= seq

    def phase_a(keys, slot, width, diagonal):
        s = jnp.dot(qs, kt_ref[0, :, keys], preferred_element_type=jnp.float32)
        s = bias(s, keys, width, diagonal)
        s_ref[:, slot] = s
        m = m_ref[...]
        for c in range(width // LANES):
            m = jnp.maximum(m, s[:, c * LANES:(c + 1) * LANES])
        m_ref[...] = m

    def phase_b(keys, slot, m):
        p = jnp.exp2(s_ref[:, slot] - m).astype(jnp.bfloat16)
        for k in range(HEADS_PER_BLOCK):
            acc_ref[k * t:(k + 1) * t, :] += jnp.dot(
                p[k * t:(k + 1) * t], v_ref[0, keys, k * LANES:(k + 1) * LANES],
                preferred_element_type=jnp.float32)

    def run(base, below, with_diagonal):
        def keys_of(slot, width):
            return pl.ds(pl.multiple_of(base + slot.start, width), width)

        def maybe(flag, fn):
            if isinstance(flag, bool):
                if flag:
                    fn()
            else:
                pl.when(flag)(fn)

        diag_slot = pl.ds(pl.multiple_of(below, t), t)
        diag_keys = pl.ds(pl.multiple_of(i * t, t), t)

        def below_a(slot, width, carry):
            phase_a(keys_of(slot, width), slot, width, False)
            return carry

        m_ref[...] = jnp.full(m_ref.shape, NEG, jnp.float32)
        _cover(below, widths, below_a)
        maybe(with_diagonal, lambda: phase_a(diag_keys, diag_slot, t, True))
        m = jnp.max(m_ref[...], axis=1, keepdims=True)

        def below_b(slot, width, carry):
            phase_b(keys_of(slot, width), slot, m)
            return carry

        acc_ref[...] = jnp.zeros(acc_ref.shape, jnp.float32)
        _cover(below, widths, below_b)
        maybe(with_diagonal, lambda: phase_b(diag_keys, diag_slot, m))
        return m, acc_ref[...]

    limit = i * t
    if panel >= seq:
        _, acc = run(0, limit, True)
    else:
        m1, acc1 = run(0, jnp.minimum(limit, panel), limit < panel)
        m2, acc2 = lax.cond(
            limit >= panel,
            lambda: run(panel, limit - panel, True),
            lambda: (jnp.full((qs.shape[0], 1), NEG, jnp.float32),
                     jnp.zeros(acc_ref.shape, jnp.float32)))
        m = jnp.maximum(m1, m2)
        acc = acc1 * jnp.exp2(m1 - m) + acc2 * jnp.exp2(m2 - m)

    o = acc / acc[:, HEAD_DIM:HEAD_DIM + 1]
    lane = lax.broadcasted_iota(jnp.int32, (t, LANES), 1)
    return jnp.where(lane < HEAD_DIM, o[:t], pltpu.roll(o[t:], HEAD_DIM, 1))


def _attend_scratch(t, panel):
    rows = HEADS_PER_BLOCK * t
    return [pltpu.VMEM((rows, panel), jnp.float32),
            pltpu.VMEM((rows, LANES), jnp.float32),
            pltpu.VMEM((rows, LANES), jnp.float32)]


def _index_mask(iq, misc, ik_ref, sct_ref, sc_ref, out_rows, hi_ref, lo_ref, i):
    t = ATT_TILE
    top_k = float(TOPK_MAX)

    def fold(x, op, rows=SUBLANES):
        return op(x.reshape(x.shape[0] // rows, rows, t), axis=0)

    qt = []
    for h in range(N_IDX_HEADS):
        block = _half_select(iq[:, (h // 2) * LANES:(h // 2 + 1) * LANES], h % 2 == 0)
        qt.append(jnp.transpose(block.astype(jnp.float32)).astype(jnp.bfloat16))
    misc_t = jnp.transpose(misc)
    wh = [misc_t[MISC_IW + h:MISC_IW + h + 1, :] for h in range(N_IDX_HEADS)]

    def scores(keys):
        k = ik_ref[0, keys, :]
        sc = None
        for h in range(N_IDX_HEADS):
            rel = jnp.dot(k, qt[h], preferred_element_type=jnp.float32)
            term = wh[h] * jnp.maximum(rel, 0.0)
            sc = term if sc is None else sc + term
        return sc

    def track(for_max, for_min):
        hi_ref[...] = jnp.maximum(hi_ref[...], fold(for_max, jnp.max))
        lo_ref[...] = jnp.minimum(lo_ref[...], fold(for_min, jnp.min))

    def score_block(keys, width, carry):
        sc = scores(keys)
        sct_ref[keys, :] = sc
        track(sc, sc)
        return carry

    hi_ref[...] = jnp.full(hi_ref.shape, -jnp.inf, jnp.float32)
    lo_ref[...] = jnp.full(lo_ref.shape, jnp.inf, jnp.float32)
    _cover(i * t, INDEX_WIDTHS, score_block)
    diag = pl.ds(pl.multiple_of(i * t, t), t)
    sc = scores(diag)
    causal = (lax.broadcasted_iota(jnp.int32, (t, t), 0)
              <= lax.broadcasted_iota(jnp.int32, (t, t), 1))
    masked = jnp.where(causal, sc, -jnp.inf)
    sct_ref[diag, :] = masked
    track(masked, jnp.where(causal, sc, jnp.inf))

    def count_ge(thr):
        rows = COUNT_CHAINS * SUBLANES

        def block(keys, width, acc):
            for r in range(width // rows):
                part = pl.ds(pl.multiple_of(keys.start + r * rows, rows), rows)
                acc = acc + jnp.where(sct_ref[part, :] >= thr, 1.0, 0.0)
            return acc

        acc = _cover((i + 1) * t, INDEX_WIDTHS, block,
                     jnp.zeros((rows, t), jnp.float32))
        return jnp.sum(acc, axis=0, keepdims=True)

    def flag(cond):
        return jnp.where(cond, 1.0, 0.0)

    rmax = jnp.max(hi_ref[...], axis=0, keepdims=True)
    rmin = jnp.min(lo_ref[...], axis=0, keepdims=True)
    n_causal = (i * t + 1 + lax.broadcasted_iota(jnp.int32, (1, t), 1)
                ).astype(jnp.float32)
    c_pos = count_ge(jnp.full((1, t), TINY, jnp.float32))
    c_zero = lax.cond(jnp.min(c_pos) < top_k,
                      lambda: count_ge(jnp.zeros((1, t), jnp.float32)),
                      lambda: c_pos)
    take_all = n_causal <= top_k
    tie_zero = jnp.where(c_pos < top_k, flag(c_zero >= top_k), 0.0) > 0.5
    done = jnp.where(take_all, 1.0, jnp.where(tie_zero, 1.0, flag(c_pos == top_k)))
    thr = jnp.where(take_all, ALL_THR, jnp.where(tie_zero, 0.0, TINY))
    take = jnp.where(take_all, BIG, jnp.where(tie_zero, top_k - c_pos, BIG))
    positive = c_pos > top_k
    lo = jnp.where(positive, TINY, rmin)
    c_lo = jnp.where(positive, c_pos, n_causal)
    hi = jnp.where(positive, rmax * (1.0 + 2.0 ** -20) + TINY, 0.0)
    c_hi = jnp.where(positive, 0.0, c_zero)
    log_target = float(np.log(top_k + 0.5))

    def cond(state):
        it, done = state[0], state[-1]
        return (it < MAX_SEARCH) & (jnp.min(done) < 0.5)

    def step(state):
        it, lo, hi, c_lo, c_hi, w_lo, w_hi, last, thr, take, done = state
        f_lo = (jnp.log(c_lo) - log_target) * w_lo
        f_hi = (log_target - jnp.log(jnp.maximum(c_hi, 0.5))) * w_hi
        frac = f_lo / (f_lo + f_hi)
        frac = jnp.clip(frac, MIN_FRACTION, 1.0 - MIN_FRACTION)
        frac = jnp.where(it < INTERPOLATE_STEPS, frac, 0.5)
        guess = lo + (hi - lo) * frac
        halfway = 0.5 * lo + 0.5 * hi
        mid = jnp.where(guess <= lo, halfway, jnp.where(guess >= hi, halfway, guess))
        spent = jnp.where(mid <= lo, 1.0, flag(mid >= hi))
        c = count_ge(mid)
        finish = (1.0 - done) * jnp.maximum(spent, flag(c == top_k)) > 0.5
        thr = jnp.where(finish, jnp.where(spent > 0.5, lo, mid), thr)
        take = jnp.where(finish, jnp.where(spent > 0.5, top_k - c_hi, BIG), take)
        above = c > top_k
        below = c < top_k
        lo = jnp.where(above, mid, lo)
        c_lo = jnp.where(above, c, c_lo)
        hi = jnp.where(below, mid, hi)
        c_hi = jnp.where(below, c, c_hi)
        w_hi = jnp.where(above, jnp.where(last > 0.5, 0.5 * w_hi, 1.0),
                         jnp.where(below, 1.0, w_hi))
        w_lo = jnp.where(below, jnp.where(last < -0.5, 0.5 * w_lo, 1.0),
                         jnp.where(above, 1.0, w_lo))
        last = jnp.where(above, 1.0, jnp.where(below, -1.0, last))
        done = jnp.where(finish, 1.0, done)
        return it + 1, lo, hi, c_lo, c_hi, w_lo, w_hi, last, thr, take, done

    one = jnp.ones((1, t), jnp.float32)
    _, lo, _, _, c_hi, _, _, _, thr, take, done = lax.while_loop(
        cond, step, (jnp.int32(0), lo, hi, c_lo, c_hi, one, one, 0.0 * one,
                     thr, take, done))
    thr = jnp.where(done < 0.5, lo, thr)
    take = jnp.where(done < 0.5, top_k - c_hi, take)

    ranked = jnp.max(flag(take < BIG)) > 0.5

    @pl.when(jnp.logical_not(ranked))
    def _():
        def block(keys, width, carry):
            for u in range(width // t):
                part = pl.ds(pl.multiple_of(keys.start + u * t, t), t)
                sc_ref[out_rows, part] = jnp.transpose(
                    jnp.where(sct_ref[part, :] >= thr, 0.0, NEG)).astype(sc_ref.dtype)
            return carry

        _cover((i + 1) * t, MASK_WIDTHS, block)

    @pl.when(ranked)
    def _():
        earlier = (lax.broadcasted_iota(jnp.int32, (t, t), 1)
                   < lax.broadcasted_iota(jnp.int32, (t, t), 0))
        earlier = jnp.where(earlier, 1.0, 0.0).astype(jnp.bfloat16)

        def tile(j, seen):
            keys = pl.ds(pl.multiple_of(j * t, t), t)
            s = sct_ref[keys, :]
            eq = flag(s == thr)
            rank = seen + jnp.dot(earlier, eq.astype(jnp.bfloat16),
                                  preferred_element_type=jnp.float32)
            sc_ref[out_rows, keys] = jnp.transpose(jnp.where(s > thr, 0.0, jnp.where(
                s == thr, jnp.where(rank < take, 0.0, NEG), NEG))).astype(sc_ref.dtype)
            return seen + jnp.sum(eq, axis=0, keepdims=True)

        lax.fori_loop(0, i + 1, tile, jnp.zeros((1, t), jnp.float32))


def _dsa_kernel(iq_ref, ik_ref, misc_ref, q_ref, kt_ref, v_ref, o_ref, sct_ref, sc_ref,
                hi_ref, lo_ref, s_ref, m_ref, acc_ref):
    i = pl.program_id(1)
    n_index = q_ref.shape[1] // ATT_TILE

    @pl.when(pl.program_id(2) == 0)
    def _():
        for k in range(n_index):
            rows = slice(k * ATT_TILE, (k + 1) * ATT_TILE)
            first = n_index * i + k
            _index_mask(iq_ref[0, rows, :], misc_ref[0, rows, :], ik_ref, sct_ref,
                        sc_ref, rows, hi_ref, lo_ref, first)
            if k + 1 < n_index:
                rest = (n_index - 1 - k) * ATT_TILE
                later = pl.ds(pl.multiple_of((first + 1) * ATT_TILE, ATT_TILE), rest)
                sc_ref[rows, later] = jnp.full((ATT_TILE, rest), NEG, sc_ref.dtype)

    def bias(s, keys, width, diagonal):
        mask = sc_ref[:, keys].astype(jnp.float32)
        return s + jnp.concatenate([mask, mask], axis=0)

    o_ref[0] = _attend(_stack_heads(q_ref[0]), kt_ref, v_ref, bias, i, DSA_WIDTHS,
                       s_ref, m_ref, acc_ref)


def _dsa_attention(iq, ik2, misc, aq, ak_t, av):
    b, s, _ = aq.shape
    t = DSA_TILE
    return pl.pallas_call(
        _dsa_kernel,
        grid=(b, s // t, N_HEAD_BLOCKS),
        in_specs=[
            pl.BlockSpec((1, t, N_IDX_HEADS * IDX_DIM), lambda bb, i, hp: (bb, i, 0)),
            pl.BlockSpec((1, s, LANES), lambda bb, i, hp: (bb, 0, 0)),
            pl.BlockSpec((1, t, LANES), lambda bb, i, hp: (bb, i, 0)),
            pl.BlockSpec((1, t, LANES), lambda bb, i, hp: (bb, i, hp)),
            pl.BlockSpec((1, LANES, s), lambda bb, i, hp: (bb, hp, 0)),
            pl.BlockSpec((1, s, HEADS_PER_BLOCK * LANES), lambda bb, i, hp: (bb, 0, hp)),
        ],
        out_specs=pl.BlockSpec((1, t, LANES), lambda bb, i, hp: (bb, i, hp)),
        out_shape=jax.ShapeDtypeStruct((b, s, WIDTH), jnp.float32),
        scratch_shapes=[pltpu.VMEM((s, ATT_TILE), jnp.float32),
                        pltpu.VMEM((t, s), jnp.bfloat16),
                        pltpu.VMEM((SUBLANES, ATT_TILE), jnp.float32),
                        pltpu.VMEM((SUBLANES, ATT_TILE), jnp.float32)]
        + _attend_scratch(t, DSA_PANEL),
        compiler_params=pltpu.CompilerParams(
            dimension_semantics=("arbitrary", "arbitrary", "arbitrary"),
            vmem_limit_bytes=VMEM_LIMIT),
        name="dsa_attention",
    )(iq, ik2, misc, aq, ak_t, av)


def _fox_kernel(q_ref, kt_ref, v_ref, cq_ref, ck_ref, o_ref, s_ref, m_ref, acc_ref):
    t = q_ref.shape[1]
    cq = jnp.concatenate([cq_ref[0, 0, :, 0:1], cq_ref[0, 0, :, 1:2]], axis=0)

    def bias(s, keys, width, diagonal):
        ck = ck_ref[0, 0, :, keys]
        ck = jnp.concatenate([jnp.broadcast_to(ck[0:1], (t, width)),
                              jnp.broadcast_to(ck[1:2], (t, width))], axis=0)
        s = s + (cq - ck)
        if diagonal:
            below = (lax.broadcasted_iota(jnp.int32, (t, t), 1)
                     <= lax.broadcasted_iota(jnp.int32, (t, t), 0))
            s = jnp.where(jnp.concatenate([below, below], axis=0), s, NEG)
        return s

    o_ref[0] = _attend(_stack_heads(q_ref[0]), kt_ref, v_ref, bias, pl.program_id(2),
                       FOX_WIDTHS, s_ref, m_ref, acc_ref)


def _fox_attention(fq, fk_t, fv, cum_q, cum_k):
    b, s, _ = fq.shape
    t = FOX_TILE
    return pl.pallas_call(
        _fox_kernel,
        grid=(b, N_HEAD_BLOCKS, s // t),
        in_specs=[
            pl.BlockSpec((1, t, LANES), lambda bb, hp, i: (bb, i, hp)),
            pl.BlockSpec((1, LANES, s), lambda bb, hp, i: (bb, hp, 0)),
            pl.BlockSpec((1, s, HEADS_PER_BLOCK * LANES), lambda bb, hp, i: (bb, 0, hp)),
            pl.BlockSpec((1, 1, t, HEADS_PER_BLOCK), lambda bb, hp, i: (bb, hp, i, 0)),
            pl.BlockSpec((1, 1, HEADS_PER_BLOCK, s), lambda bb, hp, i: (bb, hp, 0, 0)),
        ],
        out_specs=pl.BlockSpec((1, t, LANES), lambda bb, hp, i: (bb, i, hp)),
        out_shape=jax.ShapeDtypeStruct((b, s, WIDTH), jnp.float32),
        scratch_shapes=_attend_scratch(t, s),
        compiler_params=pltpu.CompilerParams(
            dimension_semantics=("arbitrary", "arbitrary", "arbitrary"),
            vmem_limit_bytes=VMEM_LIMIT),
        name="fox_attention",
    )(fq, fk_t, fv, cum_q, cum_k)


def _out_kernel(x_ref, ya_ref, yb_ref, gain_ref, wg_ref, bm_ref, wa_ref, wb_ref,
                wo_ref, ngain_ref, o_ref):
    x = x_ref[...]
    h = _rms(x, gain_ref[...]).astype(jnp.bfloat16)
    g = jnp.dot(h, wg_ref[...], preferred_element_type=jnp.float32)
    a_gate = g[:, :WIDTH]
    f_gate = g[:, WIDTH:2 * WIDTH]
    m_logit = g[:, 2 * WIDTH:] + bm_ref[...]
    za = (ya_ref[...] * (a_gate * jax.nn.sigmoid(a_gate))).astype(jnp.bfloat16)
    zb = (yb_ref[...] * (f_gate * jax.nn.sigmoid(f_gate))).astype(jnp.bfloat16)
    ua = jnp.dot(za, wa_ref[...], preferred_element_type=jnp.float32)
    ub = jnp.dot(zb, wb_ref[...], preferred_element_type=jnp.float32)
    gates = jax.nn.sigmoid(m_logit)
    merged = gates[:, :D_MODEL] * ua + gates[:, D_MODEL:] * ub
    y = x + jnp.dot(merged.astype(jnp.bfloat16), wo_ref[...],
                    preferred_element_type=jnp.float32)
    o_ref[...] = _rms(y, ngain_ref[...])


def _output(x2, ya, yb, gain, w_gates, b_merge, w_a, w_b, w_o, next_gain):
    m = x2.shape[0]
    tm = PROJ_ROWS
    row = lambda i: (i, 0)
    fixed = lambda i: (0, 0)
    return pl.pallas_call(
        _out_kernel,
        grid=(m // tm,),
        in_specs=[
            pl.BlockSpec((tm, D_MODEL), row),
            pl.BlockSpec((tm, WIDTH), row),
            pl.BlockSpec((tm, WIDTH), row),
            pl.BlockSpec((1, D_MODEL), fixed),
            pl.BlockSpec(w_gates.shape, fixed),
            pl.BlockSpec((1, 2 * D_MODEL), fixed),
            pl.BlockSpec(w_a.shape, fixed),
            pl.BlockSpec(w_b.shape, fixed),
            pl.BlockSpec(w_o.shape, fixed),
            pl.BlockSpec((1, D_MODEL), fixed),
        ],
        out_specs=pl.BlockSpec((tm, D_MODEL), row),
        out_shape=jax.ShapeDtypeStruct((m, D_MODEL), jnp.float32),
        compiler_params=pltpu.CompilerParams(
            dimension_semantics=("arbitrary",), vmem_limit_bytes=VMEM_LIMIT),
        name="out_proj",
    )(x2, ya, yb, gain, w_gates, b_merge, w_a, w_b, w_o, next_gain)


def _split_w_in(w):
    sizes = (WIDTH, WIDTH, WIDTH, WIDTH, N_IDX_HEADS * IDX_DIM, IDX_DIM, N_IDX_HEADS,
             WIDTH, WIDTH, WIDTH, WIDTH, N_HEADS, 2 * D_MODEL)
    points = np.cumsum(sizes)[:-1]
    return jnp.split(w, points, axis=-1)


def kernel(x, positions, norm_gain, w_in, b_forget, b_merge, w_branch_dsa,
           w_branch_fox, w_out, final_gain):
    b, s, d = x.shape
    depth = w_in.shape[0]
    bf16 = jnp.bfloat16

    half = jnp.arange(ROPE_HALF, dtype=jnp.float32)
    inv_freq = ROPE_THETA ** (-half * 2.0 / ROPE_DIM)
    dim = np.arange(LANES) % HEAD_DIM
    invf = jnp.where(dim < ROPE_DIM, inv_freq[dim % ROPE_HALF], 0.0)[None, :]
    pos2 = positions.reshape(b * s, 1)

    x2 = x.reshape(b * s, d)
    normed = None
    for l in range(depth):
        (w_aq, w_ak, w_av, w_ag, w_iq, w_ik, w_iw,
         w_fq, w_fk, w_fv, w_fg, w_fl, w_m) = _split_w_in(w_in[l])
        pad = jnp.zeros((d, LANES - N_IDX_HEADS - N_HEADS), w_in.dtype)
        w_all = jnp.concatenate(
            [w_aq, w_ak, w_av, w_fq, w_fk, w_fv, w_iq, w_ik, w_ik, w_iw, w_fl, pad],
            axis=1).astype(bf16)
        w_gates = jnp.concatenate([w_ag, w_fg, w_m], axis=1).astype(bf16)
        bf_row = jnp.zeros((1, LANES), jnp.float32).at[
            0, MISC_CUM:MISC_CUM + N_HEADS].set(b_forget[l])
        gain = norm_gain[l][None, :]
        next_gain = (norm_gain[l + 1] if l + 1 < depth else final_gain)[None, :]

        aq, ak_t, av, fq, fk_t, fv, iq, ik2, misc = _project(
            x2, pos2, gain, invf, w_all, bf_row, s)
        shape3 = lambda a: a.reshape(b, s, a.shape[-1])
        ya = _dsa_attention(shape3(iq), shape3(ik2), shape3(misc), shape3(aq), ak_t,
                            shape3(av))
        cum = misc[:, MISC_CUM:MISC_CUM + N_HEADS].reshape(
            b, s, N_HEAD_BLOCKS, HEADS_PER_BLOCK)
        cum_q = cum.transpose(0, 2, 1, 3)
        cum_k = cum.transpose(0, 2, 3, 1)
        yb = _fox_attention(shape3(fq), fk_t, shape3(fv), cum_q, cum_k)
        normed = _output(x2, ya.reshape(b * s, WIDTH), yb.reshape(b * s, WIDTH),
                         gain, w_gates, b_merge[l][None, :],
                         w_branch_dsa[l].astype(bf16), w_branch_fox[l].astype(bf16),
                         w_out[l].astype(bf16), next_gain)
        assert depth == 1, "stacked layers need the un-normalised residual as well"
    return normed.reshape(b, s, d)
```

```python
import functools

import jax
import jax.numpy as jnp
import numpy as np
from jax import lax
from jax.experimental import pallas as pl
from jax.experimental.pallas import tpu as pltpu

D_MODEL = 1024
HEAD_DIM = 64
N_HEADS = 8
WIDTH = N_HEADS * HEAD_DIM
N_IDX_HEADS = 4
IDX_DIM = 64
TOPK_MAX = 256
ROPE_THETA = 500000.0
ROPE_DIM = HEAD_DIM // 4
ROPE_HALF = ROPE_DIM // 2
RMS_EPS = 1e-6
NEG = -1e30
LOG2_E = float(np.log2(np.e))

LANES = 128
SUBLANES = 8
HEADS_PER_BLOCK = LANES // HEAD_DIM
N_HEAD_BLOCKS = N_HEADS // HEADS_PER_BLOCK
VMEM_LIMIT = 56 * 1024 * 1024

PROJ_ROWS = 512
ATT_TILE = 256
INDEX_WIDTHS = (2048, 1024, 512, 256)
MASK_WIDTHS = (1024, 512, 256)
DSA_TILE = 512
DSA_PANEL = 4096
DSA_WIDTHS = (4096, 2048, 1024, 512)
FOX_TILE = 512
FOX_WIDTHS = (4096, 2048, 1024, 512)
COUNT_CHAINS = 4

TINY = float(np.finfo(np.float32).tiny)
ALL_THR = -3.0e38
BIG = 1.0e9
JUST_ABOVE = 1.0 + 2.0 ** -20
MAX_SEARCH = 400
INTERPOLATE_STEPS = 32
MIN_FRACTION = 1.0 / 32.0

MISC_IW = 0
MISC_CUM = N_IDX_HEADS


def _rms(x, gain):
    ms = jnp.mean(x * x, axis=-1, keepdims=True)
    return x * lax.rsqrt(ms + RMS_EPS) * gain


def _proj_kernel(x_ref, pos_ref, gain_ref, invf_ref, w_ref, bf_ref,
                 aq_ref, ak_ref, av_ref, fq_ref, fk_ref, fv_ref, iq_ref, ik_ref,
                 misc_ref, carry_ref, *, tiles_per_batch):
    tm = x_ref.shape[0]
    h = _rms(x_ref[...], gain_ref[...]).astype(jnp.bfloat16)

    ang = pos_ref[...].astype(jnp.float32) * invf_ref[...]
    cos = jnp.cos(ang)
    sin = jnp.sin(ang)
    d = lax.broadcasted_iota(jnp.int32, (tm, LANES), 1) % HEAD_DIM
    sin_lo = jnp.where(d < ROPE_HALF, -sin, 0.0)
    sin_hi = jnp.where(d >= ROPE_HALF, sin, 0.0)

    def rope(t):
        up = pltpu.roll(t, LANES - ROPE_HALF, 1)
        dn = pltpu.roll(t, ROPE_HALF, 1)
        return t * cos + up * sin_lo + dn * sin_hi

    def blocks_of(col0, width, use_rope, scale):
        r = jnp.dot(h, w_ref[:, col0:col0 + width],
                    preferred_element_type=jnp.float32)
        for c in range(width // LANES):
            t = r[:, c * LANES:(c + 1) * LANES]
            if use_rope:
                t = rope(t)
            if scale != 1.0:
                t = t * scale
            yield c, t

    def emit(out_ref, col0, width, use_rope, scale):
        for c, t in blocks_of(col0, width, use_rope, scale):
            out_ref[:, c * LANES:(c + 1) * LANES] = t.astype(out_ref.dtype)

    def emit_transposed(out_ref, col0, width, use_rope):
        for c, t in blocks_of(col0, width, use_rope, 1.0):
            out_ref[0, c * LANES:(c + 1) * LANES, :] = jnp.transpose(t).astype(
                out_ref.dtype)

    def emit_values(out_ref, col0):
        lane = lax.broadcasted_iota(jnp.int32, (tm, LANES), 1)
        tail = jnp.where(lane == HEAD_DIM, 1.0, 0.0)
        for c, t in blocks_of(col0, WIDTH, False, 1.0):
            pair = (t, pltpu.roll(t, HEAD_DIM, 1))
            for k in range(HEADS_PER_BLOCK):
                head = c * HEADS_PER_BLOCK + k
                out_ref[:, head * LANES:(head + 1) * LANES] = jnp.where(
                    lane < HEAD_DIM, pair[k], tail).astype(out_ref.dtype)

    qk_scale = HEAD_DIM ** -0.5 * LOG2_E
    emit(aq_ref, 0 * WIDTH, WIDTH, True, qk_scale)
    emit_transposed(ak_ref, 1 * WIDTH, WIDTH, True)
    emit_values(av_ref, 2 * WIDTH)
    emit(fq_ref, 3 * WIDTH, WIDTH, False, qk_scale)
    emit_transposed(fk_ref, 4 * WIDTH, WIDTH, False)
    emit_values(fv_ref, 5 * WIDTH)
    col = 6 * WIDTH
    emit(iq_ref, col, N_IDX_HEADS * IDX_DIM, True, IDX_DIM ** -0.5)
    col += N_IDX_HEADS * IDX_DIM
    emit(ik_ref, col, LANES, True, 1.0)
    col += LANES

    r = jnp.dot(h, w_ref[:, col:col + LANES], preferred_element_type=jnp.float32)
    z = r + bf_ref[...]
    logf = jnp.minimum(z, 0.0) - jnp.log1p(jnp.exp(-jnp.abs(z)))

    @pl.when(pl.program_id(0) % tiles_per_batch == 0)
    def _():
        carry_ref[...] = jnp.zeros_like(carry_ref)

    rows = lax.broadcasted_iota(jnp.int32, (tm, tm), 0)
    cols = lax.broadcasted_iota(jnp.int32, (tm, tm), 1)
    tri = jnp.where(cols <= rows, 1.0, 0.0).astype(jnp.bfloat16)
    hi = logf.astype(jnp.bfloat16)
    rem = logf - hi.astype(jnp.float32)
    mid = rem.astype(jnp.bfloat16)
    lo = (rem - mid.astype(jnp.float32)).astype(jnp.bfloat16)
    cum = (jnp.dot(tri, hi, preferred_element_type=jnp.float32)
           + jnp.dot(tri, mid, preferred_element_type=jnp.float32)
           + jnp.dot(tri, lo, preferred_element_type=jnp.float32)
           + carry_ref[...])
    carry_ref[...] = cum[tm - 1:tm, :]

    lane = lax.broadcasted_iota(jnp.int32, (tm, LANES), 1)
    misc = jnp.where(lane < MISC_CUM, r * (N_IDX_HEADS ** -0.5),
                     jnp.where(lane < MISC_CUM + N_HEADS, cum * LOG2_E, 0.0))
    misc_ref[...] = misc


def _project(x2, pos2, gain, invf, w_all, bf_row, seq):
    m = x2.shape[0]
    tm = PROJ_ROWS
    n_w = w_all.shape[1]
    row = lambda i: (i, 0)
    fixed = lambda i: (0, 0)
    bf16 = jnp.bfloat16
    tiles_per_batch = seq // tm
    batch = m // seq
    transposed = lambda i: (i // tiles_per_batch, 0, i % tiles_per_batch)

    def rows(width, dtype=bf16):
        return jax.ShapeDtypeStruct((m, width), dtype), pl.BlockSpec((tm, width), row)

    def keys(width):
        return (jax.ShapeDtypeStruct((batch, width, seq), bf16),
                pl.BlockSpec((1, width, tm), transposed))

    outs = [rows(WIDTH), keys(WIDTH), rows(N_HEADS * LANES),
            rows(WIDTH), keys(WIDTH), rows(N_HEADS * LANES),
            rows(N_IDX_HEADS * IDX_DIM), rows(LANES),
            rows(LANES, jnp.float32)]
    out_shape = [o[0] for o in outs]
    out_specs = [o[1] for o in outs]
    return pl.pallas_call(
        functools.partial(_proj_kernel, tiles_per_batch=tiles_per_batch),
        grid=(m // tm,),
        in_specs=[
            pl.BlockSpec((tm, D_MODEL), row),
            pl.BlockSpec((tm, 1), row),
            pl.BlockSpec((1, D_MODEL), fixed),
            pl.BlockSpec((1, LANES), fixed),
            pl.BlockSpec((D_MODEL, n_w), fixed),
            pl.BlockSpec((1, LANES), fixed),
        ],
        out_specs=out_specs,
        out_shape=out_shape,
        scratch_shapes=[pltpu.VMEM((1, LANES), jnp.float32)],
        compiler_params=pltpu.CompilerParams(
            dimension_semantics=("arbitrary",), vmem_limit_bytes=VMEM_LIMIT),
        name="in_proj",
    )(x2, pos2, gain, invf, w_all, bf_row)


def _half_select(x, first):
    lane = lax.broadcasted_iota(jnp.int32, x.shape, x.ndim - 1)
    keep = (lane < HEAD_DIM) if first else (lane >= HEAD_DIM)
    return jnp.where(keep, x, jnp.zeros_like(x))


def _stack_heads(q):
    return jnp.concatenate([_half_select(q, True), _half_select(q, False)], axis=0)


def _cover(limit, widths, fn, carry=0):
    widest = widths[0]
    n_widest = limit // widest

    def keys(start, width):
        return pl.ds(pl.multiple_of(start, width), width)

    carry = lax.fori_loop(
        0, n_widest, lambda j, c: fn(keys(j * widest, widest), widest, c), carry)
    start = n_widest * widest
    rest = limit - start
    for width in widths[1:]:
        part = rest & width
        carry = lax.cond(part != 0,
                         lambda c, start=start, width=width: fn(keys(start, width), width, c),
                         lambda c: c, carry)
        start = start + part
    return carry


def _attend(qs, kt_ref, v_ref, bias, i, widths, s_ref, m_ref, acc_ref):
    t = qs.shape[0] // HEADS_PER_BLOCK
    panel = s_ref.shape[1]
    seq = kt_ref.shape[2]
    assert widths[-1] == t and all(a == 2 * b for a, b in zip(widths, widths[1:]))
    assert widths[0] <= panel and 2 * panel >= seq

    def phase_a(keys, slot, width, diagonal):
        s = jnp.dot(qs, kt_ref[0, :, keys], preferred_element_type=jnp.float32)
        s = bias(s, keys, width, diagonal)
        s_ref[:, slot] = s
        m = m_ref[...]
        for c in range(width // LANES):
            m = jnp.maximum(m, s[:, c * LANES:(c + 1) * LANES])
        m_ref[...] = m

    def phase_b(keys, slot, m):
        p = jnp.exp2(s_ref[:, slot] - m).astype(jnp.bfloat16)
        for k in range(HEADS_PER_BLOCK):
            acc_ref[k * t:(k + 1) * t, :] += jnp.dot(
                p[k * t:(k + 1) * t], v_ref[0, keys, k * LANES:(k + 1) * LANES],
                preferred_element_type=jnp.float32)

    def run(base, below, with_diagonal):
        def keys_of(slot, width):
            return pl.ds(pl.multiple_of(base + slot.start, width), width)

        def maybe(flag, fn):
            if isinstance(flag, bool):
                if flag:
                    fn()
            else:
                pl.when(flag)(fn)

        diag_slot = pl.ds(pl.multiple_of(below, t), t)
        diag_keys = pl.ds(pl.multiple_of(i * t, t), t)

        def below_a(slot, width, carry):
            phase_a(keys_of(slot, width), slot, width, False)
            return carry

        m_ref[...] = jnp.full(m_ref.shape, NEG, jnp.float32)
        _cover(below, widths, below_a)
        maybe(with_diagonal, lambda: phase_a(diag_keys, diag_slot, t, True))
        m = jnp.max(m_ref[...], axis=1, keepdims=True)

        def below_b(slot, width, carry):
            phase_b(keys_of(slot, width), slot, m)
            return carry

        acc_ref[...] = jnp.zeros(acc_ref.shape, jnp.float32)
        _cover(below, widths, below_b)
        maybe(with_diagonal, lambda: phase_b(diag_keys, diag_slot, m))
        return m, acc_ref[...]

    limit = i * t
    if panel >= seq:
        _, acc = run(0, limit, True)
    else:
        m1, acc1 = run(0, jnp.minimum(limit, panel), limit < panel)
        m2, acc2 = lax.cond(
            limit >= panel,
            lambda: run(panel, limit - panel, True),
            lambda: (jnp.full((qs.shape[0], 1), NEG, jnp.float32),
                     jnp.zeros(acc_ref.shape, jnp.float32)))
        m = jnp.maximum(m1, m2)
        acc = acc1 * jnp.exp2(m1 - m) + acc2 * jnp.exp2(m2 - m)

    o = acc / acc[:, HEAD_DIM:HEAD_DIM + 1]
    lane = lax.broadcasted_iota(jnp.int32, (t, LANES), 1)
    return jnp.where(lane < HEAD_DIM, o[:t], pltpu.roll(o[t:], HEAD_DIM, 1))


def _attend_scratch(t, panel):
    rows = HEADS_PER_BLOCK * t
    return [pltpu.VMEM((rows, panel), jnp.float32),
            pltpu.VMEM((rows, LANES), jnp.float32),
            pltpu.VMEM((rows, LANES), jnp.float32)]


def _index_mask(iq, misc, ik_ref, sct_ref, sc_ref, out_rows, hi_ref, lo_ref, i):
    t = ATT_TILE
    top_k = float(TOPK_MAX)

    def fold(x, op, rows=SUBLANES):
        return op(x.reshape(x.shape[0] // rows, rows, t), axis=0)

    qt = []
    for h in range(N_IDX_HEADS):
        block = _half_select(iq[:, (h // 2) * LANES:(h // 2 + 1) * LANES], h % 2 == 0)
        qt.append(jnp.transpose(block))
    misc_t = jnp.transpose(misc)
    wh = [misc_t[MISC_IW + h:MISC_IW + h + 1, :] for h in range(N_IDX_HEADS)]

    def scores(keys):
        k = ik_ref[0, keys, :]
        sc = None
        for h in range(N_IDX_HEADS):
            rel = jnp.dot(k, qt[h], preferred_element_type=jnp.float32)
            term = wh[h] * jnp.maximum(rel, 0.0)
            sc = term if sc is None else sc + term
        return sc

    def track(for_max, for_min):
        hi_ref[...] = jnp.maximum(hi_ref[...], fold(for_max, jnp.max))
        lo_ref[...] = jnp.minimum(lo_ref[...], fold(for_min, jnp.min))

    def score_block(keys, width, carry):
        sc = scores(keys)
        sct_ref[keys, :] = sc
        track(sc, sc)
        return carry

    hi_ref[...] = jnp.full(hi_ref.shape, -jnp.inf, jnp.float32)
    lo_ref[...] = jnp.full(lo_ref.shape, jnp.inf, jnp.float32)
    _cover(i * t, INDEX_WIDTHS, score_block)
    diag = pl.ds(pl.multiple_of(i * t, t), t)
    sc = scores(diag)
    causal = (lax.broadcasted_iota(jnp.int32, (t, t), 0)
              <= lax.broadcasted_iota(jnp.int32, (t, t), 1))
    masked = jnp.where(causal, sc, -jnp.inf)
    sct_ref[diag, :] = masked
    track(masked, jnp.where(causal, sc, jnp.inf))

    def count_ge(thr):
        rows = COUNT_CHAINS * SUBLANES

        def block(keys, width, acc):
            for r in range(width // rows):
                part = pl.ds(pl.multiple_of(keys.start + r * rows, rows), rows)
                acc = acc + jnp.where(sct_ref[part, :] >= thr, 1.0, 0.0)
            return acc

        acc = _cover((i + 1) * t, INDEX_WIDTHS, block,
                     jnp.zeros((rows, t), jnp.float32))
        return jnp.sum(acc, axis=0, keepdims=True)

    def flag(cond):
        return jnp.where(cond, 1.0, 0.0)

    rmax = jnp.max(hi_ref[...], axis=0, keepdims=True)
    rmin = jnp.min(lo_ref[...], axis=0, keepdims=True)
    n_causal = (i * t + 1 + lax.broadcasted_iota(jnp.int32, (1, t), 1)
                ).astype(jnp.float32)
    c_pos = count_ge(jnp.full((1, t), TINY, jnp.float32))
    c_zero = lax.cond(jnp.min(c_pos) < top_k,
                      lambda: count_ge(jnp.zeros((1, t), jnp.float32)),
                      lambda: c_pos)
    take_all = n_causal <= top_k
    tie_zero = jnp.where(c_pos < top_k, flag(c_zero >= top_k), 0.0) > 0.5
    done = jnp.where(take_all, 1.0, jnp.where(tie_zero, 1.0, flag(c_pos == top_k)))
    thr = jnp.where(take_all, ALL_THR, jnp.where(tie_zero, 0.0, TINY))
    take = jnp.where(take_all, BIG, jnp.where(tie_zero, top_k - c_pos, BIG))
    positive = c_pos > top_k
    lo = jnp.where(positive, TINY, rmin)
    c_lo = jnp.where(positive, c_pos, n_causal)
    hi = jnp.where(positive, rmax * JUST_ABOVE + TINY, 0.0)
    c_hi = jnp.where(positive, 0.0, c_zero)
    log_target = float(np.log(top_k + 0.5))

    def cond(state):
        it, done = state[0], state[-1]
        return (it < MAX_SEARCH) & (jnp.min(done) < 0.5)

    def step(state):
        it, lo, hi, c_lo, c_hi, w_lo, w_hi, last, thr, take, done = state
        f_lo = (jnp.log(c_lo) - log_target) * w_lo
        f_hi = (log_target - jnp.log(jnp.maximum(c_hi, 0.5))) * w_hi
        frac = f_lo / (f_lo + f_hi)
        frac = jnp.clip(frac, MIN_FRACTION, 1.0 - MIN_FRACTION)
        frac = jnp.where(it < INTERPOLATE_STEPS, frac, 0.5)
        guess = lo + (hi - lo) * frac
        halfway = 0.5 * lo + 0.5 * hi
        mid = jnp.where(guess <= lo, halfway, jnp.where(guess >= hi, halfway, guess))
        spent = jnp.where(mid <= lo, 1.0, flag(mid >= hi))
        c = count_ge(mid)
        finish = (1.0 - done) * jnp.maximum(spent, flag(c == top_k)) > 0.5
        thr = jnp.where(finish, jnp.where(spent > 0.5, lo, mid), thr)
        take = jnp.where(finish, jnp.where(spent > 0.5, top_k - c_hi, BIG), take)
        above = c > top_k
        below = c < top_k
        lo = jnp.where(above, mid, lo)
        c_lo = jnp.where(above, c, c_lo)
        hi = jnp.where(below, mid, hi)
        c_hi = jnp.where(below, c, c_hi)
        w_hi = jnp.where(above, jnp.where(last > 0.5, 0.5 * w_hi, 1.0),
                         jnp.where(below, 1.0, w_hi))
        w_lo = jnp.where(below, jnp.where(last < -0.5, 0.5 * w_lo, 1.0),
                         jnp.where(above, 1.0, w_lo))
        last = jnp.where(above, 1.0, jnp.where(below, -1.0, last))
        done = jnp.where(finish, 1.0, done)
        return it + 1, lo, hi, c_lo, c_hi, w_lo, w_hi, last, thr, take, done

    one = jnp.ones((1, t), jnp.float32)
    _, lo, _, _, c_hi, _, _, _, thr, take, done = lax.while_loop(
        cond, step, (jnp.int32(0), lo, hi, c_lo, c_hi, one, one, 0.0 * one,
                     thr, take, done))
    thr = jnp.where(done < 0.5, lo, thr)
    take = jnp.where(done < 0.5, top_k - c_hi, take)

    ranked = jnp.max(flag(take < BIG)) > 0.5

    @pl.when(jnp.logical_not(ranked))
    def _():
        def block(keys, width, carry):
            for u in range(width // t):
                part = pl.ds(pl.multiple_of(keys.start + u * t, t), t)
                sc_ref[out_rows, part] = jnp.transpose(
                    jnp.where(sct_ref[part, :] >= thr, 0.0, NEG)).astype(sc_ref.dtype)
            return carry

        _cover((i + 1) * t, MASK_WIDTHS, block)

    @pl.when(ranked)
    def _():
        earlier = (lax.broadcasted_iota(jnp.int32, (t, t), 1)
                   < lax.broadcasted_iota(jnp.int32, (t, t), 0))
        earlier = jnp.where(earlier, 1.0, 0.0).astype(jnp.bfloat16)

        def tile(j, seen):
            keys = pl.ds(pl.multiple_of(j * t, t), t)
            s = sct_ref[keys, :]
            eq = flag(s == thr)
            rank = seen + jnp.dot(earlier, eq.astype(jnp.bfloat16),
                                  preferred_element_type=jnp.float32)
            sc_ref[out_rows, keys] = jnp.transpose(jnp.where(s > thr, 0.0, jnp.where(
                s == thr, jnp.where(rank < take, 0.0, NEG), NEG))).astype(sc_ref.dtype)
            return seen + jnp.sum(eq, axis=0, keepdims=True)

        lax.fori_loop(0, i + 1, tile, jnp.zeros((1, t), jnp.float32))


def _dsa_kernel(iq_ref, ik_ref, misc_ref, q_ref, kt_ref, v_ref, o_ref, sct_ref, sc_ref,
                hi_ref, lo_ref, s_ref, m_ref, acc_ref):
    i = pl.program_id(1)
    n_index = q_ref.shape[1] // ATT_TILE

    @pl.when(pl.program_id(2) == 0)
    def _():
        for k in range(n_index):
            rows = slice(k * ATT_TILE, (k + 1) * ATT_TILE)
            first = n_index * i + k
            _index_mask(iq_ref[0, rows, :], misc_ref[0, rows, :], ik_ref, sct_ref,
                        sc_ref, rows, hi_ref, lo_ref, first)
            if k + 1 < n_index:
                rest = (n_index - 1 - k) * ATT_TILE
                later = pl.ds(pl.multiple_of((first + 1) * ATT_TILE, ATT_TILE), rest)
                sc_ref[rows, later] = jnp.full((ATT_TILE, rest), NEG, sc_ref.dtype)

    def bias(s, keys, width, diagonal):
        mask = sc_ref[:, keys].astype(jnp.float32)
        return s + jnp.concatenate([mask, mask], axis=0)

    o_ref[0] = _attend(_stack_heads(q_ref[0]), kt_ref, v_ref, bias, i, DSA_WIDTHS,
                       s_ref, m_ref, acc_ref)


def _dsa_attention(iq, ik2, misc, aq, ak_t, av):
    b, s, _ = aq.shape
    t = DSA_TILE
    return pl.pallas_call(
        _dsa_kernel,
        grid=(b, s // t, N_HEAD_BLOCKS),
        in_specs=[
            pl.BlockSpec((1, t, N_IDX_HEADS * IDX_DIM), lambda bb, i, hp: (bb, i, 0)),
            pl.BlockSpec((1, s, LANES), lambda bb, i, hp: (bb, 0, 0)),
            pl.BlockSpec((1, t, LANES), lambda bb, i, hp: (bb, i, 0)),
            pl.BlockSpec((1, t, LANES), lambda bb, i, hp: (bb, i, hp)),
            pl.BlockSpec((1, LANES, s), lambda bb, i, hp: (bb, hp, 0)),
            pl.BlockSpec((1, s, HEADS_PER_BLOCK * LANES), lambda bb, i, hp: (bb, 0, hp)),
        ],
        out_specs=pl.BlockSpec((1, t, LANES), lambda bb, i, hp: (bb, i, hp)),
        out_shape=jax.ShapeDtypeStruct((b, s, WIDTH), jnp.float32),
        scratch_shapes=[pltpu.VMEM((s, ATT_TILE), jnp.float32),
                        pltpu.VMEM((t, s), jnp.bfloat16),
                        pltpu.VMEM((SUBLANES, ATT_TILE), jnp.float32),
                        pltpu.VMEM((SUBLANES, ATT_TILE), jnp.float32)]
        + _attend_scratch(t, DSA_PANEL),
        compiler_params=pltpu.CompilerParams(
            dimension_semantics=("arbitrary", "arbitrary", "arbitrary"),
            vmem_limit_bytes=VMEM_LIMIT),
        name="dsa_attention",
    )(iq, ik2, misc, aq, ak_t, av)


def _fox_kernel(q_ref, kt_ref, v_ref, cq_ref, ck_ref, o_ref, s_ref, m_ref, acc_ref):
    t = q_ref.shape[1]
    cq = jnp.concatenate([cq_ref[0, 0, :, 0:1], cq_ref[0, 0, :, 1:2]], axis=0)

    def bias(s, keys, width, diagonal):
        ck = ck_ref[0, 0, :, keys]
        ck = jnp.concatenate([jnp.broadcast_to(ck[0:1], (t, width)),
                              jnp.broadcast_to(ck[1:2], (t, width))], axis=0)
        s = s + (cq - ck)
        if diagonal:
            below = (lax.broadcasted_iota(jnp.int32, (t, t), 1)
                     <= lax.broadcasted_iota(jnp.int32, (t, t), 0))
            s = jnp.where(jnp.concatenate([below, below], axis=0), s, NEG)
        return s

    o_ref[0] = _attend(_stack_heads(q_ref[0]), kt_ref, v_ref, bias, pl.program_id(2),
                       FOX_WIDTHS, s_ref, m_ref, acc_ref)


def _fox_attention(fq, fk_t, fv, cum_q, cum_k):
    b, s, _ = fq.shape
    t = FOX_TILE
    return pl.pallas_call(
        _fox_kernel,
        grid=(b, N_HEAD_BLOCKS, s // t),
        in_specs=[
            pl.BlockSpec((1, t, LANES), lambda bb, hp, i: (bb, i, hp)),
            pl.BlockSpec((1, LANES, s), lambda bb, hp, i: (bb, hp, 0)),
            pl.BlockSpec((1, s, HEADS_PER_BLOCK * LANES), lambda bb, hp, i: (bb, 0, hp)),
            pl.BlockSpec((1, 1, t, HEADS_PER_BLOCK), lambda bb, hp, i: (bb, hp, i, 0)),
            pl.BlockSpec((1, 1, HEADS_PER_BLOCK, s), lambda bb, hp, i: (bb, hp, 0, 0)),
        ],
        out_specs=pl.BlockSpec((1, t, LANES), lambda bb, hp, i: (bb, i, hp)),
        out_shape=jax.ShapeDtypeStruct((b, s, WIDTH), jnp.float32),
        scratch_shapes=_attend_scratch(t, s),
        compiler_params=pltpu.CompilerParams(
            dimension_semantics=("arbitrary", "arbitrary", "arbitrary"),
            vmem_limit_bytes=VMEM_LIMIT),
        name="fox_attention",
    )(fq, fk_t, fv, cum_q, cum_k)


def _out_kernel(x_ref, ya_ref, yb_ref, gain_ref, wg_ref, bm_ref, wa_ref, wb_ref,
                wo_ref, ngain_ref, o_ref):
    x = x_ref[...]
    h = _rms(x, gain_ref[...]).astype(jnp.bfloat16)
    g = jnp.dot(h, wg_ref[...], preferred_element_type=jnp.float32)
    a_gate = g[:, :WIDTH]
    f_gate = g[:, WIDTH:2 * WIDTH]
    m_logit = g[:, 2 * WIDTH:] + bm_ref[...]
    za = (ya_ref[...] * (a_gate * jax.nn.sigmoid(a_gate))).astype(jnp.bfloat16)
    zb = (yb_ref[...] * (f_gate * jax.nn.sigmoid(f_gate))).astype(jnp.bfloat16)
    ua = jnp.dot(za, wa_ref[...], preferred_element_type=jnp.float32)
    ub = jnp.dot(zb, wb_ref[...], preferred_element_type=jnp.float32)
    gates = jax.nn.sigmoid(m_logit)
    merged = gates[:, :D_MODEL] * ua + gates[:, D_MODEL:] * ub
    y = x + jnp.dot(merged.astype(jnp.bfloat16), wo_ref[...],
                    preferred_element_type=jnp.float32)
    o_ref[...] = _rms(y, ngain_ref[...])


def _output(x2, ya, yb, gain, w_gates, b_merge, w_a, w_b, w_o, next_gain):
    m = x2.shape[0]
    tm = PROJ_ROWS
    row = lambda i: (i, 0)
    fixed = lambda i: (0, 0)
    return pl.pallas_call(
        _out_kernel,
        grid=(m // tm,),
        in_specs=[
            pl.BlockSpec((tm, D_MODEL), row),
            pl.BlockSpec((tm, WIDTH), row),
            pl.BlockSpec((tm, WIDTH), row),
            pl.BlockSpec((1, D_MODEL), fixed),
            pl.BlockSpec(w_gates.shape, fixed),
            pl.BlockSpec((1, 2 * D_MODEL), fixed),
            pl.BlockSpec(w_a.shape, fixed),
            pl.BlockSpec(w_b.shape, fixed),
            pl.BlockSpec(w_o.shape, fixed),
            pl.BlockSpec((1, D_MODEL), fixed),
        ],
        out_specs=pl.BlockSpec((tm, D_MODEL), row),
        out_shape=jax.ShapeDtypeStruct((m, D_MODEL), jnp.float32),
        compiler_params=pltpu.CompilerParams(
            dimension_semantics=("arbitrary",), vmem_limit_bytes=VMEM_LIMIT),
        name="out_proj",
    )(x2, ya, yb, gain, w_gates, b_merge, w_a, w_b, w_o, next_gain)


def _split_w_in(w):
    sizes = (WIDTH, WIDTH, WIDTH, WIDTH, N_IDX_HEADS * IDX_DIM, IDX_DIM, N_IDX_HEADS,
             WIDTH, WIDTH, WIDTH, WIDTH, N_HEADS, 2 * D_MODEL)
    points = np.cumsum(sizes)[:-1]
    return jnp.split(w, points, axis=-1)


def kernel(x, positions, norm_gain, w_in, b_forget, b_merge, w_branch_dsa,
           w_branch_fox, w_out, final_gain):
    b, s, d = x.shape
    depth = w_in.shape[0]
    bf16 = jnp.bfloat16

    half = jnp.arange(ROPE_HALF, dtype=jnp.float32)
    inv_freq = ROPE_THETA ** (-half * 2.0 / ROPE_DIM)
    dim = np.arange(LANES) % HEAD_DIM
    invf = jnp.where(dim < ROPE_DIM, inv_freq[dim % ROPE_HALF], 0.0)[None, :]
    pos2 = positions.reshape(b * s, 1)

    assert depth == 1, "only a single layer is supported"
    x2 = x.reshape(b * s, d)
    normed = None
    for l in range(depth):
        (w_aq, w_ak, w_av, w_ag, w_iq, w_ik, w_iw,
         w_fq, w_fk, w_fv, w_fg, w_fl, w_m) = _split_w_in(w_in[l])
        pad = jnp.zeros((d, LANES - N_IDX_HEADS - N_HEADS), w_in.dtype)
        w_all = jnp.concatenate(
            [w_aq, w_ak, w_av, w_fq, w_fk, w_fv, w_iq, w_ik, w_ik, w_iw, w_fl, pad],
            axis=1).astype(bf16)
        w_gates = jnp.concatenate([w_ag, w_fg, w_m], axis=1).astype(bf16)
        bf_row = jnp.zeros((1, LANES), jnp.float32).at[
            0, MISC_CUM:MISC_CUM + N_HEADS].set(b_forget[l])
        gain = norm_gain[l][None, :]
        next_gain = (norm_gain[l + 1] if l + 1 < depth else final_gain)[None, :]

        aq, ak_t, av, fq, fk_t, fv, iq, ik2, misc = _project(
            x2, pos2, gain, invf, w_all, bf_row, s)
        shape3 = lambda a: a.reshape(b, s, a.shape[-1])
        ya = _dsa_attention(shape3(iq), shape3(ik2), shape3(misc), shape3(aq), ak_t,
                            shape3(av))
        cum = misc[:, MISC_CUM:MISC_CUM + N_HEADS].reshape(
            b, s, N_HEAD_BLOCKS, HEADS_PER_BLOCK)
        cum_q = cum.transpose(0, 2, 1, 3)
        cum_k = cum.transpose(0, 2, 3, 1)
        yb = _fox_attention(shape3(fq), fk_t, shape3(fv), cum_q, cum_k)
        normed = _output(x2, ya.reshape(b * s, WIDTH), yb.reshape(b * s, WIDTH),
                         gain, w_gates, b_merge[l][None, :],
                         w_branch_dsa[l].astype(bf16), w_branch_fox[l].astype(bf16),
                         w_out[l].astype(bf16), next_gain)
    return normed.reshape(b, s, d)
```

```python
import functools

import jax
import jax.numpy as jnp
import numpy as np
from jax import lax
from jax.experimental import pallas as pl
from jax.experimental.pallas import tpu as pltpu

D_MODEL = 1024
HEAD_DIM = 64
N_HEADS = 8
WIDTH = N_HEADS * HEAD_DIM
N_IDX_HEADS = 4
IDX_DIM = 64
TOPK_MAX = 256
ROPE_THETA = 500000.0
ROPE_DIM = HEAD_DIM // 4
ROPE_HALF = ROPE_DIM // 2
RMS_EPS = 1e-6
NEG = -1e30
LOG2_E = float(np.log2(np.e))

LANES = 128
SUBLANES = 8
HEADS_PER_BLOCK = LANES // HEAD_DIM
N_HEAD_BLOCKS = N_HEADS // HEADS_PER_BLOCK
VMEM_LIMIT = 56 * 1024 * 1024

PROJ_ROWS = 512
ATT_TILE = 256
INDEX_WIDTHS = (4096, 2048, 1024, 512, 256)
MASK_WIDTHS = (1024, 512, 256)
DSA_TILE = 512
DSA_PANEL = 4096
DSA_WIDTHS = (4096, 2048, 1024, 512)
FOX_TILE = 512
FOX_WIDTHS = (4096, 2048, 1024, 512)
COUNT_CHAINS = 4

TINY = float(np.finfo(np.float32).tiny)
ALL_THR = -3.0e38
BIG = 1.0e9
JUST_ABOVE = 1.0 + 2.0 ** -20
MAX_SEARCH = 400
INTERPOLATE_STEPS = 32
MIN_FRACTION = 1.0 / 32.0

MISC_IW = 0
MISC_CUM = N_IDX_HEADS


def _rms(x, gain):
    ms = jnp.mean(x * x, axis=-1, keepdims=True)
    return x * lax.rsqrt(ms + RMS_EPS) * gain


def _proj_kernel(x_ref, pos_ref, gain_ref, invf_ref, w_ref, bf_ref,
                 aq_ref, ak_ref, av_ref, fq_ref, fk_ref, fv_ref, iq_ref, ik_ref,
                 misc_ref, carry_ref, *, tiles_per_batch):
    tm = x_ref.shape[0]
    h = _rms(x_ref[...], gain_ref[...]).astype(jnp.bfloat16)

    ang = pos_ref[...].astype(jnp.float32) * invf_ref[...]
    cos = jnp.cos(ang)
    sin = jnp.sin(ang)
    d = lax.broadcasted_iota(jnp.int32, (tm, LANES), 1) % HEAD_DIM
    sin_lo = jnp.where(d < ROPE_HALF, -sin, 0.0)
    sin_hi = jnp.where(d >= ROPE_HALF, sin, 0.0)

    def rope(t):
        up = pltpu.roll(t, LANES - ROPE_HALF, 1)
        dn = pltpu.roll(t, ROPE_HALF, 1)
        return t * cos + up * sin_lo + dn * sin_hi

    def blocks_of(col0, width, use_rope, scale):
        r = jnp.dot(h, w_ref[:, col0:col0 + width],
                    preferred_element_type=jnp.float32)
        for c in range(width // LANES):
            t = r[:, c * LANES:(c + 1) * LANES]
            if use_rope:
                t = rope(t)
            if scale != 1.0:
                t = t * scale
            yield c, t

    def emit(out_ref, col0, width, use_rope, scale):
        for c, t in blocks_of(col0, width, use_rope, scale):
            out_ref[:, c * LANES:(c + 1) * LANES] = t.astype(out_ref.dtype)

    def emit_transposed(out_ref, col0, width, use_rope):
        for c, t in blocks_of(col0, width, use_rope, 1.0):
            out_ref[0, c * LANES:(c + 1) * LANES, :] = jnp.transpose(t).astype(
                out_ref.dtype)

    def emit_values(out_ref, col0):
        lane = lax.broadcasted_iota(jnp.int32, (tm, LANES), 1)
        tail = jnp.where(lane == HEAD_DIM, 1.0, 0.0)
        for c, t in blocks_of(col0, WIDTH, False, 1.0):
            pair = (t, pltpu.roll(t, HEAD_DIM, 1))
            for k in range(HEADS_PER_BLOCK):
                head = c * HEADS_PER_BLOCK + k
                out_ref[:, head * LANES:(head + 1) * LANES] = jnp.where(
                    lane < HEAD_DIM, pair[k], tail).astype(out_ref.dtype)

    qk_scale = HEAD_DIM ** -0.5 * LOG2_E
    emit(aq_ref, 0 * WIDTH, WIDTH, True, qk_scale)
    emit_transposed(ak_ref, 1 * WIDTH, WIDTH, True)
    emit_values(av_ref, 2 * WIDTH)
    emit(fq_ref, 3 * WIDTH, WIDTH, False, qk_scale)
    emit_transposed(fk_ref, 4 * WIDTH, WIDTH, False)
    emit_values(fv_ref, 5 * WIDTH)
    col = 6 * WIDTH
    emit(iq_ref, col, N_IDX_HEADS * IDX_DIM, True, IDX_DIM ** -0.5)
    col += N_IDX_HEADS * IDX_DIM
    emit(ik_ref, col, LANES, True, 1.0)
    col += LANES

    r = jnp.dot(h, w_ref[:, col:col + LANES], preferred_element_type=jnp.float32)
    z = r + bf_ref[...]
    logf = jnp.minimum(z, 0.0) - jnp.log1p(jnp.exp(-jnp.abs(z)))

    @pl.when(pl.program_id(0) % tiles_per_batch == 0)
    def _():
        carry_ref[...] = jnp.zeros_like(carry_ref)

    rows = lax.broadcasted_iota(jnp.int32, (tm, tm), 0)
    cols = lax.broadcasted_iota(jnp.int32, (tm, tm), 1)
    tri = jnp.where(cols <= rows, 1.0, 0.0).astype(jnp.bfloat16)
    hi = logf.astype(jnp.bfloat16)
    rem = logf - hi.astype(jnp.float32)
    mid = rem.astype(jnp.bfloat16)
    lo = (rem - mid.astype(jnp.float32)).astype(jnp.bfloat16)
    cum = (jnp.dot(tri, hi, preferred_element_type=jnp.float32)
           + jnp.dot(tri, mid, preferred_element_type=jnp.float32)
           + jnp.dot(tri, lo, preferred_element_type=jnp.float32)
           + carry_ref[...])
    carry_ref[...] = cum[tm - 1:tm, :]

    lane = lax.broadcasted_iota(jnp.int32, (tm, LANES), 1)
    misc = jnp.where(lane < MISC_CUM, r * (N_IDX_HEADS ** -0.5),
                     jnp.where(lane < MISC_CUM + N_HEADS, cum * LOG2_E, 0.0))
    misc_ref[...] = misc


def _project(x2, pos2, gain, invf, w_all, bf_row, seq):
    m = x2.shape[0]
    tm = PROJ_ROWS
    n_w = w_all.shape[1]
    row = lambda i: (i, 0)
    fixed = lambda i: (0, 0)
    bf16 = jnp.bfloat16
    tiles_per_batch = seq // tm
    batch = m // seq
    transposed = lambda i: (i // tiles_per_batch, 0, i % tiles_per_batch)

    def rows(width, dtype=bf16):
        return jax.ShapeDtypeStruct((m, width), dtype), pl.BlockSpec((tm, width), row)

    def keys(width):
        return (jax.ShapeDtypeStruct((batch, width, seq), bf16),
                pl.BlockSpec((1, width, tm), transposed))

    outs = [rows(WIDTH), keys(WIDTH), rows(N_HEADS * LANES),
            rows(WIDTH), keys(WIDTH), rows(N_HEADS * LANES),
            rows(N_IDX_HEADS * IDX_DIM), rows(LANES),
            rows(LANES, jnp.float32)]
    out_shape = [o[0] for o in outs]
    out_specs = [o[1] for o in outs]
    return pl.pallas_call(
        functools.partial(_proj_kernel, tiles_per_batch=tiles_per_batch),
        grid=(m // tm,),
        in_specs=[
            pl.BlockSpec((tm, D_MODEL), row),
            pl.BlockSpec((tm, 1), row),
            pl.BlockSpec((1, D_MODEL), fixed),
            pl.BlockSpec((1, LANES), fixed),
            pl.BlockSpec((D_MODEL, n_w), fixed),
            pl.BlockSpec((1, LANES), fixed),
        ],
        out_specs=out_specs,
        out_shape=out_shape,
        scratch_shapes=[pltpu.VMEM((1, LANES), jnp.float32)],
        compiler_params=pltpu.CompilerParams(
            dimension_semantics=("arbitrary",), vmem_limit_bytes=VMEM_LIMIT),
        name="in_proj",
    )(x2, pos2, gain, invf, w_all, bf_row)


def _half_select(x, first):
    lane = lax.broadcasted_iota(jnp.int32, x.shape, x.ndim - 1)
    keep = (lane < HEAD_DIM) if first else (lane >= HEAD_DIM)
    return jnp.where(keep, x, jnp.zeros_like(x))


def _stack_heads(q):
    return jnp.concatenate([_half_select(q, True), _half_select(q, False)], axis=0)


def _cover(limit, widths, fn, carry=0):
    widest = widths[0]
    n_widest = limit // widest

    def keys(start, width):
        return pl.ds(pl.multiple_of(start, width), width)

    carry = lax.fori_loop(
        0, n_widest, lambda j, c: fn(keys(j * widest, widest), widest, c), carry)
    start = n_widest * widest
    rest = limit - start
    for width in widths[1:]:
        part = rest & width
        carry = lax.cond(part != 0,
                         lambda c, start=start, width=width: fn(keys(start, width), width, c),
                         lambda c: c, carry)
        start = start + part
    return carry


def _attend(qs, kt_ref, v_ref, bias, i, widths, s_ref, m_ref, acc_ref):
    t = qs.shape[0] // HEADS_PER_BLOCK
    panel = s_ref.shape[1]
    seq = kt_ref.shape[2]
    assert widths[-1] == t and all(a == 2 * b for a, b in zip(widths, widths[1:]))
    assert widths[0] <= panel and 2 * panel >= seq

    def phase_a(keys, slot, width, diagonal):
        s = jnp.dot(qs, kt_ref[0, :, keys], preferred_element_type=jnp.float32)
        s = bias(s, keys, width, diagonal)
        s_ref[:, slot] = s
        m = m_ref[...]
        for c in range(width // LANES):
            m = jnp.maximum(m, s[:, c * LANES:(c + 1) * LANES])
        m_ref[...] = m

    def phase_b(keys, slot, m):
        p = jnp.exp2(s_ref[:, slot] - m).astype(jnp.bfloat16)
        for k in range(HEADS_PER_BLOCK):
            acc_ref[k * t:(k + 1) * t, :] += jnp.dot(
                p[k * t:(k + 1) * t], v_ref[0, keys, k * LANES:(k + 1) * LANES],
                preferred_element_type=jnp.float32)

    def run(base, below, with_diagonal):
        def keys_of(slot, width):
            return pl.ds(pl.multiple_of(base + slot.start, width), width)

        def maybe(flag, fn):
            if isinstance(flag, bool):
                if flag:
                    fn()
            else:
                pl.when(flag)(fn)

        diag_slot = pl.ds(pl.multiple_of(below, t), t)
        diag_keys = pl.ds(pl.multiple_of(i * t, t), t)

        def below_a(slot, width, carry):
            phase_a(keys_of(slot, width), slot, width, False)
            return carry

        m_ref[...] = jnp.full(m_ref.shape, NEG, jnp.float32)
        _cover(below, widths, below_a)
        maybe(with_diagonal, lambda: phase_a(diag_keys, diag_slot, t, True))
        m = jnp.max(m_ref[...], axis=1, keepdims=True)

        def below_b(slot, width, carry):
            phase_b(keys_of(slot, width), slot, m)
            return carry

        acc_ref[...] = jnp.zeros(acc_ref.shape, jnp.float32)
        _cover(below, widths, below_b)
        maybe(with_diagonal, lambda: phase_b(diag_keys, diag_slot, m))
        return m, acc_ref[...]

    limit = i * t
    if panel >= seq:
        _, acc = run(0, limit, True)
    else:
        m1, acc1 = run(0, jnp.minimum(limit, panel), limit < panel)
        m2, acc2 = lax.cond(
            limit >= panel,
            lambda: run(panel, limit - panel, True),
            lambda: (jnp.full((qs.shape[0], 1), NEG, jnp.float32),
                     jnp.zeros(acc_ref.shape, jnp.float32)))
        m = jnp.maximum(m1, m2)
        acc = acc1 * jnp.exp2(m1 - m) + acc2 * jnp.exp2(m2 - m)

    o = acc / acc[:, HEAD_DIM:HEAD_DIM + 1]
    lane = lax.broadcasted_iota(jnp.int32, (t, LANES), 1)
    return jnp.where(lane < HEAD_DIM, o[:t], pltpu.roll(o[t:], HEAD_DIM, 1))


def _attend_scratch(t, panel):
    rows = HEADS_PER_BLOCK * t
    return [pltpu.VMEM((rows, panel), jnp.float32),
            pltpu.VMEM((rows, LANES), jnp.float32),
            pltpu.VMEM((rows, LANES), jnp.float32)]


def _index_mask(iq, misc, ik_ref, sct_ref, sc_ref, out_rows, hi_ref, lo_ref, i):
    t = ATT_TILE
    top_k = float(TOPK_MAX)

    def fold(x, op, rows=SUBLANES):
        return op(x.reshape(x.shape[0] // rows, rows, t), axis=0)

    qt = []
    for h in range(N_IDX_HEADS):
        block = _half_select(iq[:, (h // 2) * LANES:(h // 2 + 1) * LANES], h % 2 == 0)
        qt.append(jnp.transpose(block))
    misc_t = jnp.transpose(misc)
    wh = [misc_t[MISC_IW + h:MISC_IW + h + 1, :] for h in range(N_IDX_HEADS)]

    def scores(keys):
        k = ik_ref[0, keys, :]
        sc = None
        for h in range(N_IDX_HEADS):
            rel = jnp.dot(k, qt[h], preferred_element_type=jnp.float32)
            term = wh[h] * jnp.maximum(rel, 0.0)
            sc = term if sc is None else sc + term
        return sc

    def track(for_max, for_min):
        hi_ref[...] = jnp.maximum(hi_ref[...], fold(for_max, jnp.max))
        lo_ref[...] = jnp.minimum(lo_ref[...], fold(for_min, jnp.min))

    def score_block(keys, width, carry):
        sc = scores(keys)
        sct_ref[keys, :] = sc
        track(sc, sc)
        return carry

    hi_ref[...] = jnp.full(hi_ref.shape, -jnp.inf, jnp.float32)
    lo_ref[...] = jnp.full(lo_ref.shape, jnp.inf, jnp.float32)
    _cover(i * t, INDEX_WIDTHS, score_block)
    diag = pl.ds(pl.multiple_of(i * t, t), t)
    sc = scores(diag)
    causal = (lax.broadcasted_iota(jnp.int32, (t, t), 0)
              <= lax.broadcasted_iota(jnp.int32, (t, t), 1))
    masked = jnp.where(causal, sc, -jnp.inf)
    sct_ref[diag, :] = masked
    track(masked, jnp.where(causal, sc, jnp.inf))

    def count_ge(thr):
        rows = COUNT_CHAINS * SUBLANES

        def block(keys, width, acc):
            for r in range(width // rows):
                part = pl.ds(pl.multiple_of(keys.start + r * rows, rows), rows)
                acc = acc + jnp.where(sct_ref[part, :] >= thr, 1.0, 0.0)
            return acc

        acc = _cover((i + 1) * t, INDEX_WIDTHS, block,
                     jnp.zeros((rows, t), jnp.float32))
        return jnp.sum(acc, axis=0, keepdims=True)

    def flag(cond):
        return jnp.where(cond, 1.0, 0.0)

    rmax = jnp.max(hi_ref[...], axis=0, keepdims=True)
    rmin = jnp.min(lo_ref[...], axis=0, keepdims=True)
    n_causal = (i * t + 1 + lax.broadcasted_iota(jnp.int32, (1, t), 1)
                ).astype(jnp.float32)
    c_pos = count_ge(jnp.full((1, t), TINY, jnp.float32))
    c_zero = lax.cond(jnp.min(c_pos) < top_k,
                      lambda: count_ge(jnp.zeros((1, t), jnp.float32)),
                      lambda: c_pos)
    take_all = n_causal <= top_k
    tie_zero = jnp.where(c_pos < top_k, flag(c_zero >= top_k), 0.0) > 0.5
    done = jnp.where(take_all, 1.0, jnp.where(tie_zero, 1.0, flag(c_pos == top_k)))
    thr = jnp.where(take_all, ALL_THR, jnp.where(tie_zero, 0.0, TINY))
    take = jnp.where(take_all, BIG, jnp.where(tie_zero, top_k - c_pos, BIG))
    positive = c_pos > top_k
    lo = jnp.where(positive, TINY, rmin)
    c_lo = jnp.where(positive, c_pos, n_causal)
    hi = jnp.where(positive, rmax * JUST_ABOVE + TINY, 0.0)
    c_hi = jnp.where(positive, 0.0, c_zero)
    log_target = float(np.log(top_k + 0.5))

    def cond(state):
        it, done = state[0], state[-1]
        return (it < MAX_SEARCH) & (jnp.min(done) < 0.5)

    def step(state):
        it, lo, hi, c_lo, c_hi, w_lo, w_hi, last, thr, take, done = state
        f_lo = (jnp.log(c_lo) - log_target) * w_lo
        f_hi = (log_target - jnp.log(jnp.maximum(c_hi, 0.5))) * w_hi
        frac = f_lo / (f_lo + f_hi)
        frac = jnp.clip(frac, MIN_FRACTION, 1.0 - MIN_FRACTION)
        frac = jnp.where(it < INTERPOLATE_STEPS, frac, 0.5)
        guess = lo + (hi - lo) * frac
        halfway = 0.5 * lo + 0.5 * hi
        mid = jnp.where(guess <= lo, halfway, jnp.where(guess >= hi, halfway, guess))
        spent = jnp.where(mid <= lo, 1.0, flag(mid >= hi))
        c = count_ge(mid)
        finish = (1.0 - done) * jnp.maximum(spent, flag(c == top_k)) > 0.5
        thr = jnp.where(finish, jnp.where(spent > 0.5, lo, mid), thr)
        take = jnp.where(finish, jnp.where(spent > 0.5, top_k - c_hi, BIG), take)
        above = c > top_k
        below = c < top_k
        lo = jnp.where(above, mid, lo)
        c_lo = jnp.where(above, c, c_lo)
        hi = jnp.where(below, mid, hi)
        c_hi = jnp.where(below, c, c_hi)
        w_hi = jnp.where(above, jnp.where(last > 0.5, 0.5 * w_hi, 1.0),
                         jnp.where(below, 1.0, w_hi))
        w_lo = jnp.where(below, jnp.where(last < -0.5, 0.5 * w_lo, 1.0),
                         jnp.where(above, 1.0, w_lo))
        last = jnp.where(above, 1.0, jnp.where(below, -1.0, last))
        done = jnp.where(finish, 1.0, done)
        return it + 1, lo, hi, c_lo, c_hi, w_lo, w_hi, last, thr, take, done

    one = jnp.ones((1, t), jnp.float32)
    _, lo, _, _, c_hi, _, _, _, thr, take, done = lax.while_loop(
        cond, step, (jnp.int32(0), lo, hi, c_lo, c_hi, one, one, 0.0 * one,
                     thr, take, done))
    thr = jnp.where(done < 0.5, lo, thr)
    take = jnp.where(done < 0.5, top_k - c_hi, take)

    ranked = jnp.max(flag(take < BIG)) > 0.5

    @pl.when(jnp.logical_not(ranked))
    def _():
        def block(keys, width, carry):
            for u in range(width // t):
                part = pl.ds(pl.multiple_of(keys.start + u * t, t), t)
                sc_ref[out_rows, part] = jnp.transpose(
                    jnp.where(sct_ref[part, :] >= thr, 0.0, NEG)).astype(sc_ref.dtype)
            return carry

        _cover((i + 1) * t, MASK_WIDTHS, block)

    @pl.when(ranked)
    def _():
        earlier = (lax.broadcasted_iota(jnp.int32, (t, t), 1)
                   < lax.broadcasted_iota(jnp.int32, (t, t), 0))
        earlier = jnp.where(earlier, 1.0, 0.0).astype(jnp.bfloat16)

        def tile(j, seen):
            keys = pl.ds(pl.multiple_of(j * t, t), t)
            s = sct_ref[keys, :]
            eq = flag(s == thr)
            rank = seen + jnp.dot(earlier, eq.astype(jnp.bfloat16),
                                  preferred_element_type=jnp.float32)
            sc_ref[out_rows, keys] = jnp.transpose(jnp.where(s > thr, 0.0, jnp.where(
                s == thr, jnp.where(rank < take, 0.0, NEG), NEG))).astype(sc_ref.dtype)
            return seen + jnp.sum(eq, axis=0, keepdims=True)

        lax.fori_loop(0, i + 1, tile, jnp.zeros((1, t), jnp.float32))


def _dsa_kernel(iq_ref, ik_ref, misc_ref, q_ref, kt_ref, v_ref, o_ref, sct_ref, sc_ref,
                hi_ref, lo_ref, s_ref, m_ref, acc_ref):
    i = pl.program_id(1)
    n_index = q_ref.shape[1] // ATT_TILE

    @pl.when(pl.program_id(2) == 0)
    def _():
        for k in range(n_index):
            rows = slice(k * ATT_TILE, (k + 1) * ATT_TILE)
            first = n_index * i + k
            _index_mask(iq_ref[0, rows, :], misc_ref[0, rows, :], ik_ref, sct_ref,
                        sc_ref, rows, hi_ref, lo_ref, first)
            if k + 1 < n_index:
                rest = (n_index - 1 - k) * ATT_TILE
                later = pl.ds(pl.multiple_of((first + 1) * ATT_TILE, ATT_TILE), rest)
                sc_ref[rows, later] = jnp.full((ATT_TILE, rest), NEG, sc_ref.dtype)

    def bias(s, keys, width, diagonal):
        mask = sc_ref[:, keys].astype(jnp.float32)
        return s + jnp.concatenate([mask, mask], axis=0)

    o_ref[0] = _attend(_stack_heads(q_ref[0]), kt_ref, v_ref, bias, i, DSA_WIDTHS,
                       s_ref, m_ref, acc_ref)


def _dsa_attention(iq, ik2, misc, aq, ak_t, av):
    b, s, _ = aq.shape
    t = DSA_TILE
    return pl.pallas_call(
        _dsa_kernel,
        grid=(b, s // t, N_HEAD_BLOCKS),
        in_specs=[
            pl.BlockSpec((1, t, N_IDX_HEADS * IDX_DIM), lambda bb, i, hp: (bb, i, 0)),
            pl.BlockSpec((1, s, LANES), lambda bb, i, hp: (bb, 0, 0),
                         pipeline_mode=pl.Buffered(1)),
            pl.BlockSpec((1, t, LANES), lambda bb, i, hp: (bb, i, 0)),
            pl.BlockSpec((1, t, LANES), lambda bb, i, hp: (bb, i, hp)),
            pl.BlockSpec((1, LANES, s), lambda bb, i, hp: (bb, hp, 0)),
            pl.BlockSpec((1, s, HEADS_PER_BLOCK * LANES), lambda bb, i, hp: (bb, 0, hp)),
        ],
        out_specs=pl.BlockSpec((1, t, LANES), lambda bb, i, hp: (bb, i, hp)),
        out_shape=jax.ShapeDtypeStruct((b, s, WIDTH), jnp.float32),
        scratch_shapes=[pltpu.VMEM((s, ATT_TILE), jnp.float32),
                        pltpu.VMEM((t, s), jnp.bfloat16),
                        pltpu.VMEM((SUBLANES, ATT_TILE), jnp.float32),
                        pltpu.VMEM((SUBLANES, ATT_TILE), jnp.float32)]
        + _attend_scratch(t, DSA_PANEL),
        compiler_params=pltpu.CompilerParams(
            dimension_semantics=("arbitrary", "arbitrary", "arbitrary"),
            vmem_limit_bytes=VMEM_LIMIT),
        name="dsa_attention",
    )(iq, ik2, misc, aq, ak_t, av)


def _fox_kernel(q_ref, kt_ref, v_ref, cq_ref, ck_ref, o_ref, s_ref, m_ref, acc_ref):
    t = q_ref.shape[1]
    cq = jnp.concatenate([cq_ref[0, 0, :, 0:1], cq_ref[0, 0, :, 1:2]], axis=0)

    def bias(s, keys, width, diagonal):
        ck = ck_ref[0, 0, :, keys]
        ck = jnp.concatenate([jnp.broadcast_to(ck[0:1], (t, width)),
                              jnp.broadcast_to(ck[1:2], (t, width))], axis=0)
        s = s + (cq - ck)
        if diagonal:
            below = (lax.broadcasted_iota(jnp.int32, (t, t), 1)
                     <= lax.broadcasted_iota(jnp.int32, (t, t), 0))
            s = jnp.where(jnp.concatenate([below, below], axis=0), s, NEG)
        return s

    o_ref[0] = _attend(_stack_heads(q_ref[0]), kt_ref, v_ref, bias, pl.program_id(2),
                       FOX_WIDTHS, s_ref, m_ref, acc_ref)


def _fox_attention(fq, fk_t, fv, cum_q, cum_k):
    b, s, _ = fq.shape
    t = FOX_TILE
    return pl.pallas_call(
        _fox_kernel,
        grid=(b, N_HEAD_BLOCKS, s // t),
        in_specs=[
            pl.BlockSpec((1, t, LANES), lambda bb, hp, i: (bb, i, hp)),
            pl.BlockSpec((1, LANES, s), lambda bb, hp, i: (bb, hp, 0)),
            pl.BlockSpec((1, s, HEADS_PER_BLOCK * LANES), lambda bb, hp, i: (bb, 0, hp)),
            pl.BlockSpec((1, 1, t, HEADS_PER_BLOCK), lambda bb, hp, i: (bb, hp, i, 0)),
            pl.BlockSpec((1, 1, HEADS_PER_BLOCK, s), lambda bb, hp, i: (bb, hp, 0, 0)),
        ],
        out_specs=pl.BlockSpec((1, t, LANES), lambda bb, hp, i: (bb, i, hp)),
        out_shape=jax.ShapeDtypeStruct((b, s, WIDTH), jnp.float32),
        scratch_shapes=_attend_scratch(t, s),
        compiler_params=pltpu.CompilerParams(
            dimension_semantics=("arbitrary", "arbitrary", "arbitrary"),
            vmem_limit_bytes=VMEM_LIMIT),
        name="fox_attention",
    )(fq, fk_t, fv, cum_q, cum_k)


def _out_kernel(x_ref, ya_ref, yb_ref, gain_ref, wg_ref, bm_ref, wa_ref, wb_ref,
                wo_ref, ngain_ref, o_ref):
    x = x_ref[...]
    h = _rms(x, gain_ref[...]).astype(jnp.bfloat16)
    g = jnp.dot(h, wg_ref[...], preferred_element_type=jnp.float32)
    a_gate = g[:, :WIDTH]
    f_gate = g[:, WIDTH:2 * WIDTH]
    m_logit = g[:, 2 * WIDTH:] + bm_ref[...]
    za = (ya_ref[...] * (a_gate * jax.nn.sigmoid(a_gate))).astype(jnp.bfloat16)
    zb = (yb_ref[...] * (f_gate * jax.nn.sigmoid(f_gate))).astype(jnp.bfloat16)
    ua = jnp.dot(za, wa_ref[...], preferred_element_type=jnp.float32)
    ub = jnp.dot(zb, wb_ref[...], preferred_element_type=jnp.float32)
    gates = jax.nn.sigmoid(m_logit)
    merged = gates[:, :D_MODEL] * ua + gates[:, D_MODEL:] * ub
    y = x + jnp.dot(merged.astype(jnp.bfloat16), wo_ref[...],
                    preferred_element_type=jnp.float32)
    o_ref[...] = _rms(y, ngain_ref[...])


def _output(x2, ya, yb, gain, w_gates, b_merge, w_a, w_b, w_o, next_gain):
    m = x2.shape[0]
    tm = PROJ_ROWS
    row = lambda i: (i, 0)
    fixed = lambda i: (0, 0)
    return pl.pallas_call(
        _out_kernel,
        grid=(m // tm,),
        in_specs=[
            pl.BlockSpec((tm, D_MODEL), row),
            pl.BlockSpec((tm, WIDTH), row),
            pl.BlockSpec((tm, WIDTH), row),
            pl.BlockSpec((1, D_MODEL), fixed),
            pl.BlockSpec(w_gates.shape, fixed),
            pl.BlockSpec((1, 2 * D_MODEL), fixed),
            pl.BlockSpec(w_a.shape, fixed),
            pl.BlockSpec(w_b.shape, fixed),
            pl.BlockSpec(w_o.shape, fixed),
            pl.BlockSpec((1, D_MODEL), fixed),
        ],
        out_specs=pl.BlockSpec((tm, D_MODEL), row),
        out_shape=jax.ShapeDtypeStruct((m, D_MODEL), jnp.float32),
        compiler_params=pltpu.CompilerParams(
            dimension_semantics=("arbitrary",), vmem_limit_bytes=VMEM_LIMIT),
        name="out_proj",
    )(x2, ya, yb, gain, w_gates, b_merge, w_a, w_b, w_o, next_gain)


def _split_w_in(w):
    sizes = (WIDTH, WIDTH, WIDTH, WIDTH, N_IDX_HEADS * IDX_DIM, IDX_DIM, N_IDX_HEADS,
             WIDTH, WIDTH, WIDTH, WIDTH, N_HEADS, 2 * D_MODEL)
    points = np.cumsum(sizes)[:-1]
    return jnp.split(w, points, axis=-1)


def kernel(x, positions, norm_gain, w_in, b_forget, b_merge, w_branch_dsa,
           w_branch_fox, w_out, final_gain):
    b, s, d = x.shape
    depth = w_in.shape[0]
    bf16 = jnp.bfloat16

    half = jnp.arange(ROPE_HALF, dtype=jnp.float32)
    inv_freq = ROPE_THETA ** (-half * 2.0 / ROPE_DIM)
    dim = np.arange(LANES) % HEAD_DIM
    invf = jnp.where(dim < ROPE_DIM, inv_freq[dim % ROPE_HALF], 0.0)[None, :]
    pos2 = positions.reshape(b * s, 1)

    assert depth == 1, "only a single layer is supported"
    x2 = x.reshape(b * s, d)
    normed = None
    for l in range(depth):
        (w_aq, w_ak, w_av, w_ag, w_iq, w_ik, w_iw,
         w_fq, w_fk, w_fv, w_fg, w_fl, w_m) = _split_w_in(w_in[l])
        pad = jnp.zeros((d, LANES - N_IDX_HEADS - N_HEADS), w_in.dtype)
        w_all = jnp.concatenate(
            [w_aq, w_ak, w_av, w_fq, w_fk, w_fv, w_iq, w_ik, w_ik, w_iw, w_fl, pad],
            axis=1).astype(bf16)
        w_gates = jnp.concatenate([w_ag, w_fg, w_m], axis=1).astype(bf16)
        bf_row = jnp.zeros((1, LANES), jnp.float32).at[
            0, MISC_CUM:MISC_CUM + N_HEADS].set(b_forget[l])
        gain = norm_gain[l][None, :]
        next_gain = (norm_gain[l + 1] if l + 1 < depth else final_gain)[None, :]

        aq, ak_t, av, fq, fk_t, fv, iq, ik2, misc = _project(
            x2, pos2, gain, invf, w_all, bf_row, s)
        shape3 = lambda a: a.reshape(b, s, a.shape[-1])
        ya = _dsa_attention(shape3(iq), shape3(ik2), shape3(misc), shape3(aq), ak_t,
                            shape3(av))
        cum = misc[:, MISC_CUM:MISC_CUM + N_HEADS].reshape(
            b, s, N_HEAD_BLOCKS, HEADS_PER_BLOCK)
        cum_q = cum.transpose(0, 2, 1, 3)
        cum_k = cum.transpose(0, 2, 3, 1)
        yb = _fox_attention(shape3(fq), fk_t, shape3(fv), cum_q, cum_k)
        normed = _output(x2, ya.reshape(b * s, WIDTH), yb.reshape(b * s, WIDTH),
                         gain, w_gates, b_merge[l][None, :],
                         w_branch_dsa[l].astype(bf16), w_branch_fox[l].astype(bf16),
                         w_out[l].astype(bf16), next_gain)
    return normed.reshape(b, s, d)
```
